```python
import math
import jax, jax.numpy as jnp
from jax import lax
import numpy as np

D_MODEL = 2048
BATCH = 1
SEQ = 16384
DEPTH = 4

D_MIX = D_MODEL
D_ATTN = D_MIX // 2
D_POOL = D_MIX // 4
D_SSM = D_MIX - D_ATTN - D_POOL
ATTN_HEAD_DIM = 128
N_ATTN_HEADS = D_ATTN // ATTN_HEAD_DIM
Q_BLOCK = 128
POOL_WINDOWS = (2, 4, 8, 16)
N_POOL_GROUPS = len(POOL_WINDOWS)
POOL_GROUP = D_POOL // N_POOL_GROUPS
SSM_GROUP = 16
N_SSM_GROUPS = D_SSM // SSM_GROUP
SSM_STATE = 64
DT_MIN = 1e-3
DT_MAX = 1e-1
NORM_EPS = 1e-6
PROJ_SIZES = (D_ATTN, D_ATTN, D_ATTN, D_ATTN, N_ATTN_HEADS, D_POOL, D_POOL, D_SSM, D_SSM)
D_IN = sum(PROJ_SIZES)
PROJ_SPLITS = tuple(int(v) for v in np.cumsum(PROJ_SIZES)[:-1])

kernel_name = 'hybrid_fox_pool_s5_adaln_trunk'

F32 = jnp.float32


def rms_norm(x, g):
    xf = x.astype(F32)
    return xf * lax.rsqrt(jnp.mean(xf * xf, axis=-1, keepdims=True) + NORM_EPS) * g.astype(F32)


def forgetting_attention(q, k, v, log_f):
    b, s, h, dh = q.shape
    nb = s // Q_BLOCK
    cum = jnp.cumsum(log_f, axis=1)

    def to_blocks(t):
        return t.astype(F32).reshape(b, nb, Q_BLOCK, h, dh).transpose(1, 0, 3, 2, 4)

    qb = to_blocks(q) * (dh ** -0.5)
    kb = to_blocks(k)
    vb = to_blocks(v)
    cb = cum.reshape(b, nb, Q_BLOCK, h).transpose(1, 0, 3, 2)
    causal = jnp.tril(jnp.ones((Q_BLOCK, Q_BLOCK), dtype=bool))

    def one_query_block(i):
        qi = qb[i]
        ci = cb[i]

        def body(j, carry):
            m, l, acc = carry
            kj = kb[j]
            vj = vb[j]
            cj = cb[j]
            logits = jnp.einsum('bhqd,bhkd->bhqk', qi, kj) + (ci[..., :, None] - cj[..., None, :])
            mask = jnp.logical_or(j < i, causal)
            logits = jnp.where(mask, logits, -jnp.inf)
            m_new = jnp.maximum(m, logits.max(axis=-1))
            p = jnp.exp(logits - m_new[..., None])
            corr = jnp.exp(m - m_new)
            l_new = l * corr + p.sum(axis=-1)
            acc_new = acc * corr[..., None] + jnp.einsum('bhqk,bhkd->bhqd', p, vj)
            return (m_new, l_new, acc_new)

        init = (jnp.full((b, h, Q_BLOCK), -jnp.inf, F32),
                jnp.zeros((b, h, Q_BLOCK), F32),
                jnp.zeros((b, h, Q_BLOCK, dh), F32))
        m, l, acc = lax.fori_loop(0, i + 1, body, init)
        return acc / l[..., None]

    out = lax.map(one_query_block, jnp.arange(nb))
    return out.transpose(1, 0, 3, 2, 4).reshape(b, s, h * dh)


def multiscale_pool(u, w_pool, pool_scale):
    b, s, _ = u.shape
    uf = u.astype(F32).reshape(b, s, N_POOL_GROUPS, POOL_GROUP)
    cs = jnp.pad(jnp.cumsum(uf, axis=1), ((0, 0), (1, 0), (0, 0), (0, 0)))
    t1 = jnp.arange(1, s + 1)
    outs = []
    for g, w in enumerate(POOL_WINDOWS):
        lo = jnp.maximum(t1 - w, 0)
        win_sum = cs[:, t1, g] - cs[:, lo, g]
        count = jnp.minimum(t1, w).astype(F32)[None, :, None]
        outs.append(win_sum / count - uf[:, :, g])
    pooled = jnp.stack(outs, axis=2)
    mixed = jnp.einsum('bsgc,gcd->bsgd', pooled, w_pool.astype(F32))
    return mixed.reshape(b, s, D_POOL) * pool_scale.astype(F32)


def s5_ssm_glu(u, lam_re, lam_im, b_re, b_im, c_re, c_im, d_skip, log_dt, w_glu, b_glu):
    b, s, _ = u.shape
    uf = u.astype(F32).reshape(b, s, N_SSM_GROUPS, SSM_GROUP)
    dt = jnp.exp(log_dt.astype(F32))[:, None]
    lr = lam_re.astype(F32)
    li = lam_im.astype(F32)
    mag = jnp.exp(lr * dt)
    ab_re = mag * jnp.cos(li * dt)
    ab_im = mag * jnp.sin(li * dt)
    den = lr * lr + li * li
    nr = ab_re - 1.0
    ni = ab_im
    z_re = (nr * lr + ni * li) / den
    z_im = (ni * lr - nr * li) / den
    br = b_re.astype(F32)
    bi = b_im.astype(F32)
    bb_re = z_re[..., None] * br - z_im[..., None] * bi
    bb_im = z_re[..., None] * bi + z_im[..., None] * br
    bu_re = jnp.einsum('gpc,bsgc->bsgp', bb_re, uf)
    bu_im = jnp.einsum('gpc,bsgc->bsgp', bb_im, uf)
    a_re = jnp.broadcast_to(ab_re, bu_re.shape)
    a_im = jnp.broadcast_to(ab_im, bu_im.shape)

    def combine(e1, e2):
        a1r, a1i, b1r, b1i = e1
        a2r, a2i, b2r, b2i = e2
        return (a2r * a1r - a2i * a1i,
                a2r * a1i + a2i * a1r,
                a2r * b1r - a2i * b1i + b2r,
                a2r * b1i + a2i * b1r + b2i)

    _, _, x_re, x_im = lax.associative_scan(combine, (a_re, a_im, bu_re, bu_im), axis=1)
    y = (jnp.einsum('gcp,bsgp->bsgc', c_re.astype(F32), x_re)
         - jnp.einsum('gcp,bsgp->bsgc', c_im.astype(F32), x_im)
         + d_skip.astype(F32) * uf)
    y = jax.nn.gelu(y.reshape(b, s, D_SSM))
    return y * jax.nn.sigmoid(y @ w_glu.astype(F32) + b_glu.astype(F32))


def setup_inputs(seed: int = 0) -> dict:
    key = jax.random.key(seed)
    ks = jax.random.split(key, 24)
    nrm = lambda k, shape, std: jax.random.normal(k, shape, F32) * std
    n_idx = jnp.arange(SSM_STATE, dtype=F32)
    return {
        'x': nrm(ks[0], (BATCH, SEQ, D_MODEL), 1.0),
        'c': nrm(ks[1], (BATCH, D_MODEL), 1.0),
        'norm_g': 1.0 + nrm(ks[2], (DEPTH, D_MODEL), 0.02),
        'w_ada': nrm(ks[3], (DEPTH, D_MODEL, 3 * D_MODEL), 0.5 * D_MODEL ** -0.5),
        'b_ada': nrm(ks[4], (DEPTH, 3 * D_MODEL), 0.02),
        'w_in': nrm(ks[5], (DEPTH, D_MODEL, D_IN), D_MODEL ** -0.5),
        'b_f': jax.random.uniform(ks[6], (DEPTH, N_ATTN_HEADS), F32, 1.0, 6.0),
        'w_pool': nrm(ks[7], (DEPTH, N_POOL_GROUPS, POOL_GROUP, POOL_GROUP), POOL_GROUP ** -0.5),
        'pool_scale': 1.0 + nrm(ks[8], (DEPTH, D_POOL), 0.02),
        'lam_re': -0.5 + nrm(ks[9], (DEPTH, N_SSM_GROUPS, SSM_STATE), 0.01),
        'lam_im': math.pi * n_idx + nrm(ks[10], (DEPTH, N_SSM_GROUPS, SSM_STATE), 0.01),
        'ssm_b_re': nrm(ks[11], (DEPTH, N_SSM_GROUPS, SSM_STATE, SSM_GROUP), (2 * SSM_GROUP) ** -0.5),
        'ssm_b_im': nrm(ks[12], (DEPTH, N_SSM_GROUPS, SSM_STATE, SSM_GROUP), (2 * SSM_GROUP) ** -0.5),
        'ssm_c_re': nrm(ks[13], (DEPTH, N_SSM_GROUPS, SSM_GROUP, SSM_STATE), SSM_STATE ** -0.5),
        'ssm_c_im': nrm(ks[14], (DEPTH, N_SSM_GROUPS, SSM_GROUP, SSM_STATE), SSM_STATE ** -0.5),
        'ssm_d': nrm(ks[15], (DEPTH, N_SSM_GROUPS, SSM_GROUP), 1.0),
        'log_dt': jax.random.uniform(ks[16], (DEPTH, N_SSM_GROUPS), F32, math.log(DT_MIN), math.log(DT_MAX)),
        'w_glu': nrm(ks[17], (DEPTH, D_SSM, D_SSM), D_SSM ** -0.5),
        'b_glu': nrm(ks[18], (DEPTH, D_SSM), 0.02),
        'w_out': nrm(ks[19], (DEPTH, D_MIX, D_MODEL), D_MIX ** -0.5),
        'final_g': 1.0 + nrm(ks[20], (D_MODEL,), 0.02),
    }


def reference(x, c, norm_g, w_ada, b_ada, w_in, b_f, w_pool, pool_scale, lam_re, lam_im,
              ssm_b_re, ssm_b_im, ssm_c_re, ssm_c_im, ssm_d, log_dt, w_glu, b_glu, w_out, final_g):
    b, s, _ = x.shape
    dtype = x.dtype
    c_act = jax.nn.silu(c)
    for l in range(DEPTH):
        mod = c_act @ w_ada[l] + b_ada[l]
        shift, scale, gate = jnp.split(mod.astype(F32), 3, axis=-1)
        h = (rms_norm(x, norm_g[l]) * (1.0 + scale[:, None, :]) + shift[:, None, :]).astype(dtype)
        proj = h @ w_in[l]
        q, k, v, g_attn, f_logit, u_pool, g_pool, u_ssm, g_ssm = jnp.split(proj, PROJ_SPLITS, axis=-1)
        log_f = jax.nn.log_sigmoid(f_logit.astype(F32) + b_f[l].astype(F32))
        shp = (b, s, N_ATTN_HEADS, ATTN_HEAD_DIM)
        y_attn = forgetting_attention(q.reshape(shp), k.reshape(shp), v.reshape(shp), log_f)
        y_pool = multiscale_pool(u_pool, w_pool[l], pool_scale[l])
        y_ssm = s5_ssm_glu(u_ssm, lam_re[l], lam_im[l], ssm_b_re[l], ssm_b_im[l], ssm_c_re[l],
                           ssm_c_im[l], ssm_d[l], log_dt[l], w_glu[l], b_glu[l])
        y = jnp.concatenate([
            y_attn * jax.nn.silu(g_attn.astype(F32)),
            y_pool * jax.nn.silu(g_pool.astype(F32)),
            y_ssm * jax.nn.silu(g_ssm.astype(F32)),
        ], axis=-1).astype(dtype)
        out = y @ w_out[l]
        x = (x.astype(F32) + gate[:, None, :] * out.astype(F32)).astype(dtype)
    return rms_norm(x, final_g).astype(dtype)
```

```python
import functools
import math

import jax
import jax.numpy as jnp
from jax import lax
from jax.experimental import pallas as pl
from jax.experimental.pallas import tpu as pltpu

F32 = jnp.float32
BF16 = jnp.bfloat16

N_HEADS = 8
HEAD_DIM = 128
POOL_WINDOWS = (2, 4, 8, 16)
POOL_GROUP = 128
POOL_HALO = 16
SSM_GROUP = 16
SSM_STATE = 64
SSM_SLAB_GROUPS = 8
NORM_EPS = 1e-6
LANES = 128
VMEM_LIMIT = 56 * 1024 * 1024


def _params(sem, vmem=VMEM_LIMIT):
    return pltpu.CompilerParams(dimension_semantics=sem, vmem_limit_bytes=vmem)


def _sigmoid(x):
    return 1.0 / (1.0 + jnp.exp(-x))


def _silu(x):
    return x * _sigmoid(x)


def _pick(n, pref):
    t = min(n, pref)
    while n % t:
        t //= 2
    return t


def _ada_kernel(c_ref, w_ref, b_ref, o_ref):
    ca = _silu(c_ref[...])
    o_ref[0] = jnp.sum(w_ref[0] * ca, axis=0, keepdims=True) + b_ref[0]


def _ada_mod(c, w_ada, b_ada):
    depth, d, n = w_ada.shape
    tn = _pick(n, 1024)
    return pl.pallas_call(
        _ada_kernel,
        grid=(depth, n // tn),
        in_specs=[
            pl.BlockSpec((d, 1), lambda l, j: (0, 0)),
            pl.BlockSpec((1, d, tn), lambda l, j: (l, 0, j)),
            pl.BlockSpec((1, 1, tn), lambda l, j: (l, 0, j)),
        ],
        out_specs=pl.BlockSpec((1, 1, tn), lambda l, j: (l, 0, j)),
        out_shape=jax.ShapeDtypeStruct((depth, 1, n), F32),
        compiler_params=_params(("arbitrary", "arbitrary")),
        name="ada_mod",
    )(c.reshape(d, 1), w_ada, b_ada.reshape(depth, 1, n))


def _inproj_kernel(x_ref, g_ref, mod_ref, w_ref, wf_ref, bf_ref,
                   qkv_ref, rest_ref, ft_ref, h_ref, carry_ref,
                   *, n_qkv_tiles, n_q_tiles, q_scale):
    i = pl.program_id(0)
    j = pl.program_id(1)
    tm = x_ref.shape[0]

    @pl.when(j == 0)
    def _():
        x = x_ref[...]
        ms = jnp.mean(x * x, axis=-1, keepdims=True)
        shift = mod_ref[0:1, :]
        scale = mod_ref[1:2, :]
        h = (x * lax.rsqrt(ms + NORM_EPS) * g_ref[...]) * (1.0 + scale) + shift
        hb = h.astype(BF16)
        h_ref[...] = hb
        f = jnp.dot(hb, wf_ref[...], preferred_element_type=F32) + bf_ref[...]
        logf = -(jnp.maximum(-f, 0.0) + jnp.log1p(jnp.exp(-jnp.abs(f))))
        row = lax.broadcasted_iota(jnp.int32, logf.shape, 0)
        cum = logf
        d = 1
        while d < tm:
            cum = cum + jnp.where(row >= d, pltpu.roll(cum, d, axis=0), 0.0)
            d *= 2

        @pl.when(i == 0)
        def _():
            carry_ref[...] = jnp.zeros_like(carry_ref)

        cum = cum + carry_ref[0:1, :]
        carry_ref[...] = jnp.broadcast_to(cum[tm - 1:tm, :], carry_ref.shape)
        cum_t = cum.T
        for hh in range(N_HEADS):
            ft_ref[hh] = cum_t[hh:hh + 1, :]

    proj = jnp.dot(h_ref[...], w_ref[...], preferred_element_type=F32)

    @pl.when(j < n_qkv_tiles)
    def _():
        s = jnp.where(j < n_q_tiles, q_scale, 1.0).astype(F32)
        qkv_ref[...] = (proj * s).astype(BF16)

    @pl.when(j >= n_qkv_tiles)
    def _():
        rest_ref[...] = proj


def _inproj(x, g, mod, w_main, w_f, b_f_row, *, d_attn):
    s, d = x.shape
    n = w_main.shape[1]
    n_qkv = 3 * d_attn
    tm = _pick(s, 1024)
    tn = 512
    n_qkv_tiles = n_qkv // tn
    kern = functools.partial(_inproj_kernel, n_qkv_tiles=n_qkv_tiles,
                             n_q_tiles=d_attn // tn, q_scale=HEAD_DIM ** -0.5)
    return pl.pallas_call(
        kern,
        grid=(s // tm, n // tn),
        in_specs=[
            pl.BlockSpec((tm, d), lambda i, j: (i, 0)),
            pl.BlockSpec((1, d), lambda i, j: (0, 0)),
            pl.BlockSpec((3, d), lambda i, j: (0, 0)),
            pl.BlockSpec((d, tn), lambda i, j: (0, j)),
            pl.BlockSpec((d, LANES), lambda i, j: (0, 0)),
            pl.BlockSpec((1, LANES), lambda i, j: (0, 0)),
        ],
        out_specs=[
            pl.BlockSpec((tm, tn), lambda i, j: (i, jnp.minimum(j, n_qkv_tiles - 1))),
            pl.BlockSpec((tm, tn), lambda i, j: (i, jnp.maximum(j - n_qkv_tiles, 0))),
            pl.BlockSpec((N_HEADS, 1, tm), lambda i, j: (0, 0, i)),
        ],
        out_shape=[
            jax.ShapeDtypeStruct((s, n_qkv), BF16),
            jax.ShapeDtypeStruct((s, n - n_qkv), F32),
            jax.ShapeDtypeStruct((N_HEADS, 1, s), F32),
        ],
        scratch_shapes=[pltpu.VMEM((tm, d), BF16), pltpu.VMEM((8, LANES), F32)],
        compiler_params=_params(("arbitrary", "arbitrary")),
        name="inproj",
    )(x, g, mod, w_main, w_f, b_f_row)


def _attn_kernel(q_ref, k_ref, v_ref, ft_ref, g_ref, o_ref, bias_ref, *, tq, tk):
    h = pl.program_id(0)
    i = pl.program_id(1)
    seq = k_ref.shape[0]

    @pl.when(i == 0)
    def _():
        def fill(b, carry):
            off = pl.multiple_of(b * LANES, LANES)
            row = ft_ref[0, :, pl.ds(off, LANES)]
            bias_ref[pl.ds(off, LANES), :] = -(jnp.broadcast_to(row, (LANES, LANES)).T)
            return carry
        lax.fori_loop(0, seq // LANES, fill, 0)

    q = q_ref[...]
    q_off = pl.multiple_of(i * tq, tq)
    gref = bias_ref[pl.ds(q_off, 1), :]

    def step(j, carry, masked):
        m, l, acc = carry
        k_off = pl.multiple_of(j * tk, tk)
        kc = k_ref[pl.ds(k_off, tk), :]
        vc = v_ref[pl.ds(k_off, tk), :]
        s = lax.dot_general(kc, q, (((1,), (1,)), ((), ())),
                            preferred_element_type=F32)
        b = bias_ref[pl.ds(k_off, tk), :] - gref
        s = s + jnp.concatenate([b] * (tq // LANES), axis=1)
        if masked:
            key = lax.broadcasted_iota(jnp.int32, (tk, tq), 0)
            qry = lax.broadcasted_iota(jnp.int32, (tk, tq), 1)
            s = jnp.where(key <= qry, s, -jnp.inf)
        m_new = jnp.maximum(m, jnp.max(s, axis=0, keepdims=True))
        p = jnp.exp(s - m_new)
        corr = jnp.exp(m - m_new)
        l_new = l * corr + jnp.sum(p, axis=0, keepdims=True)
        pv = lax.dot_general(vc, p.astype(BF16), (((0,), (0,)), ((), ())),
                             preferred_element_type=F32)
        return m_new, l_new, acc * corr + pv

    init = (jnp.full((1, tq), -jnp.inf, F32), jnp.zeros((1, tq), F32),
            jnp.zeros((HEAD_DIM, tq), F32))
    carry = lax.fori_loop(0, i, lambda j, c: step(j, c, False), init)
    m, l, acc = step(i, carry, True)
    y = (acc / l).T
    o_ref[...] = (y * _silu(g_ref[...])).astype(o_ref.dtype)


def _attention(qkv, ft, rest, *, d_attn):
    s = qkv.shape[0]
    tq = _pick(s, 1024)
    nh = d_attn // HEAD_DIM
    kern = functools.partial(_attn_kernel, tq=tq, tk=tq)
    return pl.pallas_call(
        kern,
        grid=(nh, s // tq),
        in_specs=[
            pl.BlockSpec((tq, HEAD_DIM), lambda h, i: (i, h)),
            pl.BlockSpec((s, HEAD_DIM), lambda h, i: (0, nh + h)),
            pl.BlockSpec((s, HEAD_DIM), lambda h, i: (0, 2 * nh + h)),
            pl.BlockSpec((1, 1, s), lambda h, i: (h, 0, 0)),
            pl.BlockSpec((tq, HEAD_DIM), lambda h, i: (i, h)),
        ],
        out_specs=pl.BlockSpec((tq, HEAD_DIM), lambda h, i: (i, h)),
        out_shape=jax.ShapeDtypeStruct((s, d_attn), BF16),
        scratch_shapes=[pltpu.VMEM((s, LANES), F32)],
        compiler_params=_params(("arbitrary", "arbitrary")),
        name="fox_attention",
    )(qkv, qkv, qkv, ft, rest)


def _pool_kernel(u_ref, prev_ref, g_ref, w_ref, sc_ref, o_ref):
    i = pl.program_id(0)
    tp = u_ref.shape[0]
    u = u_ref[...]
    prev = jnp.where(i > 0, prev_ref[...], 0.0)
    t1 = lax.broadcasted_iota(jnp.int32, (tp, POOL_GROUP), 0) + (i * tp + 1)
    outs = []
    for g, w in enumerate(POOL_WINDOWS):
        lo, hi = g * POOL_GROUP, (g + 1) * POOL_GROUP
        ug = u[:, lo:hi]
        ext = jnp.concatenate([prev[:, lo:hi], ug], axis=0)
        win = ext
        span = 1
        while span < w:
            win = win + pltpu.roll(win, span, axis=0)
            span *= 2
        win = win[POOL_HALO:]
        cnt = jnp.minimum(t1, w).astype(F32)
        pooled = win / cnt - ug
        mixed = jnp.dot(pooled.astype(BF16), w_ref[g], preferred_element_type=F32)
        outs.append(mixed)
    mixed = jnp.concatenate(outs, axis=1) * sc_ref[...]
    o_ref[...] = (mixed * _silu(g_ref[...])).astype(o_ref.dtype)


def _pool(rest, w_pool, pool_scale, *, col_u, col_g, d_pool):
    s = rest.shape[0]
    tp = _pick(s, 1024)
    cu, cg = col_u // d_pool, col_g // d_pool
    halo_blocks = tp // POOL_HALO
    return pl.pallas_call(
        _pool_kernel,
        grid=(s // tp,),
        in_specs=[
            pl.BlockSpec((tp, d_pool), lambda i: (i, cu)),
            pl.BlockSpec((POOL_HALO, d_pool),
                         lambda i: (jnp.maximum(i * halo_blocks - 1, 0), cu)),
            pl.BlockSpec((tp, d_pool), lambda i: (i, cg)),
            pl.BlockSpec(w_pool.shape, lambda i: (0, 0, 0)),
            pl.BlockSpec((1, d_pool), lambda i: (0, 0)),
        ],
        out_specs=pl.BlockSpec((tp, d_pool), lambda i: (i, 0)),
        out_shape=jax.ShapeDtypeStruct((s, d_pool), BF16),
        compiler_params=_params(("arbitrary",)),
        name="pool",
    )(rest, rest, rest, w_pool, pool_scale)


def _ssm_prep_kernel(lr_ref, li_ref, ldt_ref, br_ref, bi_ref, cr_ref, ci_ref,
                     a_ref, bb_ref, cc_ref):
    lr = lr_ref[0]
    li = li_ref[0]
    dt = jnp.exp(ldt_ref[0])
    mag = jnp.exp(lr * dt)
    ab_re = mag * jnp.cos(li * dt)
    ab_im = mag * jnp.sin(li * dt)
    den = lr * lr + li * li
    nr = ab_re - 1.0
    ni = ab_im
    z_re = (nr * lr + ni * li) / den
    z_im = (ni * lr - nr * li) / den
    a_ref[0] = jnp.concatenate([ab_re, ab_im], axis=0)
    n_slab = br_ref.shape[1]
    w = br_ref.shape[3]
    for k in range(n_slab):
        zr = z_re[:, k * w:(k + 1) * w]
        zi = z_im[:, k * w:(k + 1) * w]
        br = br_ref[0, k]
        bi = bi_ref[0, k]
        bb_ref[0, k] = jnp.concatenate([zr * br - zi * bi, zr * bi + zi * br],
                                       axis=1).astype(BF16)
        cc_ref[0, k] = jnp.concatenate([cr_ref[0, k], -ci_ref[0, k]], axis=0).astype(BF16)


def _ssm_prep(lam_re, lam_im, log_dt, b_re, b_im, c_re, c_im):
    depth, ng, ns = lam_re.shape
    gc = b_re.shape[-1]
    n_slab = ng // SSM_SLAB_GROUPS
    eye = jnp.eye(SSM_SLAB_GROUPS, dtype=F32)
    n_state = ng * ns
    sw = SSM_SLAB_GROUPS * ns

    def place_b(b):
        b = b.reshape(depth, n_slab, SSM_SLAB_GROUPS, ns, gc).transpose(0, 1, 2, 4, 3)
        return (b[:, :, :, :, None, :] * eye[None, None, :, None, :, None]).reshape(
            depth, n_slab, SSM_SLAB_GROUPS * gc, sw)

    def place_c(c):
        c = c.reshape(depth, n_slab, SSM_SLAB_GROUPS, gc, ns).transpose(0, 1, 2, 4, 3)
        return (c[:, :, :, :, None, :] * eye[None, None, :, None, :, None]).reshape(
            depth, n_slab, sw, SSM_SLAB_GROUPS * gc)

    row = lambda v: v.reshape(depth, 1, n_state)
    ldt = jnp.broadcast_to(log_dt[:, :, None], (depth, ng, ns))
    cw = SSM_SLAB_GROUPS * gc
    spec_row = pl.BlockSpec((1, 1, n_state), lambda l: (l, 0, 0))
    spec_b = pl.BlockSpec((1, n_slab, cw, sw), lambda l: (l, 0, 0, 0))
    spec_c = pl.BlockSpec((1, n_slab, sw, cw), lambda l: (l, 0, 0, 0))
    return pl.pallas_call(
        _ssm_prep_kernel,
        grid=(depth,),
        in_specs=[spec_row, spec_row, spec_row, spec_b, spec_b, spec_c, spec_c],
        out_specs=[
            pl.BlockSpec((1, 2, n_state), lambda l: (l, 0, 0)),
            pl.BlockSpec((1, n_slab, cw, 2 * sw), lambda l: (l, 0, 0, 0)),
            pl.BlockSpec((1, n_slab, 2 * sw, cw), lambda l: (l, 0, 0, 0)),
        ],
        out_shape=[
            jax.ShapeDtypeStruct((depth, 2, n_state), F32),
            jax.ShapeDtypeStruct((depth, n_slab, cw, 2 * sw), BF16),
            jax.ShapeDtypeStruct((depth, n_slab, 2 * sw, cw), BF16),
        ],
        compiler_params=_params(("arbitrary",)),
        name="ssm_prep",
    )(row(lam_re), row(lam_im), row(ldt), place_b(b_re), place_b(b_im),
      place_c(c_re), place_c(c_im))


def _gelu_tanh(y):
    c = math.sqrt(2.0 / math.pi)
    return 0.5 * y * (1.0 + jnp.tanh(c * (y + 0.044715 * (y * y * y))))


def _ssm_kernel(u_ref, g_ref, a_ref, bb_ref, cc_ref, d_ref, wg_ref, bg_ref,
                o_ref, carry_ref):
    i = pl.program_id(0)
    ts = u_ref.shape[0]
    n_slab = bb_ref.shape[0]
    cw = bb_ref.shape[1]
    sw = bb_ref.shape[2] // 2

    @pl.when(i == 0)
    def _():
        carry_ref[...] = jnp.zeros_like(carry_ref)

    u = u_ref[...]
    ub = u.astype(BF16)
    row = lax.broadcasted_iota(jnp.int32, (ts, sw), 0)
    ys = []
    for k in range(n_slab):
        a_re = a_ref[0:1, k * sw:(k + 1) * sw]
        a_im = a_ref[1:2, k * sw:(k + 1) * sw]
        bu = jnp.dot(ub[:, k * cw:(k + 1) * cw], bb_ref[k], preferred_element_type=F32)
        xr = bu[:, :sw]
        xi = bu[:, sw:]
        c_re = carry_ref[0:1, k * sw:(k + 1) * sw]
        c_im = carry_ref[1:2, k * sw:(k + 1) * sw]
        first = row == 0
        xr = xr + jnp.where(first, a_re * c_re - a_im * c_im, 0.0)
        xi = xi + jnp.where(first, a_re * c_im + a_im * c_re, 0.0)
        p_re, p_im = a_re, a_im
        d = 1
        while d < ts:
            keep = row >= d
            sr = jnp.where(keep, pltpu.roll(xr, d, axis=0), 0.0)
            si = jnp.where(keep, pltpu.roll(xi, d, axis=0), 0.0)
            xr, xi = xr + (p_re * sr - p_im * si), xi + (p_re * si + p_im * sr)
            p_re, p_im = p_re * p_re - p_im * p_im, 2.0 * (p_re * p_im)
            d *= 2
        carry_ref[0:1, k * sw:(k + 1) * sw] = xr[ts - 1:ts, :]
        carry_ref[1:2, k * sw:(k + 1) * sw] = xi[ts - 1:ts, :]
        xc = jnp.concatenate([xr, xi], axis=1).astype(BF16)
        ys.append(jnp.dot(xc, cc_ref[k], preferred_element_type=F32))
    y = jnp.concatenate(ys, axis=1) + d_ref[...] * u
    y = _gelu_tanh(y)
    z = jnp.dot(y.astype(BF16), wg_ref[...], preferred_element_type=F32) + bg_ref[...]
    o_ref[...] = (y * _sigmoid(z) * _silu(g_ref[...])).astype(o_ref.dtype)


def _ssm(rest, a, bb, cc, d_row, w_glu, b_glu, *, col_u, col_g, d_ssm):
    s = rest.shape[0]
    ts = _pick(s, 256)
    cu, cg = col_u // d_ssm, col_g // d_ssm
    n_state = a.shape[1]
    return pl.pallas_call(
        _ssm_kernel,
        grid=(s // ts,),
        in_specs=[
            pl.BlockSpec((ts, d_ssm), lambda i: (i, cu)),
            pl.BlockSpec((ts, d_ssm), lambda i: (i, cg)),
            pl.BlockSpec(a.shape, lambda i: (0, 0)),
            pl.BlockSpec(bb.shape, lambda i: (0, 0, 0)),
            pl.BlockSpec(cc.shape, lambda i: (0, 0, 0)),
            pl.BlockSpec((1, d_ssm), lambda i: (0, 0)),
            pl.BlockSpec(w_glu.shape, lambda i: (0, 0)),
            pl.BlockSpec((1, d_ssm), lambda i: (0, 0)),
        ],
        out_specs=pl.BlockSpec((ts, d_ssm), lambda i: (i, 0)),
        out_shape=jax.ShapeDtypeStruct((s, d_ssm), BF16),
        scratch_shapes=[pltpu.VMEM((2, n_state), F32)],
        compiler_params=_params(("arbitrary",)),
        name="ssm",
    )(rest, rest, a, bb, cc, d_row, w_glu, b_glu)


def _outproj_kernel(ya_ref, yp_ref, ys_ref, w_ref, x_ref, mod_ref, o_ref, y_ref):
    j = pl.program_id(1)
    da = ya_ref.shape[1]
    dp = yp_ref.shape[1]

    @pl.when(j == 0)
    def _():
        y_ref[:, 0:da] = ya_ref[...]
        y_ref[:, da:da + dp] = yp_ref[...]
        y_ref[:, da + dp:] = ys_ref[...]

    out = jnp.dot(y_ref[...], w_ref[...], preferred_element_type=F32)
    o_ref[...] = x_ref[...] + mod_ref[...] * out


def _outproj(ya, yp, ysm, w_out, x, gate):
    s, d = x.shape
    dm = w_out.shape[0]
    tm = _pick(s, 1024)
    tn = 512
    return pl.pallas_call(
        _outproj_kernel,
        grid=(s // tm, d // tn),
        in_specs=[
            pl.BlockSpec((tm, ya.shape[1]), lambda i, j: (i, 0)),
            pl.BlockSpec((tm, yp.shape[1]), lambda i, j: (i, 0)),
            pl.BlockSpec((tm, ysm.shape[1]), lambda i, j: (i, 0)),
            pl.BlockSpec((dm, tn), lambda i, j: (0, j)),
            pl.BlockSpec((tm, tn), lambda i, j: (i, j)),
            pl.BlockSpec((1, tn), lambda i, j: (0, j)),
        ],
        out_specs=pl.BlockSpec((tm, tn), lambda i, j: (i, j)),
        out_shape=jax.ShapeDtypeStruct((s, d), F32),
        scratch_shapes=[pltpu.VMEM((tm, dm), BF16)],
        compiler_params=_params(("arbitrary", "arbitrary")),
        name="outproj",
    )(ya, yp, ysm, w_out, x, gate)


def _final_norm_kernel(x_ref, g_ref, o_ref):
    x = x_ref[...]
    ms = jnp.mean(x * x, axis=-1, keepdims=True)
    o_ref[...] = x * lax.rsqrt(ms + NORM_EPS) * g_ref[...]


def _final_norm(x, g):
    s, d = x.shape
    tm = _pick(s, 512)
    return pl.pallas_call(
        _final_norm_kernel,
        grid=(s // tm,),
        in_specs=[pl.BlockSpec((tm, d), lambda i: (i, 0)),
                  pl.BlockSpec((1, d), lambda i: (0, 0))],
        out_specs=pl.BlockSpec((tm, d), lambda i: (i, 0)),
        out_shape=jax.ShapeDtypeStruct((s, d), F32),
        compiler_params=_params(("arbitrary",)),
        name="final_norm",
    )(x, g)


def kernel(x, c, norm_g, w_ada, b_ada, w_in, b_f, w_pool, pool_scale, lam_re, lam_im,
           ssm_b_re, ssm_b_im, ssm_c_re, ssm_c_im, ssm_d, log_dt, w_glu, b_glu, w_out,
           final_g):
    b, s, d = x.shape
    assert b == 1
    depth = w_in.shape[0]
    d_pool = pool_scale.shape[1]
    d_ssm = b_glu.shape[1]
    d_attn = N_HEADS * HEAD_DIM
    n_f = b_f.shape[1]
    assert n_f == N_HEADS and w_in.shape[2] == 4 * d_attn + n_f + 2 * d_pool + 2 * d_ssm

    w_main = jnp.concatenate([w_in[:, :, :4 * d_attn], w_in[:, :, 4 * d_attn + n_f:]],
                             axis=2).astype(BF16)
    w_f = jnp.pad(w_in[:, :, 4 * d_attn:4 * d_attn + n_f],
                  ((0, 0), (0, 0), (0, LANES - n_f))).astype(BF16)
    b_f_row = jnp.pad(b_f, ((0, 0), (0, LANES - n_f))).reshape(depth, 1, LANES)
    col_up = d_attn
    col_gp = col_up + d_pool
    col_us = col_gp + d_pool
    col_gs = col_us + d_ssm

    mod = _ada_mod(c, w_ada, b_ada).reshape(depth, 3, d)
    a_all, bb_all, cc_all = _ssm_prep(lam_re, lam_im, log_dt, ssm_b_re, ssm_b_im,
                                      ssm_c_re, ssm_c_im)
    w_pool_b = w_pool.astype(BF16)
    w_glu_b = w_glu.astype(BF16)
    w_out_b = w_out.astype(BF16)

    xs = x.reshape(s, d)
    for l in range(depth):
        qkv, rest, ft = _inproj(xs, norm_g[l].reshape(1, d), mod[l], w_main[l], w_f[l],
                                b_f_row[l], d_attn=d_attn)
        ya = _attention(qkv, ft, rest, d_attn=d_attn)
        yp = _pool(rest, w_pool_b[l], pool_scale[l].reshape(1, d_pool),
                   col_u=col_up, col_g=col_gp, d_pool=d_pool)
        ysm = _ssm(rest, a_all[l], bb_all[l], cc_all[l], ssm_d[l].reshape(1, d_ssm),
                   w_glu_b[l], b_glu[l].reshape(1, d_ssm),
                   col_u=col_us, col_g=col_gs, d_ssm=d_ssm)
        xs = _outproj(ya, yp, ysm, w_out_b[l], xs, mod[l, 2:3, :])
    return _final_norm(xs, final_g.reshape(1, d)).reshape(b, s, d).astype(x.dtype)
```

```python
import functools
import math

import jax
import jax.numpy as jnp
from jax import lax
from jax.experimental import pallas as pl
from jax.experimental.pallas import tpu as pltpu

F32 = jnp.float32
BF16 = jnp.bfloat16

N_HEADS = 8
HEAD_DIM = 128
POOL_WINDOWS = (2, 4, 8, 16)
POOL_GROUP = 128
POOL_HALO = 16
SSM_GROUP = 16
SSM_STATE = 64
SSM_SLAB_GROUPS = 8
NORM_EPS = 1e-6
LANES = 128
VMEM_LIMIT = 56 * 1024 * 1024


def _params(sem, vmem=VMEM_LIMIT):
    return pltpu.CompilerParams(dimension_semantics=sem, vmem_limit_bytes=vmem)


def _sigmoid(x):
    return 1.0 / (1.0 + jnp.exp(-x))


def _silu(x):
    return x * _sigmoid(x)


def _pick(n, pref):
    t = min(n, pref)
    while n % t:
        t //= 2
    return t


def _ada_kernel(c_ref, w_ref, b_ref, o_ref):
    ca = _silu(c_ref[...])
    o_ref[0] = jnp.sum(w_ref[0] * ca, axis=0, keepdims=True) + b_ref[0]


def _ada_mod(c, w_ada, b_ada):
    depth, d, n = w_ada.shape
    tn = _pick(n, 1024)
    return pl.pallas_call(
        _ada_kernel,
        grid=(depth, n // tn),
        in_specs=[
            pl.BlockSpec((d, 1), lambda l, j: (0, 0)),
            pl.BlockSpec((1, d, tn), lambda l, j: (l, 0, j)),
            pl.BlockSpec((1, 1, tn), lambda l, j: (l, 0, j)),
        ],
        out_specs=pl.BlockSpec((1, 1, tn), lambda l, j: (l, 0, j)),
        out_shape=jax.ShapeDtypeStruct((depth, 1, n), F32),
        compiler_params=_params(("arbitrary", "arbitrary")),
        name="ada_mod",
    )(c.reshape(d, 1), w_ada, b_ada.reshape(depth, 1, n))


def _inproj_kernel(x_ref, g_ref, mod_ref, w_ref, wf_ref, bf_ref,
                   qkv_ref, rest_ref, ft_ref, h_ref, carry_ref,
                   *, n_qkv_tiles, n_q_tiles, q_scale):
    i = pl.program_id(0)
    j = pl.program_id(1)
    tm = x_ref.shape[0]

    @pl.when(j == 0)
    def _():
        x = x_ref[...]
        ms = jnp.mean(x * x, axis=-1, keepdims=True)
        shift = mod_ref[0:1, :]
        scale = mod_ref[1:2, :]
        h = (x * lax.rsqrt(ms + NORM_EPS) * g_ref[...]) * (1.0 + scale) + shift
        hb = h.astype(BF16)
        h_ref[...] = hb
        f = jnp.dot(hb, wf_ref[...], preferred_element_type=F32) + bf_ref[...]
        logf = -(jnp.maximum(-f, 0.0) + jnp.log1p(jnp.exp(-jnp.abs(f))))
        row = lax.broadcasted_iota(jnp.int32, logf.shape, 0)
        cum = logf
        d = 1
        while d < tm:
            cum = cum + jnp.where(row >= d, pltpu.roll(cum, d, axis=0), 0.0)
            d *= 2

        @pl.when(i == 0)
        def _():
            carry_ref[...] = jnp.zeros_like(carry_ref)

        cum = cum + carry_ref[0:1, :]
        carry_ref[...] = jnp.broadcast_to(cum[tm - 1:tm, :], carry_ref.shape)
        cum_t = cum.T
        for hh in range(N_HEADS):
            ft_ref[hh] = cum_t[hh:hh + 1, :]

    proj = jnp.dot(h_ref[...], w_ref[...], preferred_element_type=F32)

    @pl.when(j < n_qkv_tiles)
    def _():
        s = jnp.where(j < n_q_tiles, q_scale, 1.0).astype(F32)
        qkv_ref[...] = (proj * s).astype(BF16)

    @pl.when(j >= n_qkv_tiles)
    def _():
        rest_ref[...] = proj


def _inproj(x, g, mod, w_main, w_f, b_f_row, *, d_attn):
    s, d = x.shape
    n = w_main.shape[1]
    n_qkv = 3 * d_attn
    tm = _pick(s, 1024)
    tn = 512
    n_qkv_tiles = n_qkv // tn
    kern = functools.partial(_inproj_kernel, n_qkv_tiles=n_qkv_tiles,
                             n_q_tiles=d_attn // tn,
                             q_scale=HEAD_DIM ** -0.5 * math.log2(math.e))
    return pl.pallas_call(
        kern,
        grid=(s // tm, n // tn),
        in_specs=[
            pl.BlockSpec((tm, d), lambda i, j: (i, 0)),
            pl.BlockSpec((1, d), lambda i, j: (0, 0)),
            pl.BlockSpec((3, d), lambda i, j: (0, 0)),
            pl.BlockSpec((d, tn), lambda i, j: (0, j)),
            pl.BlockSpec((d, LANES), lambda i, j: (0, 0)),
            pl.BlockSpec((1, LANES), lambda i, j: (0, 0)),
        ],
        out_specs=[
            pl.BlockSpec((tm, tn), lambda i, j: (i, jnp.minimum(j, n_qkv_tiles - 1))),
            pl.BlockSpec((tm, tn), lambda i, j: (i, jnp.maximum(j - n_qkv_tiles, 0))),
            pl.BlockSpec((N_HEADS, 1, tm), lambda i, j: (0, 0, i)),
        ],
        out_shape=[
            jax.ShapeDtypeStruct((s, n_qkv), BF16),
            jax.ShapeDtypeStruct((s, n - n_qkv), F32),
            jax.ShapeDtypeStruct((N_HEADS, 1, s), F32),
        ],
        scratch_shapes=[pltpu.VMEM((tm, d), BF16), pltpu.VMEM((8, LANES), F32)],
        compiler_params=_params(("arbitrary", "arbitrary")),
        name="inproj",
    )(x, g, mod, w_main, w_f, b_f_row)


AUG_TERMS = 3
ONES_ROWS = 16
Q_STRIP = 1024
LOG2E = math.log2(math.e)


SAFE_EXP = 60.0
ZERO_EXP = -150.0
NORM_SLACK = 1.02


def _tile_lanes(row, n):
    return jnp.concatenate([row] * (n // LANES), axis=1)


def _attn_kernel(q_ref, k_ref, v_ref, ft_ref, g_ref, o_ref,
                 kaug_ref, vt_ref, base_ref, bend_ref, kall_ref, acc_ref, *, tq, tk):
    i = pl.program_id(1)
    seq = k_ref.shape[0]
    dh = HEAD_DIM
    lane = lax.broadcasted_iota(jnp.int32, (LANES, LANES), 1)
    lane_row = lax.broadcasted_iota(jnp.int32, (1, LANES), 1)
    ones_sq = jnp.ones((LANES, LANES), BF16)

    @pl.when(i == 0)
    def _():
        vt_ref[dh:, :] = jnp.ones((ONES_ROWS, seq), BF16)
        kall_ref[...] = jnp.zeros_like(kall_ref)
        bend_ref[...] = jnp.zeros_like(bend_ref)

        def fill(c, carry):
            base = None
            col = None
            for bb in range(tk // LANES):
                off = pl.multiple_of(c * tk + bb * LANES, LANES)
                row = ft_ref[0, :, pl.ds(off, LANES)]
                col = (-LOG2E) * jnp.broadcast_to(row, (LANES, LANES)).T
                if bb == 0:
                    base = col[0:1, :]
                    base_ref[pl.ds(c, 1), :] = base
                rel = col - base
                hi = rel.astype(BF16).astype(F32)
                mid = (rel - hi).astype(BF16).astype(F32)
                lo = (rel - hi - mid).astype(BF16).astype(F32)
                aug = jnp.where(lane == 0, hi, jnp.where(lane == 1, mid,
                                                         jnp.where(lane == 2, lo, 0.0)))
                kb = k_ref[pl.ds(off, LANES), :]
                kaug_ref[pl.ds(off, LANES), 0:dh] = kb
                kaug_ref[pl.ds(off, LANES), dh:] = aug.astype(BF16)
                vt_ref[0:dh, pl.ds(off, LANES)] = (
                    v_ref[pl.ds(off, LANES), :].astype(F32).T.astype(BF16))
                kf = kb.astype(F32)
                n2 = jnp.dot((kf * kf).astype(BF16), ones_sq, preferred_element_type=F32)
                kall_ref[0:1, :] = jnp.maximum(kall_ref[0:1, :],
                                               jnp.max(n2, axis=0, keepdims=True))
            bend_ref[0:1, :] = jnp.where(lane_row == c, col[LANES - 1:LANES, :],
                                         bend_ref[0:1, :])
            return carry
        lax.fori_loop(0, seq // tk, fill, 0)

    lane_q = lax.broadcasted_iota(jnp.int32, (tq, LANES), 1)
    q = q_ref[...]
    q_aug = jnp.concatenate(
        [q, jnp.where(lane_q < AUG_TERMS, 1.0, 0.0).astype(BF16)], axis=1)
    acc_ref[...] = jnp.zeros_like(acc_ref)
    base_q = base_ref[pl.ds(i, 1), :]
    n_strip = tq // Q_STRIP

    def chunk(j):
        k_off = pl.multiple_of(j * tk, tk)
        kc = kaug_ref[pl.ds(k_off, tk), :]
        vc = vt_ref[:, pl.ds(k_off, tk)]
        delta = base_ref[pl.ds(j, 1), :] - base_q
        return kc, vc, delta

    def scores(kc, lo, hi):
        return lax.dot_general(kc, q_aug[lo:hi, :], (((1,), (1,)), ((), ())),
                               preferred_element_type=F32)

    def online_step(j, m, masked):
        kc, vc, delta = chunk(j)
        delta = _tile_lanes(delta, Q_STRIP)
        m_out = []
        for st in range(n_strip):
            lo, hi = st * Q_STRIP, (st + 1) * Q_STRIP
            s = scores(kc, lo, hi)
            if masked:
                key = lax.broadcasted_iota(jnp.int32, (tk, Q_STRIP), 0)
                qry = lax.broadcasted_iota(jnp.int32, (tk, Q_STRIP), 1) + lo
                s = jnp.where(key <= qry, s, -jnp.inf)
            m_old = m[st] - delta
            m_new = jnp.maximum(m_old, jnp.max(s, axis=0, keepdims=True))
            p = jnp.exp2(s - m_new).astype(BF16)
            corr = jnp.exp2(m_old - m_new)
            pv = jnp.dot(vc, p, preferred_element_type=F32)
            acc_ref[:, lo:hi] = acc_ref[:, lo:hi] * corr + pv
            m_out.append(m_new + delta)
        return tuple(m_out)

    m0 = tuple(jnp.full((1, Q_STRIP), -jnp.inf, F32) for _ in range(n_strip))
    m_d = online_step(i, m0, True)
    m_row = jnp.concatenate(m_d, axis=1)

    qf = q.astype(F32)
    qn2 = lax.dot_general(jnp.ones((8, dh), BF16), (qf * qf).astype(BF16),
                          (((1,), (1,)), ((), ())), preferred_element_type=F32)[0:1, :]
    qk_bound = jnp.sqrt(qn2 * _tile_lanes(kall_ref[0:1, :], tq)) * NORM_SLACK + 1.0
    slack = jnp.max(qk_bound - m_row)
    live = jnp.logical_and(slack + (bend_ref[0:1, :] - base_q) >= ZERO_EXP, lane_row < i)
    n_live = jnp.sum(live.astype(jnp.int32))

    @pl.when(slack <= SAFE_EXP)
    def _():
        def fast_step(jj, carry):
            kc, vc, delta = chunk(i - 1 - jj)
            ref = m_row - _tile_lanes(delta, tq)
            for st in range(n_strip):
                lo, hi = st * Q_STRIP, (st + 1) * Q_STRIP
                p = jnp.exp2(scores(kc, lo, hi) - ref[:, lo:hi]).astype(BF16)
                acc_ref[:, lo:hi] += jnp.dot(vc, p, preferred_element_type=F32)
            return carry
        lax.fori_loop(0, n_live, fast_step, 0)

    @pl.when(slack > SAFE_EXP)
    def _():
        lax.fori_loop(0, i, lambda jj, m: online_step(i - 1 - jj, m, False), m_d)

    y = (acc_ref[0:dh, :] / acc_ref[dh:dh + 1, :]).T
    o_ref[...] = (y * _silu(g_ref[...])).astype(o_ref.dtype)


def _attention(qkv, ft, rest, *, d_attn):
    s = qkv.shape[0]
    tq = _pick(s, 1024)
    nh = d_attn // HEAD_DIM
    kern = functools.partial(_attn_kernel, tq=tq, tk=tq)
    return pl.pallas_call(
        kern,
        grid=(nh, s // tq),
        in_specs=[
            pl.BlockSpec((tq, HEAD_DIM), lambda h, i: (i, h)),
            pl.BlockSpec((s, HEAD_DIM), lambda h, i: (0, nh + h)),
            pl.BlockSpec((s, HEAD_DIM), lambda h, i: (0, 2 * nh + h)),
            pl.BlockSpec((1, 1, s), lambda h, i: (h, 0, 0)),
            pl.BlockSpec((tq, HEAD_DIM), lambda h, i: (i, h)),
        ],
        out_specs=pl.BlockSpec((tq, HEAD_DIM), lambda h, i: (i, h)),
        out_shape=jax.ShapeDtypeStruct((s, d_attn), BF16),
        scratch_shapes=[
            pltpu.VMEM((s, 2 * HEAD_DIM), BF16),
            pltpu.VMEM((HEAD_DIM + ONES_ROWS, s), BF16),
            pltpu.VMEM((max(s // tq, 8), LANES), F32),
            pltpu.VMEM((8, LANES), F32),
            pltpu.VMEM((8, LANES), F32),
            pltpu.VMEM((HEAD_DIM + ONES_ROWS, tq), F32),
        ],
        compiler_params=_params(("arbitrary", "arbitrary")),
        name="fox_attention",
    )(qkv, qkv, qkv, ft, rest)


def _pool_kernel(u_ref, prev_ref, g_ref, w_ref, sc_ref, o_ref):
    i = pl.program_id(0)
    tp = u_ref.shape[0]
    u = u_ref[...]
    prev = jnp.where(i > 0, prev_ref[...], 0.0)
    t1 = lax.broadcasted_iota(jnp.int32, (tp, POOL_GROUP), 0) + (i * tp + 1)
    outs = []
    for g, w in enumerate(POOL_WINDOWS):
        lo, hi = g * POOL_GROUP, (g + 1) * POOL_GROUP
        ug = u[:, lo:hi]
        ext = jnp.concatenate([prev[:, lo:hi], ug], axis=0)
        win = ext
        span = 1
        while span < w:
            win = win + pltpu.roll(win, span, axis=0)
            span *= 2
        win = win[POOL_HALO:]
        cnt = jnp.minimum(t1, w).astype(F32)
        pooled = win / cnt - ug
        mixed = jnp.dot(pooled.astype(BF16), w_ref[g], preferred_element_type=F32)
        outs.append(mixed)
    mixed = jnp.concatenate(outs, axis=1) * sc_ref[...]
    o_ref[...] = (mixed * _silu(g_ref[...])).astype(o_ref.dtype)


def _pool(rest, w_pool, pool_scale, *, col_u, col_g, d_pool):
    s = rest.shape[0]
    tp = _pick(s, 1024)
    cu, cg = col_u // d_pool, col_g // d_pool
    halo_blocks = tp // POOL_HALO
    return pl.pallas_call(
        _pool_kernel,
        grid=(s // tp,),
        in_specs=[
            pl.BlockSpec((tp, d_pool), lambda i: (i, cu)),
            pl.BlockSpec((POOL_HALO, d_pool),
                         lambda i: (jnp.maximum(i * halo_blocks - 1, 0), cu)),
            pl.BlockSpec((tp, d_pool), lambda i: (i, cg)),
            pl.BlockSpec(w_pool.shape, lambda i: (0, 0, 0)),
            pl.BlockSpec((1, d_pool), lambda i: (0, 0)),
        ],
        out_specs=pl.BlockSpec((tp, d_pool), lambda i: (i, 0)),
        out_shape=jax.ShapeDtypeStruct((s, d_pool), BF16),
        compiler_params=_params(("arbitrary",)),
        name="pool",
    )(rest, rest, rest, w_pool, pool_scale)


def _ssm_prep_kernel(lr_ref, li_ref, ldt_ref, br_ref, bi_ref, cr_ref, ci_ref,
                     a_ref, bb_ref, cc_ref):
    lr = lr_ref[0]
    li = li_ref[0]
    dt = jnp.exp(ldt_ref[0])
    mag = jnp.exp(lr * dt)
    ab_re = mag * jnp.cos(li * dt)
    ab_im = mag * jnp.sin(li * dt)
    den = lr * lr + li * li
    nr = ab_re - 1.0
    ni = ab_im
    z_re = (nr * lr + ni * li) / den
    z_im = (ni * lr - nr * li) / den
    a_ref[0] = jnp.concatenate([ab_re, ab_im], axis=0)
    n_slab = br_ref.shape[1]
    w = br_ref.shape[3]
    for k in range(n_slab):
        zr = z_re[:, k * w:(k + 1) * w]
        zi = z_im[:, k * w:(k + 1) * w]
        br = br_ref[0, k]
        bi = bi_ref[0, k]
        bb_ref[0, k] = jnp.concatenate([zr * br - zi * bi, zr * bi + zi * br],
                                       axis=1).astype(BF16)
        cc_ref[0, k] = jnp.concatenate([cr_ref[0, k], -ci_ref[0, k]], axis=0).astype(BF16)


def _ssm_prep(lam_re, lam_im, log_dt, b_re, b_im, c_re, c_im):
    depth, ng, ns = lam_re.shape
    gc = b_re.shape[-1]
    n_slab = ng // SSM_SLAB_GROUPS
    eye = jnp.eye(SSM_SLAB_GROUPS, dtype=F32)
    n_state = ng * ns
    sw = SSM_SLAB_GROUPS * ns

    def place_b(b):
        b = b.reshape(depth, n_slab, SSM_SLAB_GROUPS, ns, gc).transpose(0, 1, 2, 4, 3)
        return (b[:, :, :, :, None, :] * eye[None, None, :, None, :, None]).reshape(
            depth, n_slab, SSM_SLAB_GROUPS * gc, sw)

    def place_c(c):
        c = c.reshape(depth, n_slab, SSM_SLAB_GROUPS, gc, ns).transpose(0, 1, 2, 4, 3)
        return (c[:, :, :, :, None, :] * eye[None, None, :, None, :, None]).reshape(
            depth, n_slab, sw, SSM_SLAB_GROUPS * gc)

    row = lambda v: v.reshape(depth, 1, n_state)
    ldt = jnp.broadcast_to(log_dt[:, :, None], (depth, ng, ns))
    cw = SSM_SLAB_GROUPS * gc
    spec_row = pl.BlockSpec((1, 1, n_state), lambda l: (l, 0, 0))
    spec_b = pl.BlockSpec((1, n_slab, cw, sw), lambda l: (l, 0, 0, 0))
    spec_c = pl.BlockSpec((1, n_slab, sw, cw), lambda l: (l, 0, 0, 0))
    return pl.pallas_call(
        _ssm_prep_kernel,
        grid=(depth,),
        in_specs=[spec_row, spec_row, spec_row, spec_b, spec_b, spec_c, spec_c],
        out_specs=[
            pl.BlockSpec((1, 2, n_state), lambda l: (l, 0, 0)),
            pl.BlockSpec((1, n_slab, cw, 2 * sw), lambda l: (l, 0, 0, 0)),
            pl.BlockSpec((1, n_slab, 2 * sw, cw), lambda l: (l, 0, 0, 0)),
        ],
        out_shape=[
            jax.ShapeDtypeStruct((depth, 2, n_state), F32),
            jax.ShapeDtypeStruct((depth, n_slab, cw, 2 * sw), BF16),
            jax.ShapeDtypeStruct((depth, n_slab, 2 * sw, cw), BF16),
        ],
        compiler_params=_params(("arbitrary",)),
        name="ssm_prep",
    )(row(lam_re), row(lam_im), row(ldt), place_b(b_re), place_b(b_im),
      place_c(c_re), place_c(c_im))


def _gelu_tanh(y):
    c = math.sqrt(2.0 / math.pi)
    return 0.5 * y * (1.0 + jnp.tanh(c * (y + 0.044715 * (y * y * y))))


def _ssm_kernel(u_ref, g_ref, a_ref, bb_ref, cc_ref, d_ref, wg_ref, bg_ref,
                o_ref, carry_ref):
    i = pl.program_id(0)
    ts = u_ref.shape[0]
    n_slab = bb_ref.shape[0]
    cw = bb_ref.shape[1]
    sw = bb_ref.shape[2] // 2

    @pl.when(i == 0)
    def _():
        carry_ref[...] = jnp.zeros_like(carry_ref)

    u = u_ref[...]
    ub = u.astype(BF16)
    row = lax.broadcasted_iota(jnp.int32, (ts, sw), 0)
    ys = []
    for k in range(n_slab):
        a_re = a_ref[0:1, k * sw:(k + 1) * sw]
        a_im = a_ref[1:2, k * sw:(k + 1) * sw]
        bu = jnp.dot(ub[:, k * cw:(k + 1) * cw], bb_ref[k], preferred_element_type=F32)
        xr = bu[:, :sw]
        xi = bu[:, sw:]
        c_re = carry_ref[0:1, k * sw:(k + 1) * sw]
        c_im = carry_ref[1:2, k * sw:(k + 1) * sw]
        first = row == 0
        xr = xr + jnp.where(first, a_re * c_re - a_im * c_im, 0.0)
        xi = xi + jnp.where(first, a_re * c_im + a_im * c_re, 0.0)
        p_re, p_im = a_re, a_im
        d = 1
        while d < ts:
            keep = row >= d
            sr = jnp.where(keep, pltpu.roll(xr, d, axis=0), 0.0)
            si = jnp.where(keep, pltpu.roll(xi, d, axis=0), 0.0)
            xr, xi = xr + (p_re * sr - p_im * si), xi + (p_re * si + p_im * sr)
            p_re, p_im = p_re * p_re - p_im * p_im, 2.0 * (p_re * p_im)
            d *= 2
        carry_ref[0:1, k * sw:(k + 1) * sw] = xr[ts - 1:ts, :]
        carry_ref[1:2, k * sw:(k + 1) * sw] = xi[ts - 1:ts, :]
        xc = jnp.concatenate([xr, xi], axis=1).astype(BF16)
        ys.append(jnp.dot(xc, cc_ref[k], preferred_element_type=F32))
    y = jnp.concatenate(ys, axis=1) + d_ref[...] * u
    y = _gelu_tanh(y)
    z = jnp.dot(y.astype(BF16), wg_ref[...], preferred_element_type=F32) + bg_ref[...]
    o_ref[...] = (y * _sigmoid(z) * _silu(g_ref[...])).astype(o_ref.dtype)


def _ssm(rest, a, bb, cc, d_row, w_glu, b_glu, *, col_u, col_g, d_ssm):
    s = rest.shape[0]
    ts = _pick(s, 256)
    cu, cg = col_u // d_ssm, col_g // d_ssm
    n_state = a.shape[1]
    return pl.pallas_call(
        _ssm_kernel,
        grid=(s // ts,),
        in_specs=[
            pl.BlockSpec((ts, d_ssm), lambda i: (i, cu)),
            pl.BlockSpec((ts, d_ssm), lambda i: (i, cg)),
            pl.BlockSpec(a.shape, lambda i: (0, 0)),
            pl.BlockSpec(bb.shape, lambda i: (0, 0, 0)),
            pl.BlockSpec(cc.shape, lambda i: (0, 0, 0)),
            pl.BlockSpec((1, d_ssm), lambda i: (0, 0)),
            pl.BlockSpec(w_glu.shape, lambda i: (0, 0)),
            pl.BlockSpec((1, d_ssm), lambda i: (0, 0)),
        ],
        out_specs=pl.BlockSpec((ts, d_ssm), lambda i: (i, 0)),
        out_shape=jax.ShapeDtypeStruct((s, d_ssm), BF16),
        scratch_shapes=[pltpu.VMEM((2, n_state), F32)],
        compiler_params=_params(("arbitrary",)),
        name="ssm",
    )(rest, rest, a, bb, cc, d_row, w_glu, b_glu)


def _outproj_kernel(ya_ref, yp_ref, ys_ref, w_ref, x_ref, mod_ref, o_ref, y_ref):
    j = pl.program_id(1)
    da = ya_ref.shape[1]
    dp = yp_ref.shape[1]

    @pl.when(j == 0)
    def _():
        y_ref[:, 0:da] = ya_ref[...]
        y_ref[:, da:da + dp] = yp_ref[...]
        y_ref[:, da + dp:] = ys_ref[...]

    out = jnp.dot(y_ref[...], w_ref[...], preferred_element_type=F32)
    o_ref[...] = x_ref[...] + mod_ref[...] * out


def _outproj(ya, yp, ysm, w_out, x, gate):
    s, d = x.shape
    dm = w_out.shape[0]
    tm = _pick(s, 1024)
    tn = 512
    return pl.pallas_call(
        _outproj_kernel,
        grid=(s // tm, d // tn),
        in_specs=[
            pl.BlockSpec((tm, ya.shape[1]), lambda i, j: (i, 0)),
            pl.BlockSpec((tm, yp.shape[1]), lambda i, j: (i, 0)),
            pl.BlockSpec((tm, ysm.shape[1]), lambda i, j: (i, 0)),
            pl.BlockSpec((dm, tn), lambda i, j: (0, j)),
            pl.BlockSpec((tm, tn), lambda i, j: (i, j)),
            pl.BlockSpec((1, tn), lambda i, j: (0, j)),
        ],
        out_specs=pl.BlockSpec((tm, tn), lambda i, j: (i, j)),
        out_shape=jax.ShapeDtypeStruct((s, d), F32),
        scratch_shapes=[pltpu.VMEM((tm, dm), BF16)],
        compiler_params=_params(("arbitrary", "arbitrary")),
        name="outproj",
    )(ya, yp, ysm, w_out, x, gate)


def _final_norm_kernel(x_ref, g_ref, o_ref):
    x = x_ref[...]
    ms = jnp.mean(x * x, axis=-1, keepdims=True)
    o_ref[...] = x * lax.rsqrt(ms + NORM_EPS) * g_ref[...]


def _final_norm(x, g):
    s, d = x.shape
    tm = _pick(s, 512)
    return pl.pallas_call(
        _final_norm_kernel,
        grid=(s // tm,),
        in_specs=[pl.BlockSpec((tm, d), lambda i: (i, 0)),
                  pl.BlockSpec((1, d), lambda i: (0, 0))],
        out_specs=pl.BlockSpec((tm, d), lambda i: (i, 0)),
        out_shape=jax.ShapeDtypeStruct((s, d), F32),
        compiler_params=_params(("arbitrary",)),
        name="final_norm",
    )(x, g)


def kernel(x, c, norm_g, w_ada, b_ada, w_in, b_f, w_pool, pool_scale, lam_re, lam_im,
           ssm_b_re, ssm_b_im, ssm_c_re, ssm_c_im, ssm_d, log_dt, w_glu, b_glu, w_out,
           final_g):
    b, s, d = x.shape
    assert b == 1
    depth = w_in.shape[0]
    d_pool = pool_scale.shape[1]
    d_ssm = b_glu.shape[1]
    d_attn = N_HEADS * HEAD_DIM
    n_f = b_f.shape[1]
    assert n_f == N_HEADS and w_in.shape[2] == 4 * d_attn + n_f + 2 * d_pool + 2 * d_ssm

    w_main = jnp.concatenate([w_in[:, :, :4 * d_attn], w_in[:, :, 4 * d_attn + n_f:]],
                             axis=2).astype(BF16)
    w_f = jnp.pad(w_in[:, :, 4 * d_attn:4 * d_attn + n_f],
                  ((0, 0), (0, 0), (0, LANES - n_f))).astype(BF16)
    b_f_row = jnp.pad(b_f, ((0, 0), (0, LANES - n_f))).reshape(depth, 1, LANES)
    col_up = d_attn
    col_gp = col_up + d_pool
    col_us = col_gp + d_pool
    col_gs = col_us + d_ssm

    mod = _ada_mod(c, w_ada, b_ada).reshape(depth, 3, d)
    a_all, bb_all, cc_all = _ssm_prep(lam_re, lam_im, log_dt, ssm_b_re, ssm_b_im,
                                      ssm_c_re, ssm_c_im)
    w_pool_b = w_pool.astype(BF16)
    w_glu_b = w_glu.astype(BF16)
    w_out_b = w_out.astype(BF16)

    xs = x.reshape(s, d)
    for l in range(depth):
        qkv, rest, ft = _inproj(xs, norm_g[l].reshape(1, d), mod[l], w_main[l], w_f[l],
                                b_f_row[l], d_attn=d_attn)
        ya = _attention(qkv, ft, rest, d_attn=d_attn)
        yp = _pool(rest, w_pool_b[l], pool_scale[l].reshape(1, d_pool),
                   col_u=col_up, col_g=col_gp, d_pool=d_pool)
        ysm = _ssm(rest, a_all[l], bb_all[l], cc_all[l], ssm_d[l].reshape(1, d_ssm),
                   w_glu_b[l], b_glu[l].reshape(1, d_ssm),
                   col_u=col_us, col_g=col_gs, d_ssm=d_ssm)
        xs = _outproj(ya, yp, ysm, w_out_b[l], xs, mod[l, 2:3, :])
    return _final_norm(xs, final_g.reshape(1, d)).reshape(b, s, d).astype(x.dtype)
```

```python
import functools
import math

import jax
import jax.numpy as jnp
from jax import lax
from jax.experimental import pallas as pl
from jax.experimental.pallas import tpu as pltpu

F32 = jnp.float32
BF16 = jnp.bfloat16

N_HEADS = 8
HEAD_DIM = 128
POOL_WINDOWS = (2, 4, 8, 16)
POOL_GROUP = 128
POOL_HALO = 16
SSM_GROUP = 16
SSM_STATE = 64
SSM_SLAB_GROUPS = 8
NORM_EPS = 1e-6
LANES = 128
VMEM_LIMIT = 56 * 1024 * 1024
ROW_CHUNKS = 4


def _params(sem, vmem=VMEM_LIMIT):
    return pltpu.CompilerParams(dimension_semantics=sem, vmem_limit_bytes=vmem)


def _sigmoid(x):
    return 1.0 / (1.0 + jnp.exp(-x))


def _silu(x):
    return x * _sigmoid(x)


def _pick(n, pref):
    t = min(n, pref)
    while n % t:
        t //= 2
    return t


def _ada_kernel(c_ref, w_ref, b_ref, o_ref):
    ca = _silu(c_ref[...])
    o_ref[0] = jnp.sum(w_ref[0] * ca, axis=0, keepdims=True) + b_ref[0]


def _ada_mod(c, w_ada, b_ada):
    depth, d, n = w_ada.shape
    tn = _pick(n, 1024)
    return pl.pallas_call(
        _ada_kernel,
        grid=(depth, n // tn),
        in_specs=[
            pl.BlockSpec((d, 1), lambda l, j: (0, 0)),
            pl.BlockSpec((1, d, tn), lambda l, j: (l, 0, j)),
            pl.BlockSpec((1, 1, tn), lambda l, j: (l, 0, j)),
        ],
        out_specs=pl.BlockSpec((1, 1, tn), lambda l, j: (l, 0, j)),
        out_shape=jax.ShapeDtypeStruct((depth, 1, n), F32),
        compiler_params=_params(("arbitrary", "arbitrary")),
        name="ada_mod",
    )(c.reshape(d, 1), w_ada, b_ada.reshape(depth, 1, n))


def _inproj_kernel(x_ref, g_ref, mod_ref, w_ref, wf_ref, bf_ref,
                   qkv_ref, rest_ref, ft_ref, h_ref, carry_ref,
                   *, n_qkv_tiles, n_q_tiles, q_scale):
    i = pl.program_id(0)
    j = pl.program_id(1)
    tm = x_ref.shape[0]

    @pl.when(j == 0)
    def _():
        x = x_ref[...]
        ms = jnp.mean(x * x, axis=-1, keepdims=True)
        shift = mod_ref[0:1, :]
        scale = mod_ref[1:2, :]
        h = (x * lax.rsqrt(ms + NORM_EPS) * g_ref[...]) * (1.0 + scale) + shift
        hb = h.astype(BF16)
        h_ref[...] = hb
        f = jnp.dot(hb, wf_ref[...], preferred_element_type=F32) + bf_ref[...]
        logf = -(jnp.maximum(-f, 0.0) + jnp.log1p(jnp.exp(-jnp.abs(f))))
        row = lax.broadcasted_iota(jnp.int32, logf.shape, 0)
        cum = logf
        d = 1
        while d < tm:
            cum = cum + jnp.where(row >= d, pltpu.roll(cum, d, axis=0), 0.0)
            d *= 2

        @pl.when(i == 0)
        def _():
            carry_ref[...] = jnp.zeros_like(carry_ref)

        cum = cum + carry_ref[0:1, :]
        carry_ref[...] = jnp.broadcast_to(cum[tm - 1:tm, :], carry_ref.shape)
        cum_t = cum.T
        for hh in range(N_HEADS):
            ft_ref[hh] = cum_t[hh:hh + 1, :]

    cm = tm // ROW_CHUNKS

    @pl.when(j < n_qkv_tiles)
    def _():
        s = jnp.where(j < n_q_tiles, q_scale, 1.0).astype(F32)
        for c in range(ROW_CHUNKS):
            proj = jnp.dot(h_ref[c * cm:(c + 1) * cm, :], w_ref[...],
                           preferred_element_type=F32)
            qkv_ref[c * cm:(c + 1) * cm, :] = (proj * s).astype(BF16)

    @pl.when(j >= n_qkv_tiles)
    def _():
        for c in range(ROW_CHUNKS):
            rest_ref[c * cm:(c + 1) * cm, :] = jnp.dot(
                h_ref[c * cm:(c + 1) * cm, :], w_ref[...], preferred_element_type=F32)


def _inproj(x, g, mod, w_main, w_f, b_f_row, *, d_attn):
    s, d = x.shape
    n = w_main.shape[1]
    n_qkv = 3 * d_attn
    tm = _pick(s, 1024)
    tn = 512
    n_qkv_tiles = n_qkv // tn
    kern = functools.partial(_inproj_kernel, n_qkv_tiles=n_qkv_tiles,
                             n_q_tiles=d_attn // tn,
                             q_scale=HEAD_DIM ** -0.5 * math.log2(math.e))
    return pl.pallas_call(
        kern,
        grid=(s // tm, n // tn),
        in_specs=[
            pl.BlockSpec((tm, d), lambda i, j: (i, 0)),
            pl.BlockSpec((1, d), lambda i, j: (0, 0)),
            pl.BlockSpec((3, d), lambda i, j: (0, 0)),
            pl.BlockSpec((d, tn), lambda i, j: (0, j)),
            pl.BlockSpec((d, LANES), lambda i, j: (0, 0)),
            pl.BlockSpec((1, LANES), lambda i, j: (0, 0)),
        ],
        out_specs=[
            pl.BlockSpec((tm, tn), lambda i, j: (i, jnp.minimum(j, n_qkv_tiles - 1))),
            pl.BlockSpec((tm, tn), lambda i, j: (i, jnp.maximum(j - n_qkv_tiles, 0))),
            pl.BlockSpec((N_HEADS, 1, tm), lambda i, j: (0, 0, i)),
        ],
        out_shape=[
            jax.ShapeDtypeStruct((s, n_qkv), BF16),
            jax.ShapeDtypeStruct((s, n - n_qkv), F32),
            jax.ShapeDtypeStruct((N_HEADS, 1, s), F32),
        ],
        scratch_shapes=[pltpu.VMEM((tm, d), BF16), pltpu.VMEM((8, LANES), F32)],
        compiler_params=_params(("arbitrary", "arbitrary")),
        name="inproj",
    )(x, g, mod, w_main, w_f, b_f_row)


AUG_TERMS = 3
ONES_ROWS = 16
Q_STRIP = 1024
LOG2E = math.log2(math.e)


SAFE_EXP = 60.0
ZERO_EXP = -150.0
NORM_SLACK = 1.02


def _tile_lanes(row, n):
    return jnp.concatenate([row] * (n // LANES), axis=1)


def _attn_kernel(q_ref, k_ref, v_ref, ft_ref, g_ref, o_ref,
                 kaug_ref, vt_ref, base_ref, bend_ref, kall_ref, acc_ref, *, tq, tk):
    i = pl.program_id(1)
    seq = k_ref.shape[0]
    dh = HEAD_DIM
    lane = lax.broadcasted_iota(jnp.int32, (LANES, LANES), 1)
    lane_row = lax.broadcasted_iota(jnp.int32, (1, LANES), 1)
    ones_sq = jnp.ones((LANES, LANES), BF16)

    @pl.when(i == 0)
    def _():
        vt_ref[dh:, :] = jnp.ones((ONES_ROWS, seq), BF16)
        kall_ref[...] = jnp.zeros_like(kall_ref)
        bend_ref[...] = jnp.zeros_like(bend_ref)

        def fill(c, carry):
            base = None
            col = None
            for bb in range(tk // LANES):
                off = pl.multiple_of(c * tk + bb * LANES, LANES)
                row = ft_ref[0, :, pl.ds(off, LANES)]
                col = (-LOG2E) * jnp.broadcast_to(row, (LANES, LANES)).T
                if bb == 0:
                    base = col[0:1, :]
                    base_ref[pl.ds(c, 1), :] = base
                rel = col - base
                hi = rel.astype(BF16).astype(F32)
                mid = (rel - hi).astype(BF16).astype(F32)
                lo = (rel - hi - mid).astype(BF16).astype(F32)
                aug = jnp.where(lane == 0, hi, jnp.where(lane == 1, mid,
                                                         jnp.where(lane == 2, lo, 0.0)))
                kb = k_ref[pl.ds(off, LANES), :]
                kaug_ref[pl.ds(off, LANES), 0:dh] = kb
                kaug_ref[pl.ds(off, LANES), dh:] = aug.astype(BF16)
                vt_ref[0:dh, pl.ds(off, LANES)] = (
                    v_ref[pl.ds(off, LANES), :].astype(F32).T.astype(BF16))
                kf = kb.astype(F32)
                n2 = jnp.dot((kf * kf).astype(BF16), ones_sq, preferred_element_type=F32)
                kall_ref[0:1, :] = jnp.maximum(kall_ref[0:1, :],
                                               jnp.max(n2, axis=0, keepdims=True))
            bend_ref[0:1, :] = jnp.where(lane_row == c, col[LANES - 1:LANES, :],
                                         bend_ref[0:1, :])
            return carry
        lax.fori_loop(0, seq // tk, fill, 0)

    lane_q = lax.broadcasted_iota(jnp.int32, (tq, LANES), 1)
    q = q_ref[...]
    q_aug = jnp.concatenate(
        [q, jnp.where(lane_q < AUG_TERMS, 1.0, 0.0).astype(BF16)], axis=1)
    acc_ref[...] = jnp.zeros_like(acc_ref)
    base_q = base_ref[pl.ds(i, 1), :]
    n_strip = tq // Q_STRIP

    def chunk(j):
        k_off = pl.multiple_of(j * tk, tk)
        kc = kaug_ref[pl.ds(k_off, tk), :]
        vc = vt_ref[:, pl.ds(k_off, tk)]
        delta = base_ref[pl.ds(j, 1), :] - base_q
        return kc, vc, delta

    def scores(kc, lo, hi):
        return lax.dot_general(kc, q_aug[lo:hi, :], (((1,), (1,)), ((), ())),
                               preferred_element_type=F32)

    def online_step(j, m, masked):
        kc, vc, delta = chunk(j)
        delta = _tile_lanes(delta, Q_STRIP)
        m_out = []
        for st in range(n_strip):
            lo, hi = st * Q_STRIP, (st + 1) * Q_STRIP
            s = scores(kc, lo, hi)
            if masked:
                key = lax.broadcasted_iota(jnp.int32, (tk, Q_STRIP), 0)
                qry = lax.broadcasted_iota(jnp.int32, (tk, Q_STRIP), 1) + lo
                s = jnp.where(key <= qry, s, -jnp.inf)
            m_old = m[st] - delta
            m_new = jnp.maximum(m_old, jnp.max(s, axis=0, keepdims=True))
            p = jnp.exp2(s - m_new).astype(BF16)
            corr = jnp.exp2(m_old - m_new)
            pv = jnp.dot(vc, p, preferred_element_type=F32)
            acc_ref[:, lo:hi] = acc_ref[:, lo:hi] * corr + pv
            m_out.append(m_new + delta)
        return tuple(m_out)

    m0 = tuple(jnp.full((1, Q_STRIP), -jnp.inf, F32) for _ in range(n_strip))
    m_d = online_step(i, m0, True)
    m_row = jnp.concatenate(m_d, axis=1)

    qf = q.astype(F32)
    qn2 = lax.dot_general(jnp.ones((8, dh), BF16), (qf * qf).astype(BF16),
                          (((1,), (1,)), ((), ())), preferred_element_type=F32)[0:1, :]
    qk_bound = jnp.sqrt(qn2 * _tile_lanes(kall_ref[0:1, :], tq)) * NORM_SLACK + 1.0
    slack = jnp.max(qk_bound - m_row)
    live = jnp.logical_and(slack + (bend_ref[0:1, :] - base_q) >= ZERO_EXP, lane_row < i)
    n_live = jnp.sum(live.astype(jnp.int32))

    @pl.when(slack <= SAFE_EXP)
    def _():
        def fast_step(jj, carry):
            kc, vc, delta = chunk(i - 1 - jj)
            ref = m_row - _tile_lanes(delta, tq)
            for st in range(n_strip):
                lo, hi = st * Q_STRIP, (st + 1) * Q_STRIP
                p = jnp.exp2(scores(kc, lo, hi) - ref[:, lo:hi]).astype(BF16)
                acc_ref[:, lo:hi] += jnp.dot(vc, p, preferred_element_type=F32)
            return carry
        lax.fori_loop(0, n_live, fast_step, 0)

    @pl.when(slack > SAFE_EXP)
    def _():
        lax.fori_loop(0, i, lambda jj, m: online_step(i - 1 - jj, m, False), m_d)

    y = (acc_ref[0:dh, :] / acc_ref[dh:dh + 1, :]).T
    o_ref[...] = (y * _silu(g_ref[...])).astype(o_ref.dtype)


def _attention(qkv, ft, rest, *, d_attn):
    s = qkv.shape[0]
    tq = _pick(s, 1024)
    nh = d_attn // HEAD_DIM
    kern = functools.partial(_attn_kernel, tq=tq, tk=tq)
    return pl.pallas_call(
        kern,
        grid=(nh, s // tq),
        in_specs=[
            pl.BlockSpec((tq, HEAD_DIM), lambda h, i: (i, h)),
            pl.BlockSpec((s, HEAD_DIM), lambda h, i: (0, nh + h)),
            pl.BlockSpec((s, HEAD_DIM), lambda h, i: (0, 2 * nh + h)),
            pl.BlockSpec((1, 1, s), lambda h, i: (h, 0, 0)),
            pl.BlockSpec((tq, HEAD_DIM), lambda h, i: (i, h)),
        ],
        out_specs=pl.BlockSpec((tq, HEAD_DIM), lambda h, i: (i, h)),
        out_shape=jax.ShapeDtypeStruct((s, d_attn), BF16),
        scratch_shapes=[
            pltpu.VMEM((s, 2 * HEAD_DIM), BF16),
            pltpu.VMEM((HEAD_DIM + ONES_ROWS, s), BF16),
            pltpu.VMEM((max(s // tq, 8), LANES), F32),
            pltpu.VMEM((8, LANES), F32),
            pltpu.VMEM((8, LANES), F32),
            pltpu.VMEM((HEAD_DIM + ONES_ROWS, tq), F32),
        ],
        compiler_params=_params(("arbitrary", "arbitrary")),
        name="fox_attention",
    )(qkv, qkv, qkv, ft, rest)


def _pool_kernel(u_ref, prev_ref, g_ref, w_ref, sc_ref, o_ref):
    i = pl.program_id(0)
    tp = u_ref.shape[0]
    u = u_ref[...]
    prev = jnp.where(i > 0, prev_ref[...], 0.0)
    t1 = lax.broadcasted_iota(jnp.int32, (tp, POOL_GROUP), 0) + (i * tp + 1)
    outs = []
    for g, w in enumerate(POOL_WINDOWS):
        lo, hi = g * POOL_GROUP, (g + 1) * POOL_GROUP
        ug = u[:, lo:hi]
        ext = jnp.concatenate([prev[:, lo:hi], ug], axis=0)
        win = ext
        span = 1
        while span < w:
            win = win + pltpu.roll(win, span, axis=0)
            span *= 2
        win = win[POOL_HALO:]
        cnt = jnp.minimum(t1, w).astype(F32)
        pooled = win / cnt - ug
        mixed = jnp.dot(pooled.astype(BF16), w_ref[g], preferred_element_type=F32)
        outs.append(mixed)
    mixed = jnp.concatenate(outs, axis=1) * sc_ref[...]
    o_ref[...] = (mixed * _silu(g_ref[...])).astype(o_ref.dtype)


def _pool(rest, w_pool, pool_scale, *, col_u, col_g, d_pool):
    s = rest.shape[0]
    tp = _pick(s, 1024)
    cu, cg = col_u // d_pool, col_g // d_pool
    halo_blocks = tp // POOL_HALO
    return pl.pallas_call(
        _pool_kernel,
        grid=(s // tp,),
        in_specs=[
            pl.BlockSpec((tp, d_pool), lambda i: (i, cu)),
            pl.BlockSpec((POOL_HALO, d_pool),
                         lambda i: (jnp.maximum(i * halo_blocks - 1, 0), cu)),
            pl.BlockSpec((tp, d_pool), lambda i: (i, cg)),
            pl.BlockSpec(w_pool.shape, lambda i: (0, 0, 0)),
            pl.BlockSpec((1, d_pool), lambda i: (0, 0)),
        ],
        out_specs=pl.BlockSpec((tp, d_pool), lambda i: (i, 0)),
        out_shape=jax.ShapeDtypeStruct((s, d_pool), BF16),
        compiler_params=_params(("arbitrary",)),
        name="pool",
    )(rest, rest, rest, w_pool, pool_scale)


SSM_CHUNK = 8


def _discretise(lr, li, ldt):
    dt = jnp.exp(ldt)
    mag = jnp.exp(lr * dt)
    ab_re = mag * jnp.cos(li * dt)
    ab_im = mag * jnp.sin(li * dt)
    den = lr * lr + li * li
    nr = ab_re - 1.0
    ni = ab_im
    z_re = (nr * lr + ni * li) / den
    z_im = (ni * lr - nr * li) / den
    return ab_re, ab_im, z_re, z_im


def _powers(a_re, a_im, n):
    out = [(jnp.ones_like(a_re), jnp.zeros_like(a_im))]
    for _ in range(n):
        p_re, p_im = out[-1]
        out.append((p_re * a_re - p_im * a_im, p_re * a_im + p_im * a_re))
    return out


def _ssm_prep_kernel(lr_ref, li_ref, ldt_ref, lrc_ref, lic_ref, ldtc_ref,
                     br_ref, bi_ref, cr_ref, ci_ref, t_ref, p_ref, e_ref, al_ref):
    nl = SSM_CHUNK
    cw = br_ref.shape[2]
    a_re, a_im, z_re, z_im = _discretise(lr_ref[0, 0], li_ref[0, 0], ldt_ref[0, 0])
    pw = _powers(a_re, a_im, nl)
    al_ref[0, 0] = jnp.concatenate([pw[nl][0], pw[nl][1]], axis=0)
    br = br_ref[0, 0]
    bi = bi_ref[0, 0]
    bb_re = z_re * br - z_im * bi
    bb_im = z_re * bi + z_im * br
    bb = jnp.concatenate([bb_re, bb_im], axis=1)
    ac_re, ac_im, _, _ = _discretise(lrc_ref[0, 0], lic_ref[0, 0], ldtc_ref[0, 0])
    pwc = _powers(ac_re, ac_im, nl)
    cr = cr_ref[0, 0]
    ci = ci_ref[0, 0]
    ca = [jnp.concatenate([cr * q_re - ci * q_im, -(cr * q_im + ci * q_re)], axis=0)
          for q_re, q_im in pwc]
    kd = [jnp.dot(bb, ca[d], preferred_element_type=F32,
                  precision=lax.Precision.HIGHEST).astype(BF16) for d in range(nl)]
    zero = jnp.zeros((cw, cw), BF16)
    for src in range(nl):
        for dst in range(nl):
            t_ref[0, 0, src * cw:(src + 1) * cw, dst * cw:(dst + 1) * cw] = (
                kd[dst - src] if dst >= src else zero)
        q_re, q_im = pw[nl - 1 - src]
        p_ref[0, 0, src * cw:(src + 1) * cw, :] = jnp.concatenate(
            [bb_re * q_re - bb_im * q_im, bb_re * q_im + bb_im * q_re], axis=1).astype(BF16)
        e_ref[0, 0, :, src * cw:(src + 1) * cw] = ca[src + 1].astype(BF16)


def _ssm_prep(lam_re, lam_im, log_dt, b_re, b_im, c_re, c_im):
    depth, ng, ns = lam_re.shape
    gc = b_re.shape[-1]
    n_slab = ng // SSM_SLAB_GROUPS
    eye = jnp.eye(SSM_SLAB_GROUPS, dtype=F32)
    sw = SSM_SLAB_GROUPS * ns

    def place_b(b):
        b = b.reshape(depth, n_slab, SSM_SLAB_GROUPS, ns, gc).transpose(0, 1, 2, 4, 3)
        return (b[:, :, :, :, None, :] * eye[None, None, :, None, :, None]).reshape(
            depth, n_slab, SSM_SLAB_GROUPS * gc, sw)

    def place_c(c):
        c = c.reshape(depth, n_slab, SSM_SLAB_GROUPS, gc, ns).transpose(0, 1, 2, 4, 3)
        return (c[:, :, :, :, None, :] * eye[None, None, :, None, :, None]).reshape(
            depth, n_slab, sw, SSM_SLAB_GROUPS * gc)

    cw = SSM_SLAB_GROUPS * gc
    nl = SSM_CHUNK
    ldt = jnp.broadcast_to(log_dt[:, :, None], (depth, ng, ns))
    row = lambda v: v.reshape(depth, n_slab, 1, sw)
    col = lambda v: jnp.broadcast_to(v.reshape(depth, n_slab, sw, 1), (depth, n_slab, sw, cw))
    blk = lambda *shape: pl.BlockSpec((1, 1) + shape, lambda l, k: (l, k, 0, 0))
    return pl.pallas_call(
        _ssm_prep_kernel,
        grid=(depth, n_slab),
        in_specs=[blk(1, sw)] * 3 + [blk(sw, cw)] * 3 + [blk(cw, sw)] * 2 + [blk(sw, cw)] * 2,
        out_specs=[blk(nl * cw, nl * cw), blk(nl * cw, 2 * sw), blk(2 * sw, nl * cw),
                   blk(2, sw)],
        out_shape=[
            jax.ShapeDtypeStruct((depth, n_slab, nl * cw, nl * cw), BF16),
            jax.ShapeDtypeStruct((depth, n_slab, nl * cw, 2 * sw), BF16),
            jax.ShapeDtypeStruct((depth, n_slab, 2 * sw, nl * cw), BF16),
            jax.ShapeDtypeStruct((depth, n_slab, 2, sw), F32),
        ],
        compiler_params=_params(("arbitrary", "arbitrary")),
        name="ssm_prep",
    )(row(lam_re), row(lam_im), row(ldt), col(lam_re), col(lam_im), col(ldt),
      place_b(b_re), place_b(b_im), place_c(c_re), place_c(c_im))


def _gelu_tanh(y):
    c = math.sqrt(2.0 / math.pi)
    return 0.5 * y * (1.0 + jnp.tanh(c * (y + 0.044715 * (y * y * y))))


def _ssm_kernel(u_ref, t_ref, p_ref, e_ref, al_ref, d_ref, y_ref, uc_ref, carry_ref):
    i = pl.program_id(1)
    nl = SSM_CHUNK
    ts, cw = u_ref.shape
    rows = ts // nl
    sw = al_ref.shape[3]

    @pl.when(i == 0)
    def _():
        carry_ref[...] = jnp.zeros_like(carry_ref)

    for tau in range(nl):
        uc_ref[:, tau * cw:(tau + 1) * cw] = (
            u_ref[pl.ds(tau, rows, stride=nl), :].astype(BF16))
    uc = uc_ref[...]
    v = jnp.dot(uc, p_ref[0, 0], preferred_element_type=F32)
    xr = v[:, :sw]
    xi = v[:, sw:]
    a_re = al_ref[0, 0, 0:1, :]
    a_im = al_ref[0, 0, 1:2, :]
    c_re = carry_ref[0:1, :]
    c_im = carry_ref[1:2, :]
    row = lax.broadcasted_iota(jnp.int32, (rows, sw), 0)
    first = row == 0
    xr = xr + jnp.where(first, a_re * c_re - a_im * c_im, 0.0)
    xi = xi + jnp.where(first, a_re * c_im + a_im * c_re, 0.0)
    q_re, q_im = a_re, a_im
    d = 1
    while d < rows:
        keep = row >= d
        sr = jnp.where(keep, pltpu.roll(xr, d, axis=0), 0.0)
        si = jnp.where(keep, pltpu.roll(xi, d, axis=0), 0.0)
        xr, xi = xr + (q_re * sr - q_im * si), xi + (q_re * si + q_im * sr)
        q_re, q_im = q_re * q_re - q_im * q_im, 2.0 * (q_re * q_im)
        d *= 2
    pr = jnp.where(first, c_re, pltpu.roll(xr, 1, axis=0))
    pi = jnp.where(first, c_im, pltpu.roll(xi, 1, axis=0))
    carry_ref[0:1, :] = xr[rows - 1:rows, :]
    carry_ref[1:2, :] = xi[rows - 1:rows, :]
    xp = jnp.concatenate([pr, pi], axis=1).astype(BF16)
    yc = (jnp.dot(uc, t_ref[0, 0], preferred_element_type=F32)
          + jnp.dot(xp, e_ref[0, 0], preferred_element_type=F32))
    for tau in range(nl):
        y_ref[pl.ds(tau, rows, stride=nl), :] = (
            yc[:, tau * cw:(tau + 1) * cw]
            + d_ref[...] * u_ref[pl.ds(tau, rows, stride=nl), :])


def _ssm(rest, t_mat, p_mat, e_mat, a_l, d_row, *, layer, col_u):
    s = rest.shape[0]
    n_slab, cw = t_mat.shape[1], d_row.shape[1] // t_mat.shape[1]
    ts = _pick(s, 4096)
    cu = col_u // cw
    sw = a_l.shape[3]
    mat = lambda m: pl.BlockSpec((1, 1) + m.shape[2:], lambda k, i: (layer, k, 0, 0))
    return pl.pallas_call(
        _ssm_kernel,
        grid=(n_slab, s // ts),
        in_specs=[
            pl.BlockSpec((ts, cw), lambda k, i: (i, cu + k)),
            mat(t_mat), mat(p_mat), mat(e_mat), mat(a_l),
            pl.BlockSpec((1, cw), lambda k, i: (0, k)),
        ],
        out_specs=pl.BlockSpec((ts, cw), lambda k, i: (i, k)),
        out_shape=jax.ShapeDtypeStruct((s, n_slab * cw), F32),
        scratch_shapes=[pltpu.VMEM((ts // SSM_CHUNK, SSM_CHUNK * cw), BF16),
                        pltpu.VMEM((2, sw), F32)],
        compiler_params=_params(("arbitrary", "arbitrary")),
        name="ssm",
    )(rest, t_mat, p_mat, e_mat, a_l, d_row)


def _glu_kernel(y_ref, g_ref, wg_ref, bg_ref, o_ref):
    y = _gelu_tanh(y_ref[...])
    z = jnp.dot(y.astype(BF16), wg_ref[...], preferred_element_type=F32) + bg_ref[...]
    o_ref[...] = (y * _sigmoid(z) * _silu(g_ref[...])).astype(o_ref.dtype)


def _glu(y, rest, w_glu, b_glu, *, col_g):
    s, d_ssm = y.shape
    tm = _pick(s, 1024)
    cg = col_g // d_ssm
    return pl.pallas_call(
        _glu_kernel,
        grid=(s // tm,),
        in_specs=[
            pl.BlockSpec((tm, d_ssm), lambda i: (i, 0)),
            pl.BlockSpec((tm, d_ssm), lambda i: (i, cg)),
            pl.BlockSpec(w_glu.shape, lambda i: (0, 0)),
            pl.BlockSpec((1, d_ssm), lambda i: (0, 0)),
        ],
        out_specs=pl.BlockSpec((tm, d_ssm), lambda i: (i, 0)),
        out_shape=jax.ShapeDtypeStruct((s, d_ssm), BF16),
        compiler_params=_params(("arbitrary",)),
        name="ssm_glu",
    )(y, rest, w_glu, b_glu)


def _outproj_kernel(ya_ref, yp_ref, ys_ref, w_ref, x_ref, mod_ref, fg_ref, o_ref, y_ref,
                    *, final):
    j = pl.program_id(1)
    da = ya_ref.shape[1]
    dp = yp_ref.shape[1]
    tm, tn = x_ref.shape

    @pl.when(j == 0)
    def _():
        y_ref[:, 0:da] = ya_ref[...]
        y_ref[:, da:da + dp] = yp_ref[...]
        y_ref[:, da + dp:] = ys_ref[...]

    cm = tm // ROW_CHUNKS
    col = pl.multiple_of(j * tn, tn)
    for c in range(ROW_CHUNKS):
        out = jnp.dot(y_ref[c * cm:(c + 1) * cm, :], w_ref[...], preferred_element_type=F32)
        val = x_ref[c * cm:(c + 1) * cm, :] + mod_ref[...] * out
        if final:
            o_ref[c * cm:(c + 1) * cm, pl.ds(col, tn)] = val
        else:
            o_ref[c * cm:(c + 1) * cm, :] = val

    if final:
        @pl.when(j == pl.num_programs(1) - 1)
        def _():
            xn = o_ref[...]
            ms = jnp.mean(xn * xn, axis=-1, keepdims=True)
            o_ref[...] = xn * lax.rsqrt(ms + NORM_EPS) * fg_ref[...]


def _outproj(ya, yp, ysm, w_out, x, gate, final_g, *, final):
    s, d = x.shape
    dm = w_out.shape[0]
    tm = _pick(s, 1024)
    tn = 512
    out_spec = (pl.BlockSpec((tm, d), lambda i, j: (i, 0)) if final
                else pl.BlockSpec((tm, tn), lambda i, j: (i, j)))
    return pl.pallas_call(
        functools.partial(_outproj_kernel, final=final),
        grid=(s // tm, d // tn),
        in_specs=[
            pl.BlockSpec((tm, ya.shape[1]), lambda i, j: (i, 0)),
            pl.BlockSpec((tm, yp.shape[1]), lambda i, j: (i, 0)),
            pl.BlockSpec((tm, ysm.shape[1]), lambda i, j: (i, 0)),
            pl.BlockSpec((dm, tn), lambda i, j: (0, j)),
            pl.BlockSpec((tm, tn), lambda i, j: (i, j)),
            pl.BlockSpec((1, tn), lambda i, j: (0, j)),
            pl.BlockSpec((1, d), lambda i, j: (0, 0)),
        ],
        out_specs=out_spec,
        out_shape=jax.ShapeDtypeStruct((s, d), F32),
        scratch_shapes=[pltpu.VMEM((tm, dm), BF16)],
        compiler_params=_params(("arbitrary", "arbitrary")),
        name="outproj",
    )(ya, yp, ysm, w_out, x, gate, final_g)


def kernel(x, c, norm_g, w_ada, b_ada, w_in, b_f, w_pool, pool_scale, lam_re, lam_im,
           ssm_b_re, ssm_b_im, ssm_c_re, ssm_c_im, ssm_d, log_dt, w_glu, b_glu, w_out,
           final_g):
    b, s, d = x.shape
    assert b == 1
    depth = w_in.shape[0]
    d_pool = pool_scale.shape[1]
    d_ssm = b_glu.shape[1]
    d_attn = N_HEADS * HEAD_DIM
    n_f = b_f.shape[1]
    assert n_f == N_HEADS and w_in.shape[2] == 4 * d_attn + n_f + 2 * d_pool + 2 * d_ssm

    w_main = jnp.concatenate([w_in[:, :, :4 * d_attn], w_in[:, :, 4 * d_attn + n_f:]],
                             axis=2).astype(BF16)
    w_f = jnp.pad(w_in[:, :, 4 * d_attn:4 * d_attn + n_f],
                  ((0, 0), (0, 0), (0, LANES - n_f))).astype(BF16)
    b_f_row = jnp.pad(b_f, ((0, 0), (0, LANES - n_f))).reshape(depth, 1, LANES)
    col_up = d_attn
    col_gp = col_up + d_pool
    col_us = col_gp + d_pool
    col_gs = col_us + d_ssm

    mod = _ada_mod(c, w_ada, b_ada).reshape(depth, 3, d)
    t_all, p_all, e_all, al_all = _ssm_prep(lam_re, lam_im, log_dt, ssm_b_re, ssm_b_im,
                                            ssm_c_re, ssm_c_im)
    w_pool_b = w_pool.astype(BF16)
    w_glu_b = w_glu.astype(BF16)
    w_out_b = w_out.astype(BF16)

    xs = x.reshape(s, d)
    for l in range(depth):
        qkv, rest, ft = _inproj(xs, norm_g[l].reshape(1, d), mod[l], w_main[l], w_f[l],
                                b_f_row[l], d_attn=d_attn)
        ya = _attention(qkv, ft, rest, d_attn=d_attn)
        yp = _pool(rest, w_pool_b[l], pool_scale[l].reshape(1, d_pool),
                   col_u=col_up, col_g=col_gp, d_pool=d_pool)
        y_ssm = _ssm(rest, t_all, p_all, e_all, al_all, ssm_d[l].reshape(1, d_ssm),
                     layer=l, col_u=col_us)
        ysm = _glu(y_ssm, rest, w_glu_b[l], b_glu[l].reshape(1, d_ssm), col_g=col_gs)
        xs = _outproj(ya, yp, ysm, w_out_b[l], xs, mod[l, 2:3, :], final_g.reshape(1, d),
                      final=(l == depth - 1))
    return xs.reshape(b, s, d).astype(x.dtype)
```

```python
import functools
import math

import jax
import jax.numpy as jnp
from jax import lax
from jax.experimental import pallas as pl
from jax.experimental.pallas import tpu as pltpu

F32 = jnp.float32
BF16 = jnp.bfloat16

N_HEADS = 8
HEAD_DIM = 128
POOL_WINDOWS = (2, 4, 8, 16)
POOL_GROUP = 128
POOL_HALO = 16
SSM_GROUP = 16
SSM_STATE = 64
SSM_SLAB_GROUPS = 8
NORM_EPS = 1e-6
LANES = 128
VMEM_LIMIT = 56 * 1024 * 1024
ROW_CHUNKS = 4
PROJ_TN = 512


def _params(sem, vmem=VMEM_LIMIT):
    return pltpu.CompilerParams(dimension_semantics=sem, vmem_limit_bytes=vmem)


def _sigmoid(x):
    return 1.0 / (1.0 + jnp.exp(-x))


def _silu(x):
    return x * _sigmoid(x)


def _pick(n, pref):
    t = min(n, pref)
    while n % t:
        t //= 2
    return t


def _ada_kernel(c_ref, w_ref, b_ref, o_ref):
    ca = _silu(c_ref[...])
    o_ref[0] = jnp.sum(w_ref[0] * ca, axis=0, keepdims=True) + b_ref[0]


def _ada_mod(c, w_ada, b_ada):
    depth, d, n = w_ada.shape
    tn = _pick(n, 1024)
    return pl.pallas_call(
        _ada_kernel,
        grid=(depth, n // tn),
        in_specs=[
            pl.BlockSpec((d, 1), lambda l, j: (0, 0)),
            pl.BlockSpec((1, d, tn), lambda l, j: (l, 0, j)),
            pl.BlockSpec((1, 1, tn), lambda l, j: (l, 0, j)),
        ],
        out_specs=pl.BlockSpec((1, 1, tn), lambda l, j: (l, 0, j)),
        out_shape=jax.ShapeDtypeStruct((depth, 1, n), F32),
        compiler_params=_params(("arbitrary", "arbitrary")),
        name="ada_mod",
    )(c.reshape(d, 1), w_ada, b_ada.reshape(depth, 1, n))


def _inproj_kernel(x_ref, g_ref, mod_ref, w_ref, wf_ref, bf_ref,
                   qkv_ref, rest_ref, ft_ref, h_ref, carry_ref,
                   *, n_qkv_tiles, n_q_tiles, q_scale):
    i = pl.program_id(0)
    j = pl.program_id(1)
    tm = x_ref.shape[0]

    @pl.when(j == 0)
    def _():
        x = x_ref[...]
        ms = jnp.mean(x * x, axis=-1, keepdims=True)
        shift = mod_ref[0:1, :]
        scale = mod_ref[1:2, :]
        h = (x * lax.rsqrt(ms + NORM_EPS) * g_ref[...]) * (1.0 + scale) + shift
        hb = h.astype(BF16)
        h_ref[...] = hb
        f = jnp.dot(hb, wf_ref[...], preferred_element_type=F32) + bf_ref[...]
        logf = -(jnp.maximum(-f, 0.0) + jnp.log1p(jnp.exp(-jnp.abs(f))))
        row = lax.broadcasted_iota(jnp.int32, logf.shape, 0)
        cum = logf
        d = 1
        while d < tm:
            cum = cum + jnp.where(row >= d, pltpu.roll(cum, d, axis=0), 0.0)
            d *= 2

        @pl.when(i == 0)
        def _():
            carry_ref[...] = jnp.zeros_like(carry_ref)

        cum = cum + carry_ref[0:1, :]
        carry_ref[...] = jnp.broadcast_to(cum[tm - 1:tm, :], carry_ref.shape)
        cum_t = cum.T
        for hh in range(N_HEADS):
            ft_ref[hh] = cum_t[hh:hh + 1, :]

    cm = tm // ROW_CHUNKS

    @pl.when(j < n_qkv_tiles)
    def _():
        s = jnp.where(j < n_q_tiles, q_scale, 1.0).astype(F32)
        for c in range(ROW_CHUNKS):
            proj = jnp.dot(h_ref[c * cm:(c + 1) * cm, :], w_ref[0],
                           preferred_element_type=F32)
            qkv_ref[c * cm:(c + 1) * cm, :] = (proj * s).astype(BF16)

    @pl.when(j >= n_qkv_tiles)
    def _():
        for c in range(ROW_CHUNKS):
            rest_ref[c * cm:(c + 1) * cm, :] = jnp.dot(
                h_ref[c * cm:(c + 1) * cm, :], w_ref[0], preferred_element_type=F32)


def _col_tiles(w):
    *lead, k, n = w.shape
    w = w.reshape(*lead, k, n // PROJ_TN, PROJ_TN)
    return jnp.swapaxes(w, -2, -3)


def _inproj(x, g, mod, w_main, w_f, b_f_row, *, d_attn):
    s, d = x.shape
    tn = PROJ_TN
    n = w_main.shape[0] * tn
    n_qkv = 3 * d_attn
    tm = _pick(s, 1024)
    n_qkv_tiles = n_qkv // tn
    kern = functools.partial(_inproj_kernel, n_qkv_tiles=n_qkv_tiles,
                             n_q_tiles=d_attn // tn,
                             q_scale=HEAD_DIM ** -0.5 * math.log2(math.e))
    return pl.pallas_call(
        kern,
        grid=(s // tm, n // tn),
        in_specs=[
            pl.BlockSpec((tm, d), lambda i, j: (i, 0)),
            pl.BlockSpec((1, d), lambda i, j: (0, 0)),
            pl.BlockSpec((3, d), lambda i, j: (0, 0)),
            pl.BlockSpec((1, d, tn), lambda i, j: (j, 0, 0)),
            pl.BlockSpec((d, LANES), lambda i, j: (0, 0)),
            pl.BlockSpec((1, LANES), lambda i, j: (0, 0)),
        ],
        out_specs=[
            pl.BlockSpec((tm, tn), lambda i, j: (i, jnp.minimum(j, n_qkv_tiles - 1))),
            pl.BlockSpec((tm, tn), lambda i, j: (i, jnp.maximum(j - n_qkv_tiles, 0))),
            pl.BlockSpec((N_HEADS, 1, tm), lambda i, j: (0, 0, i)),
        ],
        out_shape=[
            jax.ShapeDtypeStruct((s, n_qkv), BF16),
            jax.ShapeDtypeStruct((s, n - n_qkv), F32),
            jax.ShapeDtypeStruct((N_HEADS, 1, s), F32),
        ],
        scratch_shapes=[pltpu.VMEM((tm, d), BF16), pltpu.VMEM((8, LANES), F32)],
        compiler_params=_params(("arbitrary", "arbitrary")),
        name="inproj",
    )(x, g, mod, w_main, w_f, b_f_row)


AUG_TERMS = 3
ONES_ROWS = 16
Q_STRIP = 1024
LOG2E = math.log2(math.e)


SAFE_EXP = 60.0
ZERO_EXP = -150.0
NORM_SLACK = 1.02


def _tile_lanes(row, n):
    return jnp.concatenate([row] * (n // LANES), axis=1)


def _attn_kernel(q_ref, k_ref, v_ref, ft_ref, g_ref, o_ref,
                 kaug_ref, vt_ref, base_ref, bend_ref, kall_ref, acc_ref, *, tq, tk):
    i = pl.program_id(1)
    seq = k_ref.shape[0]
    dh = HEAD_DIM
    lane = lax.broadcasted_iota(jnp.int32, (LANES, LANES), 1)
    lane_row = lax.broadcasted_iota(jnp.int32, (1, LANES), 1)
    ones_sq = jnp.ones((LANES, LANES), BF16)

    @pl.when(i == 0)
    def _():
        vt_ref[dh:, :] = jnp.ones((ONES_ROWS, seq), BF16)
        kall_ref[...] = jnp.zeros_like(kall_ref)
        bend_ref[...] = jnp.zeros_like(bend_ref)

        def fill(c, carry):
            base = None
            col = None
            for bb in range(tk // LANES):
                off = pl.multiple_of(c * tk + bb * LANES, LANES)
                row = ft_ref[0, :, pl.ds(off, LANES)]
                col = (-LOG2E) * jnp.broadcast_to(row, (LANES, LANES)).T
                if bb == 0:
                    base = col[0:1, :]
                    base_ref[pl.ds(c, 1), :] = base
                rel = col - base
                hi = rel.astype(BF16).astype(F32)
                mid = (rel - hi).astype(BF16).astype(F32)
                lo = (rel - hi - mid).astype(BF16).astype(F32)
                aug = jnp.where(lane == 0, hi, jnp.where(lane == 1, mid,
                                                         jnp.where(lane == 2, lo, 0.0)))
                kb = k_ref[pl.ds(off, LANES), :]
                kaug_ref[pl.ds(off, LANES), 0:dh] = kb
                kaug_ref[pl.ds(off, LANES), dh:] = aug.astype(BF16)
                vt_ref[0:dh, pl.ds(off, LANES)] = (
                    v_ref[pl.ds(off, LANES), :].astype(F32).T.astype(BF16))
                kf = kb.astype(F32)
                n2 = jnp.dot((kf * kf).astype(BF16), ones_sq, preferred_element_type=F32)
                kall_ref[0:1, :] = jnp.maximum(kall_ref[0:1, :],
                                               jnp.max(n2, axis=0, keepdims=True))
            bend_ref[0:1, :] = jnp.where(lane_row == c, col[LANES - 1:LANES, :],
                                         bend_ref[0:1, :])
            return carry
        lax.fori_loop(0, seq // tk, fill, 0)

    lane_q = lax.broadcasted_iota(jnp.int32, (tq, LANES), 1)
    q = q_ref[...]
    q_aug = jnp.concatenate(
        [q, jnp.where(lane_q < AUG_TERMS, 1.0, 0.0).astype(BF16)], axis=1)
    acc_ref[...] = jnp.zeros_like(acc_ref)
    base_q = base_ref[pl.ds(i, 1), :]
    n_strip = tq // Q_STRIP

    def chunk(j):
        k_off = pl.multiple_of(j * tk, tk)
        kc = kaug_ref[pl.ds(k_off, tk), :]
        vc = vt_ref[:, pl.ds(k_off, tk)]
        delta = base_ref[pl.ds(j, 1), :] - base_q
        return kc, vc, delta

    def scores(kc, lo, hi):
        return lax.dot_general(kc, q_aug[lo:hi, :], (((1,), (1,)), ((), ())),
                               preferred_element_type=F32)

    def online_step(j, m, masked):
        kc, vc, delta = chunk(j)
        delta = _tile_lanes(delta, Q_STRIP)
        m_out = []
        for st in range(n_strip):
            lo, hi = st * Q_STRIP, (st + 1) * Q_STRIP
            s = scores(kc, lo, hi)
            if masked:
                key = lax.broadcasted_iota(jnp.int32, (tk, Q_STRIP), 0)
                qry = lax.broadcasted_iota(jnp.int32, (tk, Q_STRIP), 1) + lo
                s = jnp.where(key <= qry, s, -jnp.inf)
            m_old = m[st] - delta
            m_new = jnp.maximum(m_old, jnp.max(s, axis=0, keepdims=True))
            p = jnp.exp2(s - m_new).astype(BF16)
            corr = jnp.exp2(m_old - m_new)
            pv = jnp.dot(vc, p, preferred_element_type=F32)
            acc_ref[:, lo:hi] = acc_ref[:, lo:hi] * corr + pv
            m_out.append(m_new + delta)
        return tuple(m_out)

    m0 = tuple(jnp.full((1, Q_STRIP), -jnp.inf, F32) for _ in range(n_strip))
    m_d = online_step(i, m0, True)
    m_row = jnp.concatenate(m_d, axis=1)

    qf = q.astype(F32)
    qn2 = lax.dot_general(jnp.ones((8, dh), BF16), (qf * qf).astype(BF16),
                          (((1,), (1,)), ((), ())), preferred_element_type=F32)[0:1, :]
    qk_bound = jnp.sqrt(qn2 * _tile_lanes(kall_ref[0:1, :], tq)) * NORM_SLACK + 1.0
    slack = jnp.max(qk_bound - m_row)
    live = jnp.logical_and(slack + (bend_ref[0:1, :] - base_q) >= ZERO_EXP, lane_row < i)
    n_live = jnp.sum(live.astype(jnp.int32))

    @pl.when(slack <= SAFE_EXP)
    def _():
        def fast_step(jj, carry):
            kc, vc, delta = chunk(i - 1 - jj)
            ref = m_row - _tile_lanes(delta, tq)
            for st in range(n_strip):
                lo, hi = st * Q_STRIP, (st + 1) * Q_STRIP
                p = jnp.exp2(scores(kc, lo, hi) - ref[:, lo:hi]).astype(BF16)
                acc_ref[:, lo:hi] += jnp.dot(vc, p, preferred_element_type=F32)
            return carry
        lax.fori_loop(0, n_live, fast_step, 0)

    @pl.when(slack > SAFE_EXP)
    def _():
        lax.fori_loop(0, i, lambda jj, m: online_step(i - 1 - jj, m, False), m_d)

    y = (acc_ref[0:dh, :] / acc_ref[dh:dh + 1, :]).T
    o_ref[...] = (y * _silu(g_ref[...])).astype(o_ref.dtype)


def _attention(qkv, ft, rest, *, d_attn):
    s = qkv.shape[0]
    tq = _pick(s, 1024)
    nh = d_attn // HEAD_DIM
    kern = functools.partial(_attn_kernel, tq=tq, tk=tq)
    return pl.pallas_call(
        kern,
        grid=(nh, s // tq),
        in_specs=[
            pl.BlockSpec((tq, HEAD_DIM), lambda h, i: (i, h)),
            pl.BlockSpec((s, HEAD_DIM), lambda h, i: (0, nh + h)),
            pl.BlockSpec((s, HEAD_DIM), lambda h, i: (0, 2 * nh + h)),
            pl.BlockSpec((1, 1, s), lambda h, i: (h, 0, 0)),
            pl.BlockSpec((tq, HEAD_DIM), lambda h, i: (i, h)),
        ],
        out_specs=pl.BlockSpec((tq, HEAD_DIM), lambda h, i: (i, h)),
        out_shape=jax.ShapeDtypeStruct((s, d_attn), BF16),
        scratch_shapes=[
            pltpu.VMEM((s, 2 * HEAD_DIM), BF16),
            pltpu.VMEM((HEAD_DIM + ONES_ROWS, s), BF16),
            pltpu.VMEM((max(s // tq, 8), LANES), F32),
            pltpu.VMEM((8, LANES), F32),
            pltpu.VMEM((8, LANES), F32),
            pltpu.VMEM((HEAD_DIM + ONES_ROWS, tq), F32),
        ],
        compiler_params=_params(("arbitrary", "arbitrary")),
        name="fox_attention",
    )(qkv, qkv, qkv, ft, rest)


def _pool_kernel(u_ref, prev_ref, g_ref, w_ref, sc_ref, o_ref):
    i = pl.program_id(0)
    tp = u_ref.shape[0]
    u = u_ref[...]
    prev = jnp.where(i > 0, prev_ref[...], 0.0)
    t1 = lax.broadcasted_iota(jnp.int32, (tp, POOL_GROUP), 0) + (i * tp + 1)
    outs = []
    for g, w in enumerate(POOL_WINDOWS):
        lo, hi = g * POOL_GROUP, (g + 1) * POOL_GROUP
        ug = u[:, lo:hi]
        ext = jnp.concatenate([prev[:, lo:hi], ug], axis=0)
        win = ext
        span = 1
        while span < w:
            win = win + pltpu.roll(win, span, axis=0)
            span *= 2
        win = win[POOL_HALO:]
        cnt = jnp.minimum(t1, w).astype(F32)
        pooled = win / cnt - ug
        mixed = jnp.dot(pooled.astype(BF16), w_ref[g], preferred_element_type=F32)
        outs.append(mixed)
    mixed = jnp.concatenate(outs, axis=1) * sc_ref[...]
    o_ref[...] = (mixed * _silu(g_ref[...])).astype(o_ref.dtype)


def _pool(rest, w_pool, pool_scale, *, col_u, col_g, d_pool):
    s = rest.shape[0]
    tp = _pick(s, 1024)
    cu, cg = col_u // d_pool, col_g // d_pool
    halo_blocks = tp // POOL_HALO
    return pl.pallas_call(
        _pool_kernel,
        grid=(s // tp,),
        in_specs=[
            pl.BlockSpec((tp, d_pool), lambda i: (i, cu)),
            pl.BlockSpec((POOL_HALO, d_pool),
                         lambda i: (jnp.maximum(i * halo_blocks - 1, 0), cu)),
            pl.BlockSpec((tp, d_pool), lambda i: (i, cg)),
            pl.BlockSpec(w_pool.shape, lambda i: (0, 0, 0)),
            pl.BlockSpec((1, d_pool), lambda i: (0, 0)),
        ],
        out_specs=pl.BlockSpec((tp, d_pool), lambda i: (i, 0)),
        out_shape=jax.ShapeDtypeStruct((s, d_pool), BF16),
        compiler_params=_params(("arbitrary",)),
        name="pool",
    )(rest, rest, rest, w_pool, pool_scale)


SSM_CHUNK = 8


def _discretise(lr, li, ldt):
    dt = jnp.exp(ldt)
    mag = jnp.exp(lr * dt)
    ab_re = mag * jnp.cos(li * dt)
    ab_im = mag * jnp.sin(li * dt)
    den = lr * lr + li * li
    nr = ab_re - 1.0
    ni = ab_im
    z_re = (nr * lr + ni * li) / den
    z_im = (ni * lr - nr * li) / den
    return ab_re, ab_im, z_re, z_im


def _powers(a_re, a_im, n):
    out = [(jnp.ones_like(a_re), jnp.zeros_like(a_im))]
    for _ in range(n):
        p_re, p_im = out[-1]
        out.append((p_re * a_re - p_im * a_im, p_re * a_im + p_im * a_re))
    return out


def _ssm_prep_kernel(lr_ref, li_ref, ldt_ref, lrc_ref, lic_ref, ldtc_ref,
                     br_ref, bi_ref, cr_ref, ci_ref, t_ref, p_ref, e_ref, al_ref):
    nl = SSM_CHUNK
    cw = br_ref.shape[2]
    a_re, a_im, z_re, z_im = _discretise(lr_ref[0, 0], li_ref[0, 0], ldt_ref[0, 0])
    pw = _powers(a_re, a_im, nl)
    al_ref[0, 0] = jnp.concatenate([pw[nl][0], pw[nl][1]], axis=0)
    br = br_ref[0, 0]
    bi = bi_ref[0, 0]
    bb_re = z_re * br - z_im * bi
    bb_im = z_re * bi + z_im * br
    bb = jnp.concatenate([bb_re, bb_im], axis=1)
    ac_re, ac_im, _, _ = _discretise(lrc_ref[0, 0], lic_ref[0, 0], ldtc_ref[0, 0])
    pwc = _powers(ac_re, ac_im, nl)
    cr = cr_ref[0, 0]
    ci = ci_ref[0, 0]
    ca = [jnp.concatenate([cr * q_re - ci * q_im, -(cr * q_im + ci * q_re)], axis=0)
          for q_re, q_im in pwc]
    kd = [jnp.dot(bb, ca[d], preferred_element_type=F32,
                  precision=lax.Precision.HIGHEST).astype(BF16) for d in range(nl)]
    zero = jnp.zeros((cw, cw), BF16)
    for src in range(nl):
        for dst in range(nl):
            t_ref[0, 0, src * cw:(src + 1) * cw, dst * cw:(dst + 1) * cw] = (
                kd[dst - src] if dst >= src else zero)
        q_re, q_im = pw[nl - 1 - src]
        p_ref[0, 0, src * cw:(src + 1) * cw, :] = jnp.concatenate(
            [bb_re * q_re - bb_im * q_im, bb_re * q_im + bb_im * q_re], axis=1).astype(BF16)
        e_ref[0, 0, :, src * cw:(src + 1) * cw] = ca[src + 1].astype(BF16)


def _ssm_prep(lam_re, lam_im, log_dt, b_re, b_im, c_re, c_im):
    depth, ng, ns = lam_re.shape
    gc = b_re.shape[-1]
    n_slab = ng // SSM_SLAB_GROUPS
    eye = jnp.eye(SSM_SLAB_GROUPS, dtype=F32)
    sw = SSM_SLAB_GROUPS * ns

    def place_b(b):
        b = b.reshape(depth, n_slab, SSM_SLAB_GROUPS, ns, gc).transpose(0, 1, 2, 4, 3)
        return (b[:, :, :, :, None, :] * eye[None, None, :, None, :, None]).reshape(
            depth, n_slab, SSM_SLAB_GROUPS * gc, sw)

    def place_c(c):
        c = c.reshape(depth, n_slab, SSM_SLAB_GROUPS, gc, ns).transpose(0, 1, 2, 4, 3)
        return (c[:, :, :, :, None, :] * eye[None, None, :, None, :, None]).reshape(
            depth, n_slab, sw, SSM_SLAB_GROUPS * gc)

    cw = SSM_SLAB_GROUPS * gc
    nl = SSM_CHUNK
    ldt = jnp.broadcast_to(log_dt[:, :, None], (depth, ng, ns))
    row = lambda v: v.reshape(depth, n_slab, 1, sw)
    col = lambda v: jnp.broadcast_to(v.reshape(depth, n_slab, sw, 1), (depth, n_slab, sw, cw))
    blk = lambda *shape: pl.BlockSpec((1, 1) + shape, lambda l, k: (l, k, 0, 0))
    return pl.pallas_call(
        _ssm_prep_kernel,
        grid=(depth, n_slab),
        in_specs=[blk(1, sw)] * 3 + [blk(sw, cw)] * 3 + [blk(cw, sw)] * 2 + [blk(sw, cw)] * 2,
        out_specs=[blk(nl * cw, nl * cw), blk(nl * cw, 2 * sw), blk(2 * sw, nl * cw),
                   blk(2, sw)],
        out_shape=[
            jax.ShapeDtypeStruct((depth, n_slab, nl * cw, nl * cw), BF16),
            jax.ShapeDtypeStruct((depth, n_slab, nl * cw, 2 * sw), BF16),
            jax.ShapeDtypeStruct((depth, n_slab, 2 * sw, nl * cw), BF16),
            jax.ShapeDtypeStruct((depth, n_slab, 2, sw), F32),
        ],
        compiler_params=_params(("arbitrary", "arbitrary")),
        name="ssm_prep",
    )(row(lam_re), row(lam_im), row(ldt), col(lam_re), col(lam_im), col(ldt),
      place_b(b_re), place_b(b_im), place_c(c_re), place_c(c_im))


def _gelu_tanh(y):
    c = math.sqrt(2.0 / math.pi)
    return 0.5 * y * (1.0 + jnp.tanh(c * (y + 0.044715 * (y * y * y))))


def _ssm_kernel(u_ref, t_ref, p_ref, e_ref, al_ref, d_ref, y_ref, uc_ref, carry_ref):
    i = pl.program_id(1)
    nl = SSM_CHUNK
    ts, cw = u_ref.shape
    rows = ts // nl
    sw = al_ref.shape[3]

    @pl.when(i == 0)
    def _():
        carry_ref[...] = jnp.zeros_like(carry_ref)

    for tau in range(nl):
        uc_ref[:, tau * cw:(tau + 1) * cw] = (
            u_ref[pl.ds(tau, rows, stride=nl), :].astype(BF16))
    uc = uc_ref[...]
    v = jnp.dot(uc, p_ref[0, 0], preferred_element_type=F32)
    xr = v[:, :sw]
    xi = v[:, sw:]
    a_re = al_ref[0, 0, 0:1, :]
    a_im = al_ref[0, 0, 1:2, :]
    c_re = carry_ref[0:1, :]
    c_im = carry_ref[1:2, :]
    row = lax.broadcasted_iota(jnp.int32, (rows, sw), 0)
    first = row == 0
    xr = xr + jnp.where(first, a_re * c_re - a_im * c_im, 0.0)
    xi = xi + jnp.where(first, a_re * c_im + a_im * c_re, 0.0)
    q_re, q_im = a_re, a_im
    d = 1
    while d < rows:
        keep = row >= d
        sr = jnp.where(keep, pltpu.roll(xr, d, axis=0), 0.0)
        si = jnp.where(keep, pltpu.roll(xi, d, axis=0), 0.0)
        xr, xi = xr + (q_re * sr - q_im * si), xi + (q_re * si + q_im * sr)
        q_re, q_im = q_re * q_re - q_im * q_im, 2.0 * (q_re * q_im)
        d *= 2
    pr = jnp.where(first, c_re, pltpu.roll(xr, 1, axis=0))
    pi = jnp.where(first, c_im, pltpu.roll(xi, 1, axis=0))
    carry_ref[0:1, :] = xr[rows - 1:rows, :]
    carry_ref[1:2, :] = xi[rows - 1:rows, :]
    xp = jnp.concatenate([pr, pi], axis=1).astype(BF16)
    yc = (jnp.dot(uc, t_ref[0, 0], preferred_element_type=F32)
          + jnp.dot(xp, e_ref[0, 0], preferred_element_type=F32))
    for tau in range(nl):
        y_ref[pl.ds(tau, rows, stride=nl), :] = (
            yc[:, tau * cw:(tau + 1) * cw]
            + d_ref[...] * u_ref[pl.ds(tau, rows, stride=nl), :])


def _ssm(rest, t_mat, p_mat, e_mat, a_l, d_row, *, layer, col_u):
    s = rest.shape[0]
    n_slab, cw = t_mat.shape[1], d_row.shape[1] // t_mat.shape[1]
    ts = _pick(s, 4096)
    cu = col_u // cw
    sw = a_l.shape[3]
    mat = lambda m: pl.BlockSpec((1, 1) + m.shape[2:], lambda k, i: (layer, k, 0, 0))
    return pl.pallas_call(
        _ssm_kernel,
        grid=(n_slab, s // ts),
        in_specs=[
            pl.BlockSpec((ts, cw), lambda k, i: (i, cu + k)),
            mat(t_mat), mat(p_mat), mat(e_mat), mat(a_l),
            pl.BlockSpec((1, cw), lambda k, i: (0, k)),
        ],
        out_specs=pl.BlockSpec((ts, cw), lambda k, i: (i, k)),
        out_shape=jax.ShapeDtypeStruct((s, n_slab * cw), F32),
        scratch_shapes=[pltpu.VMEM((ts // SSM_CHUNK, SSM_CHUNK * cw), BF16),
                        pltpu.VMEM((2, sw), F32)],
        compiler_params=_params(("arbitrary", "arbitrary")),
        name="ssm",
    )(rest, t_mat, p_mat, e_mat, a_l, d_row)


def _glu_kernel(y_ref, g_ref, wg_ref, bg_ref, o_ref):
    y = _gelu_tanh(y_ref[...])
    z = jnp.dot(y.astype(BF16), wg_ref[...], preferred_element_type=F32) + bg_ref[...]
    o_ref[...] = (y * _sigmoid(z) * _silu(g_ref[...])).astype(o_ref.dtype)


def _glu(y, rest, w_glu, b_glu, *, col_g):
    s, d_ssm = y.shape
    tm = _pick(s, 1024)
    cg = col_g // d_ssm
    return pl.pallas_call(
        _glu_kernel,
        grid=(s // tm,),
        in_specs=[
            pl.BlockSpec((tm, d_ssm), lambda i: (i, 0)),
            pl.BlockSpec((tm, d_ssm), lambda i: (i, cg)),
            pl.BlockSpec(w_glu.shape, lambda i: (0, 0)),
            pl.BlockSpec((1, d_ssm), lambda i: (0, 0)),
        ],
        out_specs=pl.BlockSpec((tm, d_ssm), lambda i: (i, 0)),
        out_shape=jax.ShapeDtypeStruct((s, d_ssm), BF16),
        compiler_params=_params(("arbitrary",)),
        name="ssm_glu",
    )(y, rest, w_glu, b_glu)


def _outproj_kernel(ya_ref, yp_ref, ys_ref, w_ref, x_ref, mod_ref, fg_ref, o_ref, y_ref,
                    *, final):
    j = pl.program_id(1)
    da = ya_ref.shape[1]
    dp = yp_ref.shape[1]
    tm, tn = x_ref.shape

    @pl.when(j == 0)
    def _():
        y_ref[:, 0:da] = ya_ref[...]
        y_ref[:, da:da + dp] = yp_ref[...]
        y_ref[:, da + dp:] = ys_ref[...]

    cm = tm // ROW_CHUNKS
    col = pl.multiple_of(j * tn, tn)
    for c in range(ROW_CHUNKS):
        out = jnp.dot(y_ref[c * cm:(c + 1) * cm, :], w_ref[0], preferred_element_type=F32)
        val = x_ref[c * cm:(c + 1) * cm, :] + mod_ref[...] * out
        if final:
            o_ref[c * cm:(c + 1) * cm, pl.ds(col, tn)] = val
        else:
            o_ref[c * cm:(c + 1) * cm, :] = val

    if final:
        @pl.when(j == pl.num_programs(1) - 1)
        def _():
            xn = o_ref[...]
            ms = jnp.mean(xn * xn, axis=-1, keepdims=True)
            o_ref[...] = xn * lax.rsqrt(ms + NORM_EPS) * fg_ref[...]


def _outproj(ya, yp, ysm, w_out, x, gate, final_g, *, final):
    s, d = x.shape
    dm = w_out.shape[1]
    tm = _pick(s, 1024)
    tn = PROJ_TN
    out_spec = (pl.BlockSpec((tm, d), lambda i, j: (i, 0)) if final
                else pl.BlockSpec((tm, tn), lambda i, j: (i, j)))
    return pl.pallas_call(
        functools.partial(_outproj_kernel, final=final),
        grid=(s // tm, d // tn),
        in_specs=[
            pl.BlockSpec((tm, ya.shape[1]), lambda i, j: (i, 0)),
            pl.BlockSpec((tm, yp.shape[1]), lambda i, j: (i, 0)),
            pl.BlockSpec((tm, ysm.shape[1]), lambda i, j: (i, 0)),
            pl.BlockSpec((1, dm, tn), lambda i, j: (j, 0, 0)),
            pl.BlockSpec((tm, tn), lambda i, j: (i, j)),
            pl.BlockSpec((1, tn), lambda i, j: (0, j)),
            pl.BlockSpec((1, d), lambda i, j: (0, 0)),
        ],
        out_specs=out_spec,
        out_shape=jax.ShapeDtypeStruct((s, d), F32),
        scratch_shapes=[pltpu.VMEM((tm, dm), BF16)],
        compiler_params=_params(("arbitrary", "arbitrary")),
        name="outproj",
    )(ya, yp, ysm, w_out, x, gate, final_g)


def kernel(x, c, norm_g, w_ada, b_ada, w_in, b_f, w_pool, pool_scale, lam_re, lam_im,
           ssm_b_re, ssm_b_im, ssm_c_re, ssm_c_im, ssm_d, log_dt, w_glu, b_glu, w_out,
           final_g):
    b, s, d = x.shape
    assert b == 1
    depth = w_in.shape[0]
    d_pool = pool_scale.shape[1]
    d_ssm = b_glu.shape[1]
    d_attn = N_HEADS * HEAD_DIM
    n_f = b_f.shape[1]
    assert n_f == N_HEADS and w_in.shape[2] == 4 * d_attn + n_f + 2 * d_pool + 2 * d_ssm

    w_main = _col_tiles(jnp.concatenate(
        [w_in[:, :, :4 * d_attn], w_in[:, :, 4 * d_attn + n_f:]], axis=2).astype(BF16))
    w_f = jnp.pad(w_in[:, :, 4 * d_attn:4 * d_attn + n_f],
                  ((0, 0), (0, 0), (0, LANES - n_f))).astype(BF16)
    b_f_row = jnp.pad(b_f, ((0, 0), (0, LANES - n_f))).reshape(depth, 1, LANES)
    col_up = d_attn
    col_gp = col_up + d_pool
    col_us = col_gp + d_pool
    col_gs = col_us + d_ssm

    mod = _ada_mod(c, w_ada, b_ada).reshape(depth, 3, d)
    t_all, p_all, e_all, al_all = _ssm_prep(lam_re, lam_im, log_dt, ssm_b_re, ssm_b_im,
                                            ssm_c_re, ssm_c_im)
    w_pool_b = w_pool.astype(BF16)
    w_glu_b = w_glu.astype(BF16)
    w_out_b = _col_tiles(w_out.astype(BF16))

    xs = x.reshape(s, d)
    for l in range(depth):
        qkv, rest, ft = _inproj(xs, norm_g[l].reshape(1, d), mod[l], w_main[l], w_f[l],
                                b_f_row[l], d_attn=d_attn)
        ya = _attention(qkv, ft, rest, d_attn=d_attn)
        yp = _pool(rest, w_pool_b[l], pool_scale[l].reshape(1, d_pool),
                   col_u=col_up, col_g=col_gp, d_pool=d_pool)
        y_ssm = _ssm(rest, t_all, p_all, e_all, al_all, ssm_d[l].reshape(1, d_ssm),
                     layer=l, col_u=col_us)
        ysm = _glu(y_ssm, rest, w_glu_b[l], b_glu[l].reshape(1, d_ssm), col_g=col_gs)
        xs = _outproj(ya, yp, ysm, w_out_b[l], xs, mod[l, 2:3, :], final_g.reshape(1, d),
                      final=(l == depth - 1))
    return xs.reshape(b, s, d).astype(x.dtype)
```

```python
import functools
import math

import jax
import jax.numpy as jnp
from jax import lax
from jax.experimental import pallas as pl
from jax.experimental.pallas import tpu as pltpu

F32 = jnp.float32
BF16 = jnp.bfloat16

N_HEADS = 8
HEAD_DIM = 128
POOL_WINDOWS = (2, 4, 8, 16)
POOL_GROUP = 128
POOL_HALO = 16
SSM_GROUP = 16
SSM_STATE = 64
SSM_SLAB_GROUPS = 8
NORM_EPS = 1e-6
LANES = 128
VMEM_LIMIT = 56 * 1024 * 1024
PROJ_TN = 512


def _params(sem, vmem=VMEM_LIMIT):
    return pltpu.CompilerParams(dimension_semantics=sem, vmem_limit_bytes=vmem)


def _sigmoid(x):
    return 1.0 / (1.0 + jnp.exp(-x))


def _silu(x):
    return x * _sigmoid(x)


def _pick(n, pref):
    t = min(n, pref)
    while n % t:
        t //= 2
    return t


def _ada_kernel(c_ref, w_ref, b_ref, o_ref):
    ca = _silu(c_ref[...])
    o_ref[0] = jnp.sum(w_ref[0] * ca, axis=0, keepdims=True) + b_ref[0]


def _ada_mod(c, w_ada, b_ada):
    depth, d, n = w_ada.shape
    tn = _pick(n, 1024)
    return pl.pallas_call(
        _ada_kernel,
        grid=(depth, n // tn),
        in_specs=[
            pl.BlockSpec((d, 1), lambda l, j: (0, 0)),
            pl.BlockSpec((1, d, tn), lambda l, j: (l, 0, j)),
            pl.BlockSpec((1, 1, tn), lambda l, j: (l, 0, j)),
        ],
        out_specs=pl.BlockSpec((1, 1, tn), lambda l, j: (l, 0, j)),
        out_shape=jax.ShapeDtypeStruct((depth, 1, n), F32),
        compiler_params=_params(("arbitrary", "arbitrary")),
        name="ada_mod",
    )(c.reshape(d, 1), w_ada, b_ada.reshape(depth, 1, n))


def _inproj_kernel(x_ref, g_ref, mod_ref, w_ref, wf_ref, bf_ref,
                   qkv_ref, rest_ref, ft_ref, h_ref, carry_ref, *, d_attn, q_scale):
    i = pl.program_id(0)
    tm = x_ref.shape[0]
    n_qkv = qkv_ref.shape[1]
    n = w_ref.shape[1]
    tn = PROJ_TN

    x = x_ref[...]
    ms = jnp.mean(x * x, axis=-1, keepdims=True)
    shift = mod_ref[0:1, :]
    scale = mod_ref[1:2, :]
    h = (x * lax.rsqrt(ms + NORM_EPS) * g_ref[...]) * (1.0 + scale) + shift
    h_ref[...] = h.astype(BF16)

    for c0 in range(0, n, tn):
        proj = jnp.dot(h_ref[...], w_ref[:, c0:c0 + tn], preferred_element_type=F32)
        if c0 < d_attn:
            qkv_ref[:, c0:c0 + tn] = (proj * q_scale).astype(BF16)
        elif c0 < n_qkv:
            qkv_ref[:, c0:c0 + tn] = proj.astype(BF16)
        else:
            rest_ref[:, c0 - n_qkv:c0 - n_qkv + tn] = proj

    f = jnp.dot(h_ref[...], wf_ref[...], preferred_element_type=F32) + bf_ref[...]
    logf = -(jnp.maximum(-f, 0.0) + jnp.log1p(jnp.exp(-jnp.abs(f))))
    row = lax.broadcasted_iota(jnp.int32, logf.shape, 0)
    cum = logf
    d = 1
    while d < tm:
        cum = cum + jnp.where(row >= d, pltpu.roll(cum, d, axis=0), 0.0)
        d *= 2

    @pl.when(i == 0)
    def _():
        carry_ref[...] = jnp.zeros_like(carry_ref)

    cum = cum + carry_ref[0:1, :]
    carry_ref[...] = jnp.broadcast_to(cum[tm - 1:tm, :], carry_ref.shape)
    cum_t = cum.T
    for hh in range(N_HEADS):
        ft_ref[hh] = cum_t[hh:hh + 1, :]


def _inproj(x, g, mod, w_main, w_f, b_f_row, *, d_attn):
    s, d = x.shape
    n = w_main.shape[1]
    n_qkv = 3 * d_attn
    tm = _pick(s, 256)
    kern = functools.partial(_inproj_kernel, d_attn=d_attn,
                             q_scale=HEAD_DIM ** -0.5 * math.log2(math.e))
    once = pl.Buffered(1)
    return pl.pallas_call(
        kern,
        grid=(s // tm,),
        in_specs=[
            pl.BlockSpec((tm, d), lambda i: (i, 0)),
            pl.BlockSpec((1, d), lambda i: (0, 0)),
            pl.BlockSpec((3, d), lambda i: (0, 0)),
            pl.BlockSpec((d, n), lambda i: (0, 0), pipeline_mode=once),
            pl.BlockSpec((d, LANES), lambda i: (0, 0), pipeline_mode=once),
            pl.BlockSpec((1, LANES), lambda i: (0, 0)),
        ],
        out_specs=[
            pl.BlockSpec((tm, n_qkv), lambda i: (i, 0)),
            pl.BlockSpec((tm, n - n_qkv), lambda i: (i, 0)),
            pl.BlockSpec((N_HEADS, 1, tm), lambda i: (0, 0, i)),
        ],
        out_shape=[
            jax.ShapeDtypeStruct((s, n_qkv), BF16),
            jax.ShapeDtypeStruct((s, n - n_qkv), F32),
            jax.ShapeDtypeStruct((N_HEADS, 1, s), F32),
        ],
        scratch_shapes=[pltpu.VMEM((tm, d), BF16), pltpu.VMEM((8, LANES), F32)],
        compiler_params=_params(("arbitrary",)),
        name="inproj",
    )(x, g, mod, w_main, w_f, b_f_row)


AUG_TERMS = 3
ONES_ROWS = 16
Q_STRIP = 1024
LOG2E = math.log2(math.e)


SAFE_EXP = 60.0
ZERO_EXP = -150.0
NORM_SLACK = 1.02


def _tile_lanes(row, n):
    return jnp.concatenate([row] * (n // LANES), axis=1)


def _attn_kernel(q_ref, k_ref, v_ref, ft_ref, g_ref, o_ref,
                 kaug_ref, vt_ref, base_ref, bend_ref, kall_ref, acc_ref, *, tq, tk):
    i = pl.program_id(1)
    seq = k_ref.shape[0]
    dh = HEAD_DIM
    lane = lax.broadcasted_iota(jnp.int32, (LANES, LANES), 1)
    lane_row = lax.broadcasted_iota(jnp.int32, (1, LANES), 1)
    ones_sq = jnp.ones((LANES, LANES), BF16)

    @pl.when(i == 0)
    def _():
        vt_ref[dh:, :] = jnp.ones((ONES_ROWS, seq), BF16)
        kall_ref[...] = jnp.zeros_like(kall_ref)
        bend_ref[...] = jnp.zeros_like(bend_ref)

        def fill(c, carry):
            base = None
            col = None
            for bb in range(tk // LANES):
                off = pl.multiple_of(c * tk + bb * LANES, LANES)
                row = ft_ref[0, :, pl.ds(off, LANES)]
                col = (-LOG2E) * jnp.broadcast_to(row, (LANES, LANES)).T
                if bb == 0:
                    base = col[0:1, :]
                    base_ref[pl.ds(c, 1), :] = base
                rel = col - base
                hi = rel.astype(BF16).astype(F32)
                mid = (rel - hi).astype(BF16).astype(F32)
                lo = (rel - hi - mid).astype(BF16).astype(F32)
                aug = jnp.where(lane == 0, hi, jnp.where(lane == 1, mid,
                                                         jnp.where(lane == 2, lo, 0.0)))
                kb = k_ref[pl.ds(off, LANES), :]
                kaug_ref[pl.ds(off, LANES), 0:dh] = kb
                kaug_ref[pl.ds(off, LANES), dh:] = aug.astype(BF16)
                vt_ref[0:dh, pl.ds(off, LANES)] = (
                    v_ref[pl.ds(off, LANES), :].astype(F32).T.astype(BF16))
                kf = kb.astype(F32)
                n2 = jnp.dot((kf * kf).astype(BF16), ones_sq, preferred_element_type=F32)
                kall_ref[0:1, :] = jnp.maximum(kall_ref[0:1, :],
                                               jnp.max(n2, axis=0, keepdims=True))
            bend_ref[0:1, :] = jnp.where(lane_row == c, col[LANES - 1:LANES, :],
                                         bend_ref[0:1, :])
            return carry
        lax.fori_loop(0, seq // tk, fill, 0)

    lane_q = lax.broadcasted_iota(jnp.int32, (tq, LANES), 1)
    q = q_ref[...]
    q_aug = jnp.concatenate(
        [q, jnp.where(lane_q < AUG_TERMS, 1.0, 0.0).astype(BF16)], axis=1)
    acc_ref[...] = jnp.zeros_like(acc_ref)
    base_q = base_ref[pl.ds(i, 1), :]
    n_strip = tq // Q_STRIP

    def chunk(j):
        k_off = pl.multiple_of(j * tk, tk)
        kc = kaug_ref[pl.ds(k_off, tk), :]
        vc = vt_ref[:, pl.ds(k_off, tk)]
        delta = base_ref[pl.ds(j, 1), :] - base_q
        return kc, vc, delta

    def scores(kc, lo, hi):
        return lax.dot_general(kc, q_aug[lo:hi, :], (((1,), (1,)), ((), ())),
                               preferred_element_type=F32)

    def online_step(j, m, masked):
        kc, vc, delta = chunk(j)
        delta = _tile_lanes(delta, Q_STRIP)
        m_out = []
        for st in range(n_strip):
            lo, hi = st * Q_STRIP, (st + 1) * Q_STRIP
            s = scores(kc, lo, hi)
            if masked:
                key = lax.broadcasted_iota(jnp.int32, (tk, Q_STRIP), 0)
                qry = lax.broadcasted_iota(jnp.int32, (tk, Q_STRIP), 1) + lo
                s = jnp.where(key <= qry, s, -jnp.inf)
            m_old = m[st] - delta
            m_new = jnp.maximum(m_old, jnp.max(s, axis=0, keepdims=True))
            p = jnp.exp2(s - m_new).astype(BF16)
            corr = jnp.exp2(m_old - m_new)
            pv = jnp.dot(vc, p, preferred_element_type=F32)
            acc_ref[:, lo:hi] = acc_ref[:, lo:hi] * corr + pv
            m_out.append(m_new + delta)
        return tuple(m_out)

    m0 = tuple(jnp.full((1, Q_STRIP), -jnp.inf, F32) for _ in range(n_strip))
    m_d = online_step(i, m0, True)
    m_row = jnp.concatenate(m_d, axis=1)

    qf = q.astype(F32)
    qn2 = lax.dot_general(jnp.ones((8, dh), BF16), (qf * qf).astype(BF16),
                          (((1,), (1,)), ((), ())), preferred_element_type=F32)[0:1, :]
    qk_bound = jnp.sqrt(qn2 * _tile_lanes(kall_ref[0:1, :], tq)) * NORM_SLACK + 1.0
    slack = jnp.max(qk_bound - m_row)
    live = jnp.logical_and(slack + (bend_ref[0:1, :] - base_q) >= ZERO_EXP, lane_row < i)
    n_live = jnp.sum(live.astype(jnp.int32))

    @pl.when(slack <= SAFE_EXP)
    def _():
        def fast_step(jj, carry):
            kc, vc, delta = chunk(i - 1 - jj)
            ref = m_row - _tile_lanes(delta, tq)
            for st in range(n_strip):
                lo, hi = st * Q_STRIP, (st + 1) * Q_STRIP
                p = jnp.exp2(scores(kc, lo, hi) - ref[:, lo:hi]).astype(BF16)
                acc_ref[:, lo:hi] += jnp.dot(vc, p, preferred_element_type=F32)
            return carry
        lax.fori_loop(0, n_live, fast_step, 0)

    @pl.when(slack > SAFE_EXP)
    def _():
        lax.fori_loop(0, i, lambda jj, m: online_step(i - 1 - jj, m, False), m_d)

    y = (acc_ref[0:dh, :] / acc_ref[dh:dh + 1, :]).T
    o_ref[...] = (y * _silu(g_ref[...])).astype(o_ref.dtype)


def _attention(qkv, ft, rest, *, d_attn):
    s = qkv.shape[0]
    tq = _pick(s, 1024)
    nh = d_attn // HEAD_DIM
    kern = functools.partial(_attn_kernel, tq=tq, tk=tq)
    return pl.pallas_call(
        kern,
        grid=(nh, s // tq),
        in_specs=[
            pl.BlockSpec((tq, HEAD_DIM), lambda h, i: (i, h)),
            pl.BlockSpec((s, HEAD_DIM), lambda h, i: (0, nh + h)),
            pl.BlockSpec((s, HEAD_DIM), lambda h, i: (0, 2 * nh + h)),
            pl.BlockSpec((1, 1, s), lambda h, i: (h, 0, 0)),
            pl.BlockSpec((tq, HEAD_DIM), lambda h, i: (i, h)),
        ],
        out_specs=pl.BlockSpec((tq, HEAD_DIM), lambda h, i: (i, h)),
        out_shape=jax.ShapeDtypeStruct((s, d_attn), BF16),
        scratch_shapes=[
            pltpu.VMEM((s, 2 * HEAD_DIM), BF16),
            pltpu.VMEM((HEAD_DIM + ONES_ROWS, s), BF16),
            pltpu.VMEM((max(s // tq, 8), LANES), F32),
            pltpu.VMEM((8, LANES), F32),
            pltpu.VMEM((8, LANES), F32),
            pltpu.VMEM((HEAD_DIM + ONES_ROWS, tq), F32),
        ],
        compiler_params=_params(("arbitrary", "arbitrary")),
        name="fox_attention",
    )(qkv, qkv, qkv, ft, rest)


def _pool_kernel(u_ref, prev_ref, g_ref, w_ref, sc_ref, o_ref):
    i = pl.program_id(0)
    tp = u_ref.shape[0]
    u = u_ref[...]
    prev = jnp.where(i > 0, prev_ref[...], 0.0)
    t1 = lax.broadcasted_iota(jnp.int32, (tp, POOL_GROUP), 0) + (i * tp + 1)
    outs = []
    for g, w in enumerate(POOL_WINDOWS):
        lo, hi = g * POOL_GROUP, (g + 1) * POOL_GROUP
        ug = u[:, lo:hi]
        ext = jnp.concatenate([prev[:, lo:hi], ug], axis=0)
        win = ext
        span = 1
        while span < w:
            win = win + pltpu.roll(win, span, axis=0)
            span *= 2
        win = win[POOL_HALO:]
        cnt = jnp.minimum(t1, w).astype(F32)
        pooled = win / cnt - ug
        mixed = jnp.dot(pooled.astype(BF16), w_ref[g], preferred_element_type=F32)
        outs.append(mixed)
    mixed = jnp.concatenate(outs, axis=1) * sc_ref[...]
    o_ref[...] = (mixed * _silu(g_ref[...])).astype(o_ref.dtype)


def _pool(rest, w_pool, pool_scale, *, col_u, col_g, d_pool):
    s = rest.shape[0]
    tp = _pick(s, 1024)
    cu, cg = col_u // d_pool, col_g // d_pool
    halo_blocks = tp // POOL_HALO
    return pl.pallas_call(
        _pool_kernel,
        grid=(s // tp,),
        in_specs=[
            pl.BlockSpec((tp, d_pool), lambda i: (i, cu)),
            pl.BlockSpec((POOL_HALO, d_pool),
                         lambda i: (jnp.maximum(i * halo_blocks - 1, 0), cu)),
            pl.BlockSpec((tp, d_pool), lambda i: (i, cg)),
            pl.BlockSpec(w_pool.shape, lambda i: (0, 0, 0)),
            pl.BlockSpec((1, d_pool), lambda i: (0, 0)),
        ],
        out_specs=pl.BlockSpec((tp, d_pool), lambda i: (i, 0)),
        out_shape=jax.ShapeDtypeStruct((s, d_pool), BF16),
        compiler_params=_params(("arbitrary",)),
        name="pool",
    )(rest, rest, rest, w_pool, pool_scale)


SSM_CHUNK = 8


def _discretise(lr, li, ldt):
    dt = jnp.exp(ldt)
    mag = jnp.exp(lr * dt)
    ab_re = mag * jnp.cos(li * dt)
    ab_im = mag * jnp.sin(li * dt)
    den = lr * lr + li * li
    nr = ab_re - 1.0
    ni = ab_im
    z_re = (nr * lr + ni * li) / den
    z_im = (ni * lr - nr * li) / den
    return ab_re, ab_im, z_re, z_im


def _powers(a_re, a_im, n):
    out = [(jnp.ones_like(a_re), jnp.zeros_like(a_im))]
    for _ in range(n):
        p_re, p_im = out[-1]
        out.append((p_re * a_re - p_im * a_im, p_re * a_im + p_im * a_re))
    return out


def _ssm_prep_kernel(lr_ref, li_ref, ldt_ref, lrc_ref, lic_ref, ldtc_ref,
                     br_ref, bi_ref, cr_ref, ci_ref, t_ref, p_ref, e_ref, al_ref):
    nl = SSM_CHUNK
    cw = br_ref.shape[2]
    a_re, a_im, z_re, z_im = _discretise(lr_ref[0, 0], li_ref[0, 0], ldt_ref[0, 0])
    pw = _powers(a_re, a_im, nl)
    al_ref[0, 0] = jnp.concatenate([pw[nl][0], pw[nl][1]], axis=0)
    br = br_ref[0, 0]
    bi = bi_ref[0, 0]
    bb_re = z_re * br - z_im * bi
    bb_im = z_re * bi + z_im * br
    bb = jnp.concatenate([bb_re, bb_im], axis=1)
    ac_re, ac_im, _, _ = _discretise(lrc_ref[0, 0], lic_ref[0, 0], ldtc_ref[0, 0])
    pwc = _powers(ac_re, ac_im, nl)
    cr = cr_ref[0, 0]
    ci = ci_ref[0, 0]
    ca = [jnp.concatenate([cr * q_re - ci * q_im, -(cr * q_im + ci * q_re)], axis=0)
          for q_re, q_im in pwc]
    kd = [jnp.dot(bb, ca[d], preferred_element_type=F32,
                  precision=lax.Precision.HIGHEST).astype(BF16) for d in range(nl)]
    zero = jnp.zeros((cw, cw), BF16)
    for src in range(nl):
        for dst in range(nl):
            t_ref[0, 0, src * cw:(src + 1) * cw, dst * cw:(dst + 1) * cw] = (
                kd[dst - src] if dst >= src else zero)
        q_re, q_im = pw[nl - 1 - src]
        p_ref[0, 0, src * cw:(src + 1) * cw, :] = jnp.concatenate(
            [bb_re * q_re - bb_im * q_im, bb_re * q_im + bb_im * q_re], axis=1).astype(BF16)
        e_ref[0, 0, :, src * cw:(src + 1) * cw] = ca[src + 1].astype(BF16)


def _ssm_prep(lam_re, lam_im, log_dt, b_re, b_im, c_re, c_im):
    depth, ng, ns = lam_re.shape
    gc = b_re.shape[-1]
    n_slab = ng // SSM_SLAB_GROUPS
    eye = jnp.eye(SSM_SLAB_GROUPS, dtype=F32)
    sw = SSM_SLAB_GROUPS * ns

    def place_b(b):
        b = b.reshape(depth, n_slab, SSM_SLAB_GROUPS, ns, gc).transpose(0, 1, 2, 4, 3)
        return (b[:, :, :, :, None, :] * eye[None, None, :, None, :, None]).reshape(
            depth, n_slab, SSM_SLAB_GROUPS * gc, sw)

    def place_c(c):
        c = c.reshape(depth, n_slab, SSM_SLAB_GROUPS, gc, ns).transpose(0, 1, 2, 4, 3)
        return (c[:, :, :, :, None, :] * eye[None, None, :, None, :, None]).reshape(
            depth, n_slab, sw, SSM_SLAB_GROUPS * gc)

    cw = SSM_SLAB_GROUPS * gc
    nl = SSM_CHUNK
    ldt = jnp.broadcast_to(log_dt[:, :, None], (depth, ng, ns))
    row = lambda v: v.reshape(depth, n_slab, 1, sw)
    col = lambda v: jnp.broadcast_to(v.reshape(depth, n_slab, sw, 1), (depth, n_slab, sw, cw))
    blk = lambda *shape: pl.BlockSpec((1, 1) + shape, lambda l, k: (l, k, 0, 0))
    return pl.pallas_call(
        _ssm_prep_kernel,
        grid=(depth, n_slab),
        in_specs=[blk(1, sw)] * 3 + [blk(sw, cw)] * 3 + [blk(cw, sw)] * 2 + [blk(sw, cw)] * 2,
        out_specs=[blk(nl * cw, nl * cw), blk(nl * cw, 2 * sw), blk(2 * sw, nl * cw),
                   blk(2, sw)],
        out_shape=[
            jax.ShapeDtypeStruct((depth, n_slab, nl * cw, nl * cw), BF16),
            jax.ShapeDtypeStruct((depth, n_slab, nl * cw, 2 * sw), BF16),
            jax.ShapeDtypeStruct((depth, n_slab, 2 * sw, nl * cw), BF16),
            jax.ShapeDtypeStruct((depth, n_slab, 2, sw), F32),
        ],
        compiler_params=_params(("arbitrary", "arbitrary")),
        name="ssm_prep",
    )(row(lam_re), row(lam_im), row(ldt), col(lam_re), col(lam_im), col(ldt),
      place_b(b_re), place_b(b_im), place_c(c_re), place_c(c_im))


def _gelu_tanh(y):
    c = math.sqrt(2.0 / math.pi)
    return 0.5 * y * (1.0 + jnp.tanh(c * (y + 0.044715 * (y * y * y))))


def _ssm_kernel(u_ref, t_ref, p_ref, e_ref, al_ref, d_ref, y_ref, uc_ref, carry_ref):
    i = pl.program_id(1)
    nl = SSM_CHUNK
    ts, cw = u_ref.shape
    rows = ts // nl
    sw = al_ref.shape[3]

    @pl.when(i == 0)
    def _():
        carry_ref[...] = jnp.zeros_like(carry_ref)

    for tau in range(nl):
        uc_ref[:, tau * cw:(tau + 1) * cw] = (
            u_ref[pl.ds(tau, rows, stride=nl), :].astype(BF16))
    uc = uc_ref[...]
    v = jnp.dot(uc, p_ref[0, 0], preferred_element_type=F32)
    xr = v[:, :sw]
    xi = v[:, sw:]
    a_re = al_ref[0, 0, 0:1, :]
    a_im = al_ref[0, 0, 1:2, :]
    c_re = carry_ref[0:1, :]
    c_im = carry_ref[1:2, :]
    row = lax.broadcasted_iota(jnp.int32, (rows, sw), 0)
    first = row == 0
    xr = xr + jnp.where(first, a_re * c_re - a_im * c_im, 0.0)
    xi = xi + jnp.where(first, a_re * c_im + a_im * c_re, 0.0)
    q_re, q_im = a_re, a_im
    d = 1
    while d < rows:
        keep = row >= d
        sr = jnp.where(keep, pltpu.roll(xr, d, axis=0), 0.0)
        si = jnp.where(keep, pltpu.roll(xi, d, axis=0), 0.0)
        xr, xi = xr + (q_re * sr - q_im * si), xi + (q_re * si + q_im * sr)
        q_re, q_im = q_re * q_re - q_im * q_im, 2.0 * (q_re * q_im)
        d *= 2
    pr = jnp.where(first, c_re, pltpu.roll(xr, 1, axis=0))
    pi = jnp.where(first, c_im, pltpu.roll(xi, 1, axis=0))
    carry_ref[0:1, :] = xr[rows - 1:rows, :]
    carry_ref[1:2, :] = xi[rows - 1:rows, :]
    xp = jnp.concatenate([pr, pi], axis=1).astype(BF16)
    yc = (jnp.dot(uc, t_ref[0, 0], preferred_element_type=F32)
          + jnp.dot(xp, e_ref[0, 0], preferred_element_type=F32))
    for tau in range(nl):
        y_ref[pl.ds(tau, rows, stride=nl), :] = (
            yc[:, tau * cw:(tau + 1) * cw]
            + d_ref[...] * u_ref[pl.ds(tau, rows, stride=nl), :])


def _ssm(rest, t_mat, p_mat, e_mat, a_l, d_row, *, layer, col_u):
    s = rest.shape[0]
    n_slab, cw = t_mat.shape[1], d_row.shape[1] // t_mat.shape[1]
    ts = _pick(s, 4096)
    cu = col_u // cw
    sw = a_l.shape[3]
    mat = lambda m: pl.BlockSpec((1, 1) + m.shape[2:], lambda k, i: (layer, k, 0, 0))
    return pl.pallas_call(
        _ssm_kernel,
        grid=(n_slab, s // ts),
        in_specs=[
            pl.BlockSpec((ts, cw), lambda k, i: (i, cu + k)),
            mat(t_mat), mat(p_mat), mat(e_mat), mat(a_l),
            pl.BlockSpec((1, cw), lambda k, i: (0, k)),
        ],
        out_specs=pl.BlockSpec((ts, cw), lambda k, i: (i, k)),
        out_shape=jax.ShapeDtypeStruct((s, n_slab * cw), F32),
        scratch_shapes=[pltpu.VMEM((ts // SSM_CHUNK, SSM_CHUNK * cw), BF16),
                        pltpu.VMEM((2, sw), F32)],
        compiler_params=_params(("arbitrary", "arbitrary")),
        name="ssm",
    )(rest, t_mat, p_mat, e_mat, a_l, d_row)


def _glu_kernel(y_ref, g_ref, wg_ref, bg_ref, o_ref):
    y = _gelu_tanh(y_ref[...])
    z = jnp.dot(y.astype(BF16), wg_ref[...], preferred_element_type=F32) + bg_ref[...]
    o_ref[...] = (y * _sigmoid(z) * _silu(g_ref[...])).astype(o_ref.dtype)


def _glu(y, rest, w_glu, b_glu, *, col_g):
    s, d_ssm = y.shape
    tm = _pick(s, 1024)
    cg = col_g // d_ssm
    return pl.pallas_call(
        _glu_kernel,
        grid=(s // tm,),
        in_specs=[
            pl.BlockSpec((tm, d_ssm), lambda i: (i, 0)),
            pl.BlockSpec((tm, d_ssm), lambda i: (i, cg)),
            pl.BlockSpec(w_glu.shape, lambda i: (0, 0)),
            pl.BlockSpec((1, d_ssm), lambda i: (0, 0)),
        ],
        out_specs=pl.BlockSpec((tm, d_ssm), lambda i: (i, 0)),
        out_shape=jax.ShapeDtypeStruct((s, d_ssm), BF16),
        compiler_params=_params(("arbitrary",)),
        name="ssm_glu",
    )(y, rest, w_glu, b_glu)


def _outproj_kernel(ya_ref, yp_ref, ys_ref, w_ref, x_ref, mod_ref, fg_ref, o_ref, y_ref,
                    *, final):
    da = ya_ref.shape[1]
    dp = yp_ref.shape[1]
    d = x_ref.shape[1]
    tn = PROJ_TN
    y_ref[:, 0:da] = ya_ref[...]
    y_ref[:, da:da + dp] = yp_ref[...]
    y_ref[:, da + dp:] = ys_ref[...]
    for c0 in range(0, d, tn):
        out = jnp.dot(y_ref[...], w_ref[:, c0:c0 + tn], preferred_element_type=F32)
        o_ref[:, c0:c0 + tn] = x_ref[:, c0:c0 + tn] + mod_ref[:, c0:c0 + tn] * out
    if final:
        xn = o_ref[...]
        ms = jnp.mean(xn * xn, axis=-1, keepdims=True)
        o_ref[...] = xn * lax.rsqrt(ms + NORM_EPS) * fg_ref[...]


def _outproj(ya, yp, ysm, w_out, x, gate, final_g, *, final):
    s, d = x.shape
    dm = w_out.shape[0]
    tm = _pick(s, 512)
    return pl.pallas_call(
        functools.partial(_outproj_kernel, final=final),
        grid=(s // tm,),
        in_specs=[
            pl.BlockSpec((tm, ya.shape[1]), lambda i: (i, 0)),
            pl.BlockSpec((tm, yp.shape[1]), lambda i: (i, 0)),
            pl.BlockSpec((tm, ysm.shape[1]), lambda i: (i, 0)),
            pl.BlockSpec((dm, d), lambda i: (0, 0), pipeline_mode=pl.Buffered(1)),
            pl.BlockSpec((tm, d), lambda i: (i, 0)),
            pl.BlockSpec((1, d), lambda i: (0, 0)),
            pl.BlockSpec((1, d), lambda i: (0, 0)),
        ],
        out_specs=pl.BlockSpec((tm, d), lambda i: (i, 0)),
        out_shape=jax.ShapeDtypeStruct((s, d), F32),
        scratch_shapes=[pltpu.VMEM((tm, dm), BF16)],
        compiler_params=_params(("arbitrary",)),
        name="outproj",
    )(ya, yp, ysm, w_out, x, gate, final_g)


def kernel(x, c, norm_g, w_ada, b_ada, w_in, b_f, w_pool, pool_scale, lam_re, lam_im,
           ssm_b_re, ssm_b_im, ssm_c_re, ssm_c_im, ssm_d, log_dt, w_glu, b_glu, w_out,
           final_g):
    b, s, d = x.shape
    assert b == 1
    depth = w_in.shape[0]
    d_pool = pool_scale.shape[1]
    d_ssm = b_glu.shape[1]
    d_attn = N_HEADS * HEAD_DIM
    n_f = b_f.shape[1]
    assert n_f == N_HEADS and w_in.shape[2] == 4 * d_attn + n_f + 2 * d_pool + 2 * d_ssm

    w_main = jnp.concatenate([w_in[:, :, :4 * d_attn], w_in[:, :, 4 * d_attn + n_f:]],
                             axis=2).astype(BF16)
    w_f = jnp.pad(w_in[:, :, 4 * d_attn:4 * d_attn + n_f],
                  ((0, 0), (0, 0), (0, LANES - n_f))).astype(BF16)
    b_f_row = jnp.pad(b_f, ((0, 0), (0, LANES - n_f))).reshape(depth, 1, LANES)
    col_up = d_attn
    col_gp = col_up + d_pool
    col_us = col_gp + d_pool
    col_gs = col_us + d_ssm

    mod = _ada_mod(c, w_ada, b_ada).reshape(depth, 3, d)
    t_all, p_all, e_all, al_all = _ssm_prep(lam_re, lam_im, log_dt, ssm_b_re, ssm_b_im,
                                            ssm_c_re, ssm_c_im)
    w_pool_b = w_pool.astype(BF16)
    w_glu_b = w_glu.astype(BF16)
    w_out_b = w_out.astype(BF16)

    xs = x.reshape(s, d)
    for l in range(depth):
        qkv, rest, ft = _inproj(xs, norm_g[l].reshape(1, d), mod[l], w_main[l], w_f[l],
                                b_f_row[l], d_attn=d_attn)
        ya = _attention(qkv, ft, rest, d_attn=d_attn)
        yp = _pool(rest, w_pool_b[l], pool_scale[l].reshape(1, d_pool),
                   col_u=col_up, col_g=col_gp, d_pool=d_pool)
        y_ssm = _ssm(rest, t_all, p_all, e_all, al_all, ssm_d[l].reshape(1, d_ssm),
                     layer=l, col_u=col_us)
        ysm = _glu(y_ssm, rest, w_glu_b[l], b_glu[l].reshape(1, d_ssm), col_g=col_gs)
        xs = _outproj(ya, yp, ysm, w_out_b[l], xs, mod[l, 2:3, :], final_g.reshape(1, d),
                      final=(l == depth - 1))
    return xs.reshape(b, s, d).astype(x.dtype)
```

```python
import functools
import math

import jax
import jax.numpy as jnp
from jax import lax
from jax.experimental import pallas as pl
from jax.experimental.pallas import tpu as pltpu

F32 = jnp.float32
BF16 = jnp.bfloat16

N_HEADS = 8
HEAD_DIM = 128
POOL_WINDOWS = (2, 4, 8, 16)
POOL_GROUP = 128
POOL_HALO = 16
SSM_GROUP = 16
SSM_STATE = 64
SSM_SLAB_GROUPS = 8
NORM_EPS = 1e-6
LANES = 128
VMEM_LIMIT = 56 * 1024 * 1024
PROJ_TN = 512


def _params(sem, vmem=VMEM_LIMIT):
    return pltpu.CompilerParams(dimension_semantics=sem, vmem_limit_bytes=vmem)


def _sigmoid(x):
    return 1.0 / (1.0 + jnp.exp(-x))


def _silu(x):
    return x * _sigmoid(x)


def _pick(n, pref):
    t = min(n, pref)
    while n % t:
        t //= 2
    return t


def _ada_kernel(c_ref, w_ref, b_ref, o_ref):
    ca = _silu(c_ref[...])
    o_ref[0] = jnp.sum(w_ref[0] * ca, axis=0, keepdims=True) + b_ref[0]


def _ada_mod(c, w_ada, b_ada):
    depth, d, n = w_ada.shape
    tn = _pick(n, 2048)
    return pl.pallas_call(
        _ada_kernel,
        grid=(depth, n // tn),
        in_specs=[
            pl.BlockSpec((d, 1), lambda l, j: (0, 0)),
            pl.BlockSpec((1, d, tn), lambda l, j: (l, 0, j)),
            pl.BlockSpec((1, 1, tn), lambda l, j: (l, 0, j)),
        ],
        out_specs=pl.BlockSpec((1, 1, tn), lambda l, j: (l, 0, j)),
        out_shape=jax.ShapeDtypeStruct((depth, 1, n), F32),
        compiler_params=_params(("arbitrary", "arbitrary")),
        name="ada_mod",
    )(c.reshape(d, 1), w_ada, b_ada.reshape(depth, 1, n))


def _inproj_kernel(x_ref, g_ref, mod_ref, wa_ref, wt_ref, wf_ref, bf_ref,
                   qkv_ref, rest_ref, ft_ref, h_ref, carry_ref, *, d_attn, q_scale):
    i = pl.program_id(0)
    tm = x_ref.shape[0]
    n_qkv = qkv_ref.shape[0] * HEAD_DIM
    n_a = wa_ref.shape[2]
    n = n_a + wt_ref.shape[2]
    tn = PROJ_TN

    def put_heads(c0, val):
        for hh in range(tn // HEAD_DIM):
            qkv_ref[c0 // HEAD_DIM + hh] = val[:, hh * HEAD_DIM:(hh + 1) * HEAD_DIM]

    x = x_ref[...]
    ms = jnp.mean(x * x, axis=-1, keepdims=True)
    shift = mod_ref[0:1, :]
    scale = mod_ref[1:2, :]
    h = (x * lax.rsqrt(ms + NORM_EPS) * g_ref[...]) * (1.0 + scale) + shift
    h_ref[...] = h.astype(BF16)

    for c0 in range(0, n, tn):
        w = wa_ref[0, :, c0:c0 + tn] if c0 < n_a else wt_ref[0, :, c0 - n_a:c0 - n_a + tn]
        proj = jnp.dot(h_ref[...], w, preferred_element_type=F32)
        if c0 < d_attn:
            put_heads(c0, (proj * q_scale).astype(BF16))
        elif c0 < n_qkv:
            put_heads(c0, proj.astype(BF16))
        else:
            rest_ref[:, c0 - n_qkv:c0 - n_qkv + tn] = proj

    f = jnp.dot(h_ref[...], wf_ref[...], preferred_element_type=F32) + bf_ref[...]
    logf = -(jnp.maximum(-f, 0.0) + jnp.log1p(jnp.exp(-jnp.abs(f))))
    row = lax.broadcasted_iota(jnp.int32, logf.shape, 0)
    cum = logf
    d = 1
    while d < tm:
        cum = cum + jnp.where(row >= d, pltpu.roll(cum, d, axis=0), 0.0)
        d *= 2

    @pl.when(i == 0)
    def _():
        carry_ref[...] = jnp.zeros_like(carry_ref)

    cum = cum + carry_ref[0:1, :]
    carry_ref[...] = jnp.broadcast_to(cum[tm - 1:tm, :], carry_ref.shape)
    cum_t = cum.T
    for hh in range(N_HEADS):
        ft_ref[hh] = cum_t[hh:hh + 1, :]


def _inproj(x, g, mod, w_all, w_tail, w_f, b_f_row, *, layer, d_attn):
    s, d = x.shape
    n_a = 4 * d_attn
    n = n_a + w_tail.shape[2]
    n_qkv = 3 * d_attn
    tm = _pick(s, 256)
    kern = functools.partial(_inproj_kernel, d_attn=d_attn,
                             q_scale=HEAD_DIM ** -0.5 * math.log2(math.e))
    once = pl.Buffered(1)
    return pl.pallas_call(
        kern,
        grid=(s // tm,),
        in_specs=[
            pl.BlockSpec((tm, d), lambda i: (i, 0)),
            pl.BlockSpec((1, d), lambda i: (0, 0)),
            pl.BlockSpec((3, d), lambda i: (0, 0)),
            pl.BlockSpec((1, d, n_a), lambda i: (layer, 0, 0), pipeline_mode=once),
            pl.BlockSpec((1, d, n - n_a), lambda i: (layer, 0, 0), pipeline_mode=once),
            pl.BlockSpec((d, LANES), lambda i: (0, 0), pipeline_mode=once),
            pl.BlockSpec((1, LANES), lambda i: (0, 0)),
        ],
        out_specs=[
            pl.BlockSpec((n_qkv // HEAD_DIM, tm, HEAD_DIM), lambda i: (0, i, 0)),
            pl.BlockSpec((tm, n - n_qkv), lambda i: (i, 0)),
            pl.BlockSpec((N_HEADS, 1, tm), lambda i: (0, 0, i)),
        ],
        out_shape=[
            jax.ShapeDtypeStruct((n_qkv // HEAD_DIM, s, HEAD_DIM), BF16),
            jax.ShapeDtypeStruct((s, n - n_qkv), F32),
            jax.ShapeDtypeStruct((N_HEADS, 1, s), F32),
        ],
        scratch_shapes=[pltpu.VMEM((tm, d), BF16), pltpu.VMEM((8, LANES), F32)],
        compiler_params=_params(("arbitrary",)),
        name="inproj",
    )(x, g, mod, w_all, w_tail, w_f, b_f_row)


AUG_TERMS = 3
ONES_ROWS = 16
Q_STRIP = 1024
LOG2E = math.log2(math.e)


SAFE_EXP = 60.0
ZERO_EXP = -150.0
NORM_SLACK = 1.02


def _tile_lanes(row, n):
    return jnp.concatenate([row] * (n // LANES), axis=1)


def _attn_kernel(q_ref, k_ref, v_ref, ft_ref, g_ref, o_ref,
                 kaug_ref, vt_ref, base_ref, bend_ref, kall_ref, acc_ref, *, tq, tk):
    i = pl.program_id(1)
    seq = k_ref.shape[1]
    dh = HEAD_DIM
    lane = lax.broadcasted_iota(jnp.int32, (LANES, LANES), 1)
    lane_row = lax.broadcasted_iota(jnp.int32, (1, LANES), 1)
    ones_sq = jnp.ones((LANES, LANES), BF16)

    @pl.when(i == 0)
    def _():
        vt_ref[dh:, :] = jnp.ones((ONES_ROWS, seq), BF16)
        kall_ref[...] = jnp.zeros_like(kall_ref)
        bend_ref[...] = jnp.zeros_like(bend_ref)

        def fill(c, carry):
            base = None
            col = None
            for bb in range(tk // LANES):
                off = pl.multiple_of(c * tk + bb * LANES, LANES)
                row = ft_ref[0, :, pl.ds(off, LANES)]
                col = (-LOG2E) * jnp.broadcast_to(row, (LANES, LANES)).T
                if bb == 0:
                    base = col[0:1, :]
                    base_ref[pl.ds(c, 1), :] = base
                rel = col - base
                hi = rel.astype(BF16).astype(F32)
                mid = (rel - hi).astype(BF16).astype(F32)
                lo = (rel - hi - mid).astype(BF16).astype(F32)
                aug = jnp.where(lane == 0, hi, jnp.where(lane == 1, mid,
                                                         jnp.where(lane == 2, lo, 0.0)))
                kb = k_ref[0, pl.ds(off, LANES), :]
                kaug_ref[pl.ds(off, LANES), 0:dh] = kb
                kaug_ref[pl.ds(off, LANES), dh:] = aug.astype(BF16)
                vt_ref[0:dh, pl.ds(off, LANES)] = (
                    v_ref[0, pl.ds(off, LANES), :].astype(F32).T.astype(BF16))
                kf = kb.astype(F32)
                n2 = jnp.dot((kf * kf).astype(BF16), ones_sq, preferred_element_type=F32)
                kall_ref[0:1, :] = jnp.maximum(kall_ref[0:1, :],
                                               jnp.max(n2, axis=0, keepdims=True))
            bend_ref[0:1, :] = jnp.where(lane_row == c, col[LANES - 1:LANES, :],
                                         bend_ref[0:1, :])
            return carry
        lax.fori_loop(0, seq // tk, fill, 0)

    lane_q = lax.broadcasted_iota(jnp.int32, (tq, LANES), 1)
    q = q_ref[0]
    q_aug = jnp.concatenate(
        [q, jnp.where(lane_q < AUG_TERMS, 1.0, 0.0).astype(BF16)], axis=1)
    acc_ref[...] = jnp.zeros_like(acc_ref)
    base_q = base_ref[pl.ds(i, 1), :]
    n_strip = tq // Q_STRIP

    def chunk(j):
        k_off = pl.multiple_of(j * tk, tk)
        kc = kaug_ref[pl.ds(k_off, tk), :]
        vc = vt_ref[:, pl.ds(k_off, tk)]
        delta = base_ref[pl.ds(j, 1), :] - base_q
        return kc, vc, delta

    def scores(kc, lo, hi):
        return lax.dot_general(kc, q_aug[lo:hi, :], (((1,), (1,)), ((), ())),
                               preferred_element_type=F32)

    def online_step(j, m, masked):
        kc, vc, delta = chunk(j)
        delta = _tile_lanes(delta, Q_STRIP)
        m_out = []
        for st in range(n_strip):
            lo, hi = st * Q_STRIP, (st + 1) * Q_STRIP
            s = scores(kc, lo, hi)
            if masked:
                key = lax.broadcasted_iota(jnp.int32, (tk, Q_STRIP), 0)
                qry = lax.broadcasted_iota(jnp.int32, (tk, Q_STRIP), 1) + lo
                s = jnp.where(key <= qry, s, -jnp.inf)
            m_old = m[st] - delta
            m_new = jnp.maximum(m_old, jnp.max(s, axis=0, keepdims=True))
            p = jnp.exp2(s - m_new).astype(BF16)
            corr = jnp.exp2(m_old - m_new)
            pv = jnp.dot(vc, p, preferred_element_type=F32)
            acc_ref[:, lo:hi] = acc_ref[:, lo:hi] * corr + pv
            m_out.append(m_new + delta)
        return tuple(m_out)

    q_off = pl.multiple_of(i * tq, tq)
    ones_row = jnp.ones((8, dh), BF16)
    row_sum = lambda a: lax.dot_general(ones_row, a.astype(BF16), (((1,), (1,)), ((), ())),
                                        preferred_element_type=F32)[0:1, :]
    qf = q.astype(F32)
    qk_self = row_sum(qf * kaug_ref[pl.ds(q_off, tq), 0:dh].astype(F32))
    rel_q = (-LOG2E) * ft_ref[0, :, pl.ds(q_off, tq)] - _tile_lanes(base_q, tq)
    m_row = qk_self + rel_q
    qk_bound = jnp.sqrt(row_sum(qf * qf) * _tile_lanes(kall_ref[0:1, :], tq)) * NORM_SLACK + 1.0
    slack_diag = jnp.max(qk_bound - qk_self)
    slack = jnp.max(qk_bound - m_row)
    live = jnp.logical_and(slack + (bend_ref[0:1, :] - base_q) >= ZERO_EXP, lane_row < i)
    n_live = jnp.sum(live.astype(jnp.int32))

    def one_pass(j, masked):
        kc, vc, delta = chunk(j)
        ref = m_row - _tile_lanes(delta, tq)
        for st in range(n_strip):
            lo, hi = st * Q_STRIP, (st + 1) * Q_STRIP
            x = scores(kc, lo, hi) - ref[:, lo:hi]
            if masked:
                key = lax.broadcasted_iota(jnp.int32, (tk, Q_STRIP), 0)
                qry = lax.broadcasted_iota(jnp.int32, (tk, Q_STRIP), 1) + lo
                x = jnp.where(key <= qry, x, -jnp.inf)
            acc_ref[:, lo:hi] += jnp.dot(vc, jnp.exp2(x).astype(BF16),
                                         preferred_element_type=F32)

    one_pass(i, True)

    @pl.when(slack_diag <= SAFE_EXP)
    def _():
        def fast_step(jj, carry):
            one_pass(i - 1 - jj, False)
            return carry
        lax.fori_loop(0, n_live, fast_step, 0)

    @pl.when(slack_diag > SAFE_EXP)
    def _():
        acc_ref[...] = jnp.zeros_like(acc_ref)
        m0 = tuple(jnp.full((1, Q_STRIP), -jnp.inf, F32) for _ in range(n_strip))
        m_d = online_step(i, m0, True)
        lax.fori_loop(0, i, lambda jj, m: online_step(i - 1 - jj, m, False), m_d)

    y = (acc_ref[0:dh, :] / acc_ref[dh:dh + 1, :]).T
    o_ref[...] = (y * _silu(g_ref[...])).astype(o_ref.dtype)


def _attention(qkv, ft, rest, *, d_attn):
    s = qkv.shape[1]
    tq = _pick(s, 1024)
    nh = d_attn // HEAD_DIM
    kern = functools.partial(_attn_kernel, tq=tq, tk=tq)
    return pl.pallas_call(
        kern,
        grid=(nh, s // tq),
        in_specs=[
            pl.BlockSpec((1, tq, HEAD_DIM), lambda h, i: (h, i, 0)),
            pl.BlockSpec((1, s, HEAD_DIM), lambda h, i: (nh + h, 0, 0)),
            pl.BlockSpec((1, s, HEAD_DIM), lambda h, i: (2 * nh + h, 0, 0)),
            pl.BlockSpec((1, 1, s), lambda h, i: (h, 0, 0)),
            pl.BlockSpec((tq, HEAD_DIM), lambda h, i: (i, h)),
        ],
        out_specs=pl.BlockSpec((tq, HEAD_DIM), lambda h, i: (i, h)),
        out_shape=jax.ShapeDtypeStruct((s, d_attn), BF16),
        scratch_shapes=[
            pltpu.VMEM((s, 2 * HEAD_DIM), BF16),
            pltpu.VMEM((HEAD_DIM + ONES_ROWS, s), BF16),
            pltpu.VMEM((max(s // tq, 8), LANES), F32),
            pltpu.VMEM((8, LANES), F32),
            pltpu.VMEM((8, LANES), F32),
            pltpu.VMEM((HEAD_DIM + ONES_ROWS, tq), F32),
        ],
        compiler_params=_params(("arbitrary", "arbitrary")),
        name="fox_attention",
    )(qkv, qkv, qkv, ft, rest)


def _pool_kernel(u_ref, prev_ref, g_ref, w_ref, sc_ref, o_ref):
    i = pl.program_id(0)
    tp = u_ref.shape[0]
    u = u_ref[...]
    prev = jnp.where(i > 0, prev_ref[...], 0.0)
    t1 = lax.broadcasted_iota(jnp.int32, (tp, POOL_GROUP), 0) + (i * tp + 1)
    outs = []
    for g, w in enumerate(POOL_WINDOWS):
        lo, hi = g * POOL_GROUP, (g + 1) * POOL_GROUP
        ug = u[:, lo:hi]
        ext = jnp.concatenate([prev[:, lo:hi], ug], axis=0)
        win = ext
        span = 1
        while span < w:
            win = win + pltpu.roll(win, span, axis=0)
            span *= 2
        win = win[POOL_HALO:]
        cnt = jnp.minimum(t1, w).astype(F32)
        pooled = win / cnt - ug
        mixed = jnp.dot(pooled.astype(BF16), w_ref[g], preferred_element_type=F32)
        outs.append(mixed)
    mixed = jnp.concatenate(outs, axis=1) * sc_ref[...]
    o_ref[...] = (mixed * _silu(g_ref[...])).astype(o_ref.dtype)


def _pool(rest, w_pool, pool_scale, *, col_u, col_g, d_pool):
    s = rest.shape[0]
    tp = _pick(s, 1024)
    cu, cg = col_u // d_pool, col_g // d_pool
    halo_blocks = tp // POOL_HALO
    return pl.pallas_call(
        _pool_kernel,
        grid=(s // tp,),
        in_specs=[
            pl.BlockSpec((tp, d_pool), lambda i: (i, cu)),
            pl.BlockSpec((POOL_HALO, d_pool),
                         lambda i: (jnp.maximum(i * halo_blocks - 1, 0), cu)),
            pl.BlockSpec((tp, d_pool), lambda i: (i, cg)),
            pl.BlockSpec(w_pool.shape, lambda i: (0, 0, 0)),
            pl.BlockSpec((1, d_pool), lambda i: (0, 0)),
        ],
        out_specs=pl.BlockSpec((tp, d_pool), lambda i: (i, 0)),
        out_shape=jax.ShapeDtypeStruct((s, d_pool), BF16),
        compiler_params=_params(("arbitrary",)),
        name="pool",
    )(rest, rest, rest, w_pool, pool_scale)


SSM_CHUNK = 8


def _discretise(lr, li, ldt):
    dt = jnp.exp(ldt)
    mag = jnp.exp(lr * dt)
    ab_re = mag * jnp.cos(li * dt)
    ab_im = mag * jnp.sin(li * dt)
    den = lr * lr + li * li
    nr = ab_re - 1.0
    ni = ab_im
    z_re = (nr * lr + ni * li) / den
    z_im = (ni * lr - nr * li) / den
    return ab_re, ab_im, z_re, z_im


def _powers(a_re, a_im, n):
    out = [(jnp.ones_like(a_re), jnp.zeros_like(a_im))]
    for _ in range(n):
        p_re, p_im = out[-1]
        out.append((p_re * a_re - p_im * a_im, p_re * a_im + p_im * a_re))
    return out


def _ssm_prep_kernel(lr_ref, li_ref, ldt_ref, lrc_ref, lic_ref, ldtc_ref,
                     br_ref, bi_ref, cr_ref, ci_ref, t_ref, p_ref, e_ref, al_ref):
    nl = SSM_CHUNK
    cw = br_ref.shape[2]
    a_re, a_im, z_re, z_im = _discretise(lr_ref[0, 0], li_ref[0, 0], ldt_ref[0, 0])
    pw = _powers(a_re, a_im, nl)
    al_ref[0, 0] = jnp.concatenate([pw[nl][0], pw[nl][1]], axis=0)
    br = br_ref[0, 0]
    bi = bi_ref[0, 0]
    bb_re = z_re * br - z_im * bi
    bb_im = z_re * bi + z_im * br
    bb = jnp.concatenate([bb_re, bb_im], axis=1)
    ac_re, ac_im, _, _ = _discretise(lrc_ref[0, 0], lic_ref[0, 0], ldtc_ref[0, 0])
    pwc = _powers(ac_re, ac_im, nl)
    cr = cr_ref[0, 0]
    ci = ci_ref[0, 0]
    ca = [jnp.concatenate([cr * q_re - ci * q_im, -(cr * q_im + ci * q_re)], axis=0)
          for q_re, q_im in pwc]
    kd = [jnp.dot(bb, ca[d], preferred_element_type=F32,
                  precision=lax.Precision.HIGHEST).astype(BF16) for d in range(nl)]
    zero = jnp.zeros((cw, cw), BF16)
    for src in range(nl):
        for dst in range(nl):
            t_ref[0, 0, src * cw:(src + 1) * cw, dst * cw:(dst + 1) * cw] = (
                kd[dst - src] if dst >= src else zero)
        q_re, q_im = pw[nl - 1 - src]
        p_ref[0, 0, src * cw:(src + 1) * cw, :] = jnp.concatenate(
            [bb_re * q_re - bb_im * q_im, bb_re * q_im + bb_im * q_re], axis=1).astype(BF16)
        e_ref[0, 0, :, src * cw:(src + 1) * cw] = ca[src + 1].astype(BF16)


def _ssm_prep(lam_re, lam_im, log_dt, b_re, b_im, c_re, c_im):
    depth, ng, ns = lam_re.shape
    gc = b_re.shape[-1]
    n_slab = ng // SSM_SLAB_GROUPS
    eye = jnp.eye(SSM_SLAB_GROUPS, dtype=F32)
    sw = SSM_SLAB_GROUPS * ns

    def place_b(b):
        b = b.reshape(depth, n_slab, SSM_SLAB_GROUPS, ns, gc).transpose(0, 1, 2, 4, 3)
        return (b[:, :, :, :, None, :] * eye[None, None, :, None, :, None]).reshape(
            depth, n_slab, SSM_SLAB_GROUPS * gc, sw)

    def place_c(c):
        c = c.reshape(depth, n_slab, SSM_SLAB_GROUPS, gc, ns).transpose(0, 1, 2, 4, 3)
        return (c[:, :, :, :, None, :] * eye[None, None, :, None, :, None]).reshape(
            depth, n_slab, sw, SSM_SLAB_GROUPS * gc)

    cw = SSM_SLAB_GROUPS * gc
    nl = SSM_CHUNK
    ldt = jnp.broadcast_to(log_dt[:, :, None], (depth, ng, ns))
    row = lambda v: v.reshape(depth, n_slab, 1, sw)
    col = lambda v: jnp.broadcast_to(v.reshape(depth, n_slab, sw, 1), (depth, n_slab, sw, cw))
    blk = lambda *shape: pl.BlockSpec((1, 1) + shape, lambda l, k: (l, k, 0, 0))
    return pl.pallas_call(
        _ssm_prep_kernel,
        grid=(depth, n_slab),
        in_specs=[blk(1, sw)] * 3 + [blk(sw, cw)] * 3 + [blk(cw, sw)] * 2 + [blk(sw, cw)] * 2,
        out_specs=[blk(nl * cw, nl * cw), blk(nl * cw, 2 * sw), blk(2 * sw, nl * cw),
                   blk(2, sw)],
        out_shape=[
            jax.ShapeDtypeStruct((depth, n_slab, nl * cw, nl * cw), BF16),
            jax.ShapeDtypeStruct((depth, n_slab, nl * cw, 2 * sw), BF16),
            jax.ShapeDtypeStruct((depth, n_slab, 2 * sw, nl * cw), BF16),
            jax.ShapeDtypeStruct((depth, n_slab, 2, sw), F32),
        ],
        compiler_params=_params(("arbitrary", "arbitrary")),
        name="ssm_prep",
    )(row(lam_re), row(lam_im), row(ldt), col(lam_re), col(lam_im), col(ldt),
      place_b(b_re), place_b(b_im), place_c(c_re), place_c(c_im))


def _gelu_tanh(y):
    c = math.sqrt(2.0 / math.pi)
    return 0.5 * y * (1.0 + jnp.tanh(c * (y + 0.044715 * (y * y * y))))


def _ssm_kernel(u_ref, t_ref, p_ref, e_ref, al_ref, d_ref, y_ref, uc_ref, carry_ref):
    i = pl.program_id(1)
    nl = SSM_CHUNK
    ts, cw = u_ref.shape
    rows = ts // nl
    sw = al_ref.shape[3]

    @pl.when(i == 0)
    def _():
        carry_ref[...] = jnp.zeros_like(carry_ref)

    for tau in range(nl):
        uc_ref[:, tau * cw:(tau + 1) * cw] = (
            u_ref[pl.ds(tau, rows, stride=nl), :].astype(BF16))
    uc = uc_ref[...]
    v = jnp.dot(uc, p_ref[0, 0], preferred_element_type=F32)
    xr = v[:, :sw]
    xi = v[:, sw:]
    a_re = al_ref[0, 0, 0:1, :]
    a_im = al_ref[0, 0, 1:2, :]
    c_re = carry_ref[0:1, :]
    c_im = carry_ref[1:2, :]
    row = lax.broadcasted_iota(jnp.int32, (rows, sw), 0)
    first = row == 0
    xr = xr + jnp.where(first, a_re * c_re - a_im * c_im, 0.0)
    xi = xi + jnp.where(first, a_re * c_im + a_im * c_re, 0.0)
    q_re, q_im = a_re, a_im
    d = 1
    while d < rows:
        keep = row >= d
        sr = jnp.where(keep, pltpu.roll(xr, d, axis=0), 0.0)
        si = jnp.where(keep, pltpu.roll(xi, d, axis=0), 0.0)
        xr, xi = xr + (q_re * sr - q_im * si), xi + (q_re * si + q_im * sr)
        q_re, q_im = q_re * q_re - q_im * q_im, 2.0 * (q_re * q_im)
        d *= 2
    pr = jnp.where(first, c_re, pltpu.roll(xr, 1, axis=0))
    pi = jnp.where(first, c_im, pltpu.roll(xi, 1, axis=0))
    carry_ref[0:1, :] = xr[rows - 1:rows, :]
    carry_ref[1:2, :] = xi[rows - 1:rows, :]
    xp = jnp.concatenate([pr, pi], axis=1).astype(BF16)
    yc = (jnp.dot(uc, t_ref[0, 0], preferred_element_type=F32)
          + jnp.dot(xp, e_ref[0, 0], preferred_element_type=F32))
    for tau in range(nl):
        y_ref[pl.ds(tau, rows, stride=nl), :] = (
            yc[:, tau * cw:(tau + 1) * cw]
            + d_ref[...] * u_ref[pl.ds(tau, rows, stride=nl), :])


def _ssm(rest, t_mat, p_mat, e_mat, a_l, d_row, *, layer, col_u):
    s = rest.shape[0]
    n_slab, cw = t_mat.shape[1], d_row.shape[1] // t_mat.shape[1]
    ts = _pick(s, 4096)
    cu = col_u // cw
    sw = a_l.shape[3]
    mat = lambda m: pl.BlockSpec((1, 1) + m.shape[2:], lambda k, i: (layer, k, 0, 0))
    return pl.pallas_call(
        _ssm_kernel,
        grid=(n_slab, s // ts),
        in_specs=[
            pl.BlockSpec((ts, cw), lambda k, i: (i, cu + k)),
            mat(t_mat), mat(p_mat), mat(e_mat), mat(a_l),
            pl.BlockSpec((1, cw), lambda k, i: (0, k)),
        ],
        out_specs=pl.BlockSpec((ts, cw), lambda k, i: (i, k)),
        out_shape=jax.ShapeDtypeStruct((s, n_slab * cw), F32),
        scratch_shapes=[pltpu.VMEM((ts // SSM_CHUNK, SSM_CHUNK * cw), BF16),
                        pltpu.VMEM((2, sw), F32)],
        compiler_params=_params(("arbitrary", "arbitrary")),
        name="ssm",
    )(rest, t_mat, p_mat, e_mat, a_l, d_row)


def _glu_kernel(y_ref, g_ref, wg_ref, bg_ref, o_ref):
    y = _gelu_tanh(y_ref[...])
    z = jnp.dot(y.astype(BF16), wg_ref[...], preferred_element_type=F32) + bg_ref[...]
    o_ref[...] = (y * _sigmoid(z) * _silu(g_ref[...])).astype(o_ref.dtype)


def _glu(y, rest, w_glu, b_glu, *, col_g):
    s, d_ssm = y.shape
    tm = _pick(s, 1024)
    cg = col_g // d_ssm
    return pl.pallas_call(
        _glu_kernel,
        grid=(s // tm,),
        in_specs=[
            pl.BlockSpec((tm, d_ssm), lambda i: (i, 0)),
            pl.BlockSpec((tm, d_ssm), lambda i: (i, cg)),
            pl.BlockSpec(w_glu.shape, lambda i: (0, 0)),
            pl.BlockSpec((1, d_ssm), lambda i: (0, 0)),
        ],
        out_specs=pl.BlockSpec((tm, d_ssm), lambda i: (i, 0)),
        out_shape=jax.ShapeDtypeStruct((s, d_ssm), BF16),
        compiler_params=_params(("arbitrary",)),
        name="ssm_glu",
    )(y, rest, w_glu, b_glu)


def _outproj_kernel(ya_ref, yp_ref, ys_ref, w_ref, x_ref, mod_ref, fg_ref, o_ref, y_ref,
                    *, final):
    da = ya_ref.shape[1]
    dp = yp_ref.shape[1]
    d = x_ref.shape[1]
    tn = PROJ_TN
    y_ref[:, 0:da] = ya_ref[...]
    y_ref[:, da:da + dp] = yp_ref[...]
    y_ref[:, da + dp:] = ys_ref[...]
    for c0 in range(0, d, tn):
        out = jnp.dot(y_ref[...], w_ref[0, :, c0:c0 + tn], preferred_element_type=F32)
        o_ref[:, c0:c0 + tn] = x_ref[:, c0:c0 + tn] + mod_ref[:, c0:c0 + tn] * out
    if final:
        xn = o_ref[...]
        ms = jnp.mean(xn * xn, axis=-1, keepdims=True)
        o_ref[...] = xn * lax.rsqrt(ms + NORM_EPS) * fg_ref[...]


def _outproj(ya, yp, ysm, w_out, x, gate, final_g, *, layer, final):
    s, d = x.shape
    dm = w_out.shape[1]
    tm = _pick(s, 512)
    return pl.pallas_call(
        functools.partial(_outproj_kernel, final=final),
        grid=(s // tm,),
        in_specs=[
            pl.BlockSpec((tm, ya.shape[1]), lambda i: (i, 0)),
            pl.BlockSpec((tm, yp.shape[1]), lambda i: (i, 0)),
            pl.BlockSpec((tm, ysm.shape[1]), lambda i: (i, 0)),
            pl.BlockSpec((1, dm, d), lambda i: (layer, 0, 0), pipeline_mode=pl.Buffered(1)),
            pl.BlockSpec((tm, d), lambda i: (i, 0)),
            pl.BlockSpec((1, d), lambda i: (0, 0)),
            pl.BlockSpec((1, d), lambda i: (0, 0)),
        ],
        out_specs=pl.BlockSpec((tm, d), lambda i: (i, 0)),
        out_shape=jax.ShapeDtypeStruct((s, d), F32),
        scratch_shapes=[pltpu.VMEM((tm, dm), BF16)],
        compiler_params=_params(("arbitrary",)),
        name="outproj",
    )(ya, yp, ysm, w_out, x, gate, final_g)


def kernel(x, c, norm_g, w_ada, b_ada, w_in, b_f, w_pool, pool_scale, lam_re, lam_im,
           ssm_b_re, ssm_b_im, ssm_c_re, ssm_c_im, ssm_d, log_dt, w_glu, b_glu, w_out,
           final_g):
    b, s, d = x.shape
    assert b == 1
    depth = w_in.shape[0]
    d_pool = pool_scale.shape[1]
    d_ssm = b_glu.shape[1]
    d_attn = N_HEADS * HEAD_DIM
    n_f = b_f.shape[1]
    assert n_f == N_HEADS and w_in.shape[2] == 4 * d_attn + n_f + 2 * d_pool + 2 * d_ssm

    w_all = w_in.astype(BF16)
    w_tail = w_all[:, :, 4 * d_attn + n_f:]
    w_f = jnp.pad(w_in[:, :, 4 * d_attn:4 * d_attn + n_f],
                  ((0, 0), (0, 0), (0, LANES - n_f))).astype(BF16)
    b_f_row = jnp.pad(b_f, ((0, 0), (0, LANES - n_f))).reshape(depth, 1, LANES)
    col_up = d_attn
    col_gp = col_up + d_pool
    col_us = col_gp + d_pool
    col_gs = col_us + d_ssm

    mod = _ada_mod(c, w_ada, b_ada).reshape(depth, 3, d)
    t_all, p_all, e_all, al_all = _ssm_prep(lam_re, lam_im, log_dt, ssm_b_re, ssm_b_im,
                                            ssm_c_re, ssm_c_im)
    w_pool_b = w_pool.astype(BF16)
    w_glu_b = w_glu.astype(BF16)
    w_out_b = w_out.astype(BF16)

    xs = x.reshape(s, d)
    for l in range(depth):
        qkv, rest, ft = _inproj(xs, norm_g[l].reshape(1, d), mod[l], w_all, w_tail, w_f[l],
                                b_f_row[l], layer=l, d_attn=d_attn)
        ya = _attention(qkv, ft, rest, d_attn=d_attn)
        yp = _pool(rest, w_pool_b[l], pool_scale[l].reshape(1, d_pool),
                   col_u=col_up, col_g=col_gp, d_pool=d_pool)
        y_ssm = _ssm(rest, t_all, p_all, e_all, al_all, ssm_d[l].reshape(1, d_ssm),
                     layer=l, col_u=col_us)
        ysm = _glu(y_ssm, rest, w_glu_b[l], b_glu[l].reshape(1, d_ssm), col_g=col_gs)
        xs = _outproj(ya, yp, ysm, w_out_b, xs, mod[l, 2:3, :], final_g.reshape(1, d),
                      layer=l, final=(l == depth - 1))
    return xs.reshape(b, s, d).astype(x.dtype)
```

```python
import functools
import math

import jax
import jax.numpy as jnp
from jax import lax
from jax.experimental import pallas as pl
from jax.experimental.pallas import tpu as pltpu

F32 = jnp.float32
BF16 = jnp.bfloat16

N_HEADS = 8
HEAD_DIM = 128
POOL_WINDOWS = (2, 4, 8, 16)
POOL_GROUP = 128
POOL_HALO = 16
SSM_GROUP = 16
SSM_STATE = 64
SSM_SLAB_GROUPS = 8
NORM_EPS = 1e-6
LANES = 128
VMEM_LIMIT = 56 * 1024 * 1024
PROJ_TN = 512


def _params(sem, vmem=VMEM_LIMIT):
    return pltpu.CompilerParams(dimension_semantics=sem, vmem_limit_bytes=vmem)


def _sigmoid(x):
    return 1.0 / (1.0 + jnp.exp(-x))


def _silu(x):
    return x * _sigmoid(x)


def _pick(n, pref):
    t = min(n, pref)
    while n % t:
        t //= 2
    return t


def _ada_kernel(c_ref, w_ref, b_ref, o_ref):
    ca = _silu(c_ref[...])
    o_ref[0] = jnp.sum(w_ref[0] * ca, axis=0, keepdims=True) + b_ref[0]


def _ada_mod(c, w_ada, b_ada):
    depth, d, n = w_ada.shape
    tn = _pick(n, 2048)
    return pl.pallas_call(
        _ada_kernel,
        grid=(depth, n // tn),
        in_specs=[
            pl.BlockSpec((d, 1), lambda l, j: (0, 0)),
            pl.BlockSpec((1, d, tn), lambda l, j: (l, 0, j)),
            pl.BlockSpec((1, 1, tn), lambda l, j: (l, 0, j)),
        ],
        out_specs=pl.BlockSpec((1, 1, tn), lambda l, j: (l, 0, j)),
        out_shape=jax.ShapeDtypeStruct((depth, 1, n), F32),
        compiler_params=_params(("arbitrary", "arbitrary")),
        name="ada_mod",
    )(c.reshape(d, 1), w_ada, b_ada.reshape(depth, 1, n))


def _inproj_kernel(x_ref, g_ref, mod_ref, wa_ref, wt_ref, wf_ref, bf_ref,
                   qkv_ref, rest_ref, ft_ref, h_ref, carry_ref, *, d_attn, q_scale):
    i = pl.program_id(0)
    tm = x_ref.shape[0]
    n_qkv = qkv_ref.shape[0] * HEAD_DIM
    n_a = wa_ref.shape[2]
    n = n_a + wt_ref.shape[2]
    tn = PROJ_TN

    def put_heads(c0, val):
        for hh in range(tn // HEAD_DIM):
            qkv_ref[c0 // HEAD_DIM + hh] = val[:, hh * HEAD_DIM:(hh + 1) * HEAD_DIM]

    x = x_ref[...]
    ms = jnp.mean(x * x, axis=-1, keepdims=True)
    shift = mod_ref[0:1, :]
    scale = mod_ref[1:2, :]
    h = (x * lax.rsqrt(ms + NORM_EPS) * g_ref[...]) * (1.0 + scale) + shift
    h_ref[...] = h.astype(BF16)

    for c0 in range(0, n, tn):
        w = wa_ref[0, :, c0:c0 + tn] if c0 < n_a else wt_ref[0, :, c0 - n_a:c0 - n_a + tn]
        proj = jnp.dot(h_ref[...], w, preferred_element_type=F32)
        if c0 < d_attn:
            put_heads(c0, (proj * q_scale).astype(BF16))
        elif c0 < n_qkv:
            put_heads(c0, proj.astype(BF16))
        else:
            rest_ref[:, c0 - n_qkv:c0 - n_qkv + tn] = proj

    f = jnp.dot(h_ref[...], wf_ref[...], preferred_element_type=F32) + bf_ref[...]
    logf = -(jnp.maximum(-f, 0.0) + jnp.log1p(jnp.exp(-jnp.abs(f))))
    row = lax.broadcasted_iota(jnp.int32, logf.shape, 0)
    cum = logf
    d = 1
    while d < tm:
        cum = cum + jnp.where(row >= d, pltpu.roll(cum, d, axis=0), 0.0)
        d *= 2

    @pl.when(i == 0)
    def _():
        carry_ref[...] = jnp.zeros_like(carry_ref)

    cum = cum + carry_ref[0:1, :]
    carry_ref[...] = jnp.broadcast_to(cum[tm - 1:tm, :], carry_ref.shape)
    cum_t = cum.T
    for hh in range(N_HEADS):
        ft_ref[hh] = cum_t[hh:hh + 1, :]


def _inproj(x, g, mod, w_all, w_tail, w_f, b_f_row, *, layer, d_attn):
    s, d = x.shape
    n_a = 4 * d_attn
    n = n_a + w_tail.shape[2]
    n_qkv = 3 * d_attn
    tm = _pick(s, 256)
    kern = functools.partial(_inproj_kernel, d_attn=d_attn,
                             q_scale=HEAD_DIM ** -0.5 * math.log2(math.e))
    once = pl.Buffered(1)
    return pl.pallas_call(
        kern,
        grid=(s // tm,),
        in_specs=[
            pl.BlockSpec((tm, d), lambda i: (i, 0)),
            pl.BlockSpec((1, d), lambda i: (0, 0)),
            pl.BlockSpec((3, d), lambda i: (0, 0)),
            pl.BlockSpec((1, d, n_a), lambda i: (layer, 0, 0), pipeline_mode=once),
            pl.BlockSpec((1, d, n - n_a), lambda i: (layer, 0, 0), pipeline_mode=once),
            pl.BlockSpec((d, LANES), lambda i: (0, 0), pipeline_mode=once),
            pl.BlockSpec((1, LANES), lambda i: (0, 0)),
        ],
        out_specs=[
            pl.BlockSpec((n_qkv // HEAD_DIM, tm, HEAD_DIM), lambda i: (0, i, 0)),
            pl.BlockSpec((tm, n - n_qkv), lambda i: (i, 0)),
            pl.BlockSpec((N_HEADS, 1, tm), lambda i: (0, 0, i)),
        ],
        out_shape=[
            jax.ShapeDtypeStruct((n_qkv // HEAD_DIM, s, HEAD_DIM), BF16),
            jax.ShapeDtypeStruct((s, n - n_qkv), F32),
            jax.ShapeDtypeStruct((N_HEADS, 1, s), F32),
        ],
        scratch_shapes=[pltpu.VMEM((tm, d), BF16), pltpu.VMEM((8, LANES), F32)],
        compiler_params=_params(("arbitrary",)),
        name="inproj",
    )(x, g, mod, w_all, w_tail, w_f, b_f_row)


AUG_TERMS = 3
ONES_ROWS = 16
Q_STRIP = 1024
LOG2E = math.log2(math.e)


SAFE_EXP = 60.0
ZERO_EXP = -136.0
NORM_SLACK = 1.02


def _tile_lanes(row, n):
    return jnp.concatenate([row] * (n // LANES), axis=1)


def _attn_kernel(q_ref, k_ref, v_ref, ft_ref, g_ref, o_ref,
                 kaug_ref, vt_ref, base_ref, bend_ref, kall_ref, acc_ref, *, tq, tk):
    i = pl.program_id(1)
    seq = k_ref.shape[1]
    dh = HEAD_DIM
    lane = lax.broadcasted_iota(jnp.int32, (LANES, LANES), 1)
    lane_row = lax.broadcasted_iota(jnp.int32, (1, LANES), 1)
    ones_sq = jnp.ones((LANES, LANES), BF16)

    @pl.when(i == 0)
    def _():
        vt_ref[dh:, :] = jnp.ones((ONES_ROWS, seq), BF16)
        kall_ref[...] = jnp.zeros_like(kall_ref)
        bend_ref[...] = jnp.zeros_like(bend_ref)

        def fill(c, carry):
            base = None
            col = None
            for bb in range(tk // LANES):
                off = pl.multiple_of(c * tk + bb * LANES, LANES)
                row = ft_ref[0, :, pl.ds(off, LANES)]
                col = (-LOG2E) * jnp.broadcast_to(row, (LANES, LANES)).T
                if bb == 0:
                    base = col[0:1, :]
                    base_ref[pl.ds(c, 1), :] = base
                rel = col - base
                hi = rel.astype(BF16).astype(F32)
                mid = (rel - hi).astype(BF16).astype(F32)
                lo = (rel - hi - mid).astype(BF16).astype(F32)
                aug = jnp.where(lane == 0, hi, jnp.where(lane == 1, mid,
                                                         jnp.where(lane == 2, lo, 0.0)))
                kb = k_ref[0, pl.ds(off, LANES), :]
                kaug_ref[pl.ds(off, LANES), 0:dh] = kb
                kaug_ref[pl.ds(off, LANES), dh:] = aug.astype(BF16)
                vt_ref[0:dh, pl.ds(off, LANES)] = (
                    v_ref[0, pl.ds(off, LANES), :].astype(F32).T.astype(BF16))
                kf = kb.astype(F32)
                n2 = jnp.dot((kf * kf).astype(BF16), ones_sq, preferred_element_type=F32)
                kall_ref[0:1, :] = jnp.maximum(kall_ref[0:1, :],
                                               jnp.max(n2, axis=0, keepdims=True))
            bend_ref[0:1, :] = jnp.where(lane_row == c, col[LANES - 1:LANES, :],
                                         bend_ref[0:1, :])
            return carry
        lax.fori_loop(0, seq // tk, fill, 0)

    lane_q = lax.broadcasted_iota(jnp.int32, (tq, LANES), 1)
    q = q_ref[0]
    q_aug = jnp.concatenate(
        [q, jnp.where(lane_q < AUG_TERMS, 1.0, 0.0).astype(BF16)], axis=1)
    acc_ref[...] = jnp.zeros_like(acc_ref)
    base_q = base_ref[pl.ds(i, 1), :]
    n_strip = tq // Q_STRIP

    def chunk(j):
        k_off = pl.multiple_of(j * tk, tk)
        kc = kaug_ref[pl.ds(k_off, tk), :]
        vc = vt_ref[:, pl.ds(k_off, tk)]
        delta = base_ref[pl.ds(j, 1), :] - base_q
        return kc, vc, delta

    def scores(kc, lo, hi):
        return lax.dot_general(kc, q_aug[lo:hi, :], (((1,), (1,)), ((), ())),
                               preferred_element_type=F32)

    def online_step(j, m, masked):
        kc, vc, delta = chunk(j)
        delta = _tile_lanes(delta, Q_STRIP)
        m_out = []
        for st in range(n_strip):
            lo, hi = st * Q_STRIP, (st + 1) * Q_STRIP
            s = scores(kc, lo, hi)
            if masked:
                key = lax.broadcasted_iota(jnp.int32, (tk, Q_STRIP), 0)
                qry = lax.broadcasted_iota(jnp.int32, (tk, Q_STRIP), 1) + lo
                s = jnp.where(key <= qry, s, -jnp.inf)
            m_old = m[st] - delta
            m_new = jnp.maximum(m_old, jnp.max(s, axis=0, keepdims=True))
            p = jnp.exp2(s - m_new).astype(BF16)
            corr = jnp.exp2(m_old - m_new)
            pv = jnp.dot(vc, p, preferred_element_type=F32)
            acc_ref[:, lo:hi] = acc_ref[:, lo:hi] * corr + pv
            m_out.append(m_new + delta)
        return tuple(m_out)

    q_off = pl.multiple_of(i * tq, tq)
    ones_row = jnp.ones((8, dh), BF16)
    row_sum = lambda a: lax.dot_general(ones_row, a.astype(BF16), (((1,), (1,)), ((), ())),
                                        preferred_element_type=F32)[0:1, :]
    qf = q.astype(F32)
    qk_self = row_sum(qf * kaug_ref[pl.ds(q_off, tq), 0:dh].astype(F32))
    rel_q = (-LOG2E) * ft_ref[0, :, pl.ds(q_off, tq)] - _tile_lanes(base_q, tq)
    m_row = qk_self + rel_q
    qk_bound = jnp.sqrt(row_sum(qf * qf) * _tile_lanes(kall_ref[0:1, :], tq)) * NORM_SLACK + 1.0
    slack_diag = jnp.max(qk_bound - qk_self)
    slack = jnp.max(qk_bound - m_row)
    live = jnp.logical_and(slack + (bend_ref[0:1, :] - base_q) >= ZERO_EXP, lane_row < i)
    n_live = jnp.sum(live.astype(jnp.int32))

    def one_pass(j):
        kc, vc, delta = chunk(j)
        ref = m_row - _tile_lanes(delta, tq)
        for st in range(n_strip):
            lo, hi = st * Q_STRIP, (st + 1) * Q_STRIP
            x = scores(kc, lo, hi) - ref[:, lo:hi]
            acc_ref[:, lo:hi] += jnp.dot(vc, jnp.exp2(x).astype(BF16),
                                         preferred_element_type=F32)

    def diag_pass():
        kc, vc, _ = chunk(i)
        half = tk // 2
        tri = (lax.broadcasted_iota(jnp.int32, (half, half), 0)
               <= lax.broadcasted_iota(jnp.int32, (half, half), 1))
        x = lax.dot_general(kc[0:half, :], q_aug, (((1,), (1,)), ((), ())),
                            preferred_element_type=F32) - m_row
        x = jnp.concatenate([jnp.where(tri, x[:, 0:half], -jnp.inf), x[:, half:]], axis=1)
        acc_ref[...] += jnp.dot(vc[:, 0:half], jnp.exp2(x).astype(BF16),
                                preferred_element_type=F32)
        x = lax.dot_general(kc[half:, :], q_aug[half:, :], (((1,), (1,)), ((), ())),
                            preferred_element_type=F32) - m_row[:, half:]
        acc_ref[:, half:] += jnp.dot(vc[:, half:], jnp.exp2(jnp.where(tri, x, -jnp.inf)).astype(BF16),
                                     preferred_element_type=F32)

    diag_pass()

    @pl.when(slack_diag <= SAFE_EXP)
    def _():
        def fast_pair(jj, carry):
            one_pass(i - 1 - 2 * jj)
            one_pass(i - 2 - 2 * jj)
            return carry
        lax.fori_loop(0, n_live // 2, fast_pair, 0)

        @pl.when(n_live % 2 == 1)
        def _():
            one_pass(i - n_live)

    @pl.when(slack_diag > SAFE_EXP)
    def _():
        acc_ref[...] = jnp.zeros_like(acc_ref)
        m0 = tuple(jnp.full((1, Q_STRIP), -jnp.inf, F32) for _ in range(n_strip))
        m_d = online_step(i, m0, True)
        lax.fori_loop(0, i, lambda jj, m: online_step(i - 1 - jj, m, False), m_d)

    y = (acc_ref[0:dh, :] / acc_ref[dh:dh + 1, :]).T
    o_ref[...] = (y * _silu(g_ref[...])).astype(o_ref.dtype)


def _attention(qkv, ft, rest, *, d_attn):
    s = qkv.shape[1]
    tq = _pick(s, 1024)
    nh = d_attn // HEAD_DIM
    kern = functools.partial(_attn_kernel, tq=tq, tk=tq)
    return pl.pallas_call(
        kern,
        grid=(nh, s // tq),
        in_specs=[
            pl.BlockSpec((1, tq, HEAD_DIM), lambda h, i: (h, i, 0)),
            pl.BlockSpec((1, s, HEAD_DIM), lambda h, i: (nh + h, 0, 0)),
            pl.BlockSpec((1, s, HEAD_DIM), lambda h, i: (2 * nh + h, 0, 0)),
            pl.BlockSpec((1, 1, s), lambda h, i: (h, 0, 0)),
            pl.BlockSpec((tq, HEAD_DIM), lambda h, i: (i, h)),
        ],
        out_specs=pl.BlockSpec((tq, HEAD_DIM), lambda h, i: (i, h)),
        out_shape=jax.ShapeDtypeStruct((s, d_attn), BF16),
        scratch_shapes=[
            pltpu.VMEM((s, 2 * HEAD_DIM), BF16),
            pltpu.VMEM((HEAD_DIM + ONES_ROWS, s), BF16),
            pltpu.VMEM((max(s // tq, 8), LANES), F32),
            pltpu.VMEM((8, LANES), F32),
            pltpu.VMEM((8, LANES), F32),
            pltpu.VMEM((HEAD_DIM + ONES_ROWS, tq), F32),
        ],
        compiler_params=_params(("arbitrary", "arbitrary")),
        name="fox_attention",
    )(qkv, qkv, qkv, ft, rest)


def _pool_branch(i, u, prev, gate, w_ref, scale):
    tp = u.shape[0]
    prev = jnp.where(i > 0, prev, 0.0)
    t1 = lax.broadcasted_iota(jnp.int32, (tp, POOL_GROUP), 0) + (i * tp + 1)
    outs = []
    for g, w in enumerate(POOL_WINDOWS):
        lo, hi = g * POOL_GROUP, (g + 1) * POOL_GROUP
        ug = u[:, lo:hi]
        ext = jnp.concatenate([prev[:, lo:hi], ug], axis=0)
        win = ext
        span = 1
        while span < w:
            win = win + pltpu.roll(win, span, axis=0)
            span *= 2
        win = win[POOL_HALO:]
        cnt = jnp.minimum(t1, w).astype(F32)
        pooled = win / cnt - ug
        mixed = jnp.dot(pooled.astype(BF16), w_ref[g], preferred_element_type=F32)
        outs.append(mixed)
    mixed = jnp.concatenate(outs, axis=1) * scale
    return (mixed * _silu(gate)).astype(BF16)


SSM_CHUNK = 8


def _discretise(lr, li, ldt):
    dt = jnp.exp(ldt)
    mag = jnp.exp(lr * dt)
    ab_re = mag * jnp.cos(li * dt)
    ab_im = mag * jnp.sin(li * dt)
    den = lr * lr + li * li
    nr = ab_re - 1.0
    ni = ab_im
    z_re = (nr * lr + ni * li) / den
    z_im = (ni * lr - nr * li) / den
    return ab_re, ab_im, z_re, z_im


def _powers(a_re, a_im, n):
    out = [(jnp.ones_like(a_re), jnp.zeros_like(a_im))]
    for _ in range(n):
        p_re, p_im = out[-1]
        out.append((p_re * a_re - p_im * a_im, p_re * a_im + p_im * a_re))
    return out


def _ssm_prep_kernel(lr_ref, li_ref, ldt_ref, lrc_ref, lic_ref, ldtc_ref,
                     br_ref, bi_ref, cr_ref, ci_ref, t_ref, p_ref, e_ref, al_ref):
    nl = SSM_CHUNK
    cw = br_ref.shape[2]
    a_re, a_im, z_re, z_im = _discretise(lr_ref[0, 0], li_ref[0, 0], ldt_ref[0, 0])
    pw = _powers(a_re, a_im, nl)
    al_ref[0, 0] = jnp.concatenate([pw[nl][0], pw[nl][1]], axis=0)
    br = br_ref[0, 0]
    bi = bi_ref[0, 0]
    bb_re = z_re * br - z_im * bi
    bb_im = z_re * bi + z_im * br
    bb = jnp.concatenate([bb_re, bb_im], axis=1)
    ac_re, ac_im, _, _ = _discretise(lrc_ref[0, 0], lic_ref[0, 0], ldtc_ref[0, 0])
    pwc = _powers(ac_re, ac_im, nl)
    cr = cr_ref[0, 0]
    ci = ci_ref[0, 0]
    ca = [jnp.concatenate([cr * q_re - ci * q_im, -(cr * q_im + ci * q_re)], axis=0)
          for q_re, q_im in pwc]
    kd_all = jnp.dot(bb, jnp.concatenate(ca[:nl], axis=1), preferred_element_type=F32,
                     precision=lax.Precision.HIGHEST).astype(BF16)
    kd = [kd_all[:, d * cw:(d + 1) * cw] for d in range(nl)]
    zero = jnp.zeros((cw, cw), BF16)
    for src in range(nl):
        for dst in range(nl):
            t_ref[0, 0, src * cw:(src + 1) * cw, dst * cw:(dst + 1) * cw] = (
                kd[dst - src] if dst >= src else zero)
        q_re, q_im = pw[nl - 1 - src]
        p_ref[0, 0, src * cw:(src + 1) * cw, :] = jnp.concatenate(
            [bb_re * q_re - bb_im * q_im, bb_re * q_im + bb_im * q_re], axis=1).astype(BF16)
        e_ref[0, 0, :, src * cw:(src + 1) * cw] = ca[src + 1].astype(BF16)


def _ssm_prep(lam_re, lam_im, log_dt, b_re, b_im, c_re, c_im):
    depth, ng, ns = lam_re.shape
    gc = b_re.shape[-1]
    n_slab = ng // SSM_SLAB_GROUPS
    eye = jnp.eye(SSM_SLAB_GROUPS, dtype=F32)
    sw = SSM_SLAB_GROUPS * ns

    def place_b(b):
        b = b.reshape(depth, n_slab, SSM_SLAB_GROUPS, ns, gc).transpose(0, 1, 2, 4, 3)
        return (b[:, :, :, :, None, :] * eye[None, None, :, None, :, None]).reshape(
            depth, n_slab, SSM_SLAB_GROUPS * gc, sw)

    def place_c(c):
        c = c.reshape(depth, n_slab, SSM_SLAB_GROUPS, gc, ns).transpose(0, 1, 2, 4, 3)
        return (c[:, :, :, :, None, :] * eye[None, None, :, None, :, None]).reshape(
            depth, n_slab, sw, SSM_SLAB_GROUPS * gc)

    cw = SSM_SLAB_GROUPS * gc
    nl = SSM_CHUNK
    ldt = jnp.broadcast_to(log_dt[:, :, None], (depth, ng, ns))
    row = lambda v: v.reshape(depth, n_slab, 1, sw)
    col = lambda v: jnp.broadcast_to(v.reshape(depth, n_slab, sw, 1), (depth, n_slab, sw, cw))
    blk = lambda *shape: pl.BlockSpec((1, 1) + shape, lambda l, k: (l, k, 0, 0))
    return pl.pallas_call(
        _ssm_prep_kernel,
        grid=(depth, n_slab),
        in_specs=[blk(1, sw)] * 3 + [blk(sw, cw)] * 3 + [blk(cw, sw)] * 2 + [blk(sw, cw)] * 2,
        out_specs=[blk(nl * cw, nl * cw), blk(nl * cw, 2 * sw), blk(2 * sw, nl * cw),
                   blk(2, sw)],
        out_shape=[
            jax.ShapeDtypeStruct((depth, n_slab, nl * cw, nl * cw), BF16),
            jax.ShapeDtypeStruct((depth, n_slab, nl * cw, 2 * sw), BF16),
            jax.ShapeDtypeStruct((depth, n_slab, 2 * sw, nl * cw), BF16),
            jax.ShapeDtypeStruct((depth, n_slab, 2, sw), F32),
        ],
        compiler_params=_params(("arbitrary", "arbitrary")),
        name="ssm_prep",
    )(row(lam_re), row(lam_im), row(ldt), col(lam_re), col(lam_im), col(ldt),
      place_b(b_re), place_b(b_im), place_c(c_re), place_c(c_im))


def _gelu_tanh(y):
    c = math.sqrt(2.0 / math.pi)
    return 0.5 * y * (1.0 + jnp.tanh(c * (y + 0.044715 * (y * y * y))))


def _ssm_kernel(u_ref, t_ref, p_ref, e_ref, al_ref, d_ref, y_ref, uc_ref, carry_ref):
    i = pl.program_id(1)
    nl = SSM_CHUNK
    ts, cw = u_ref.shape
    rows = ts // nl
    sw = al_ref.shape[3]

    @pl.when(i == 0)
    def _():
        carry_ref[...] = jnp.zeros_like(carry_ref)

    for tau in range(nl):
        uc_ref[:, tau * cw:(tau + 1) * cw] = (
            u_ref[pl.ds(tau, rows, stride=nl), :].astype(BF16))
    uc = uc_ref[...]
    v = jnp.dot(uc, p_ref[0, 0], preferred_element_type=F32)
    xr = v[:, :sw]
    xi = v[:, sw:]
    a_re = al_ref[0, 0, 0:1, :]
    a_im = al_ref[0, 0, 1:2, :]
    c_re = carry_ref[0:1, :]
    c_im = carry_ref[1:2, :]
    row = lax.broadcasted_iota(jnp.int32, (rows, sw), 0)
    first = row == 0
    xr = xr + jnp.where(first, a_re * c_re - a_im * c_im, 0.0)
    xi = xi + jnp.where(first, a_re * c_im + a_im * c_re, 0.0)
    q_re, q_im = a_re, a_im
    d = 1
    while d < rows:
        keep = row >= d
        sr = jnp.where(keep, pltpu.roll(xr, d, axis=0), 0.0)
        si = jnp.where(keep, pltpu.roll(xi, d, axis=0), 0.0)
        xr, xi = xr + (q_re * sr - q_im * si), xi + (q_re * si + q_im * sr)
        q_re, q_im = q_re * q_re - q_im * q_im, 2.0 * (q_re * q_im)
        d *= 2
    pr = jnp.where(first, c_re, pltpu.roll(xr, 1, axis=0))
    pi = jnp.where(first, c_im, pltpu.roll(xi, 1, axis=0))
    carry_ref[0:1, :] = xr[rows - 1:rows, :]
    carry_ref[1:2, :] = xi[rows - 1:rows, :]
    xp = jnp.concatenate([pr, pi], axis=1).astype(BF16)
    yc = (jnp.dot(uc, t_ref[0, 0], preferred_element_type=F32)
          + jnp.dot(xp, e_ref[0, 0], preferred_element_type=F32))
    for tau in range(nl):
        y_ref[pl.ds(tau, rows, stride=nl), :] = (
            yc[:, tau * cw:(tau + 1) * cw]
            + d_ref[...] * u_ref[pl.ds(tau, rows, stride=nl), :])


def _ssm(rest, t_mat, p_mat, e_mat, a_l, d_row, *, layer, col_u):
    s = rest.shape[0]
    n_slab, cw = t_mat.shape[1], d_row.shape[1] // t_mat.shape[1]
    ts = _pick(s, 4096)
    cu = col_u // cw
    sw = a_l.shape[3]
    mat = lambda m: pl.BlockSpec((1, 1) + m.shape[2:], lambda k, i: (layer, k, 0, 0))
    return pl.pallas_call(
        _ssm_kernel,
        grid=(n_slab, s // ts),
        in_specs=[
            pl.BlockSpec((ts, cw), lambda k, i: (i, cu + k)),
            mat(t_mat), mat(p_mat), mat(e_mat), mat(a_l),
            pl.BlockSpec((1, cw), lambda k, i: (0, k)),
        ],
        out_specs=pl.BlockSpec((ts, cw), lambda k, i: (i, k)),
        out_shape=jax.ShapeDtypeStruct((s, n_slab * cw), F32),
        scratch_shapes=[pltpu.VMEM((ts // SSM_CHUNK, SSM_CHUNK * cw), BF16),
                        pltpu.VMEM((2, sw), F32)],
        compiler_params=_params(("arbitrary", "arbitrary")),
        name="ssm",
    )(rest, t_mat, p_mat, e_mat, a_l, d_row)


def _glu_branch(y_raw, gate, wg_ref, bg):
    y = _gelu_tanh(y_raw)
    z = jnp.dot(y.astype(BF16), wg_ref[...], preferred_element_type=F32) + bg
    return (y * _sigmoid(z) * _silu(gate)).astype(BF16)


def _outproj_kernel(ya_ref, up_ref, prev_ref, gp_ref, ysr_ref, gs_ref, wp_ref, sc_ref,
                    wg_ref, bg_ref, w_ref, x_ref, mod_ref, fg_ref, o_ref, *, final):
    i = pl.program_id(0)
    da = ya_ref.shape[1]
    dp = up_ref.shape[1]
    d = x_ref.shape[1]
    tn = PROJ_TN
    ya = ya_ref[...]
    parts = [jnp.dot(ya, w_ref[0, 0:da, c0:c0 + tn], preferred_element_type=F32)
             for c0 in range(0, d, tn)]
    yp = _pool_branch(i, up_ref[...], prev_ref[...], gp_ref[...], wp_ref, sc_ref[...])
    ys = _glu_branch(ysr_ref[...], gs_ref[...], wg_ref, bg_ref[...])
    for n, c0 in enumerate(range(0, d, tn)):
        out = (parts[n]
               + jnp.dot(yp, w_ref[0, da:da + dp, c0:c0 + tn], preferred_element_type=F32)
               + jnp.dot(ys, w_ref[0, da + dp:, c0:c0 + tn], preferred_element_type=F32))
        o_ref[:, c0:c0 + tn] = x_ref[:, c0:c0 + tn] + mod_ref[:, c0:c0 + tn] * out
    if final:
        xn = o_ref[...]
        ms = jnp.mean(xn * xn, axis=-1, keepdims=True)
        o_ref[...] = xn * lax.rsqrt(ms + NORM_EPS) * fg_ref[...]


def _outproj(ya, rest, y_ssm, w_pool, pool_scale, w_glu, b_glu, w_out, x, gate, final_g,
             *, layer, final, col_up, col_gp, col_gs):
    s, d = x.shape
    dm = w_out.shape[1]
    d_pool = pool_scale.shape[1]
    d_ssm = b_glu.shape[1]
    tm = _pick(s, 512)
    halo_blocks = tm // POOL_HALO
    cu, cg, cs = col_up // d_pool, col_gp // d_pool, col_gs // d_ssm
    const = lambda a: pl.BlockSpec(a.shape, lambda i: (0,) * a.ndim)
    return pl.pallas_call(
        functools.partial(_outproj_kernel, final=final),
        grid=(s // tm,),
        in_specs=[
            pl.BlockSpec((tm, ya.shape[1]), lambda i: (i, 0)),
            pl.BlockSpec((tm, d_pool), lambda i: (i, cu)),
            pl.BlockSpec((POOL_HALO, d_pool),
                         lambda i: (jnp.maximum(i * halo_blocks - 1, 0), cu)),
            pl.BlockSpec((tm, d_pool), lambda i: (i, cg)),
            pl.BlockSpec((tm, d_ssm), lambda i: (i, 0)),
            pl.BlockSpec((tm, d_ssm), lambda i: (i, cs)),
            const(w_pool), const(pool_scale), const(w_glu), const(b_glu),
            pl.BlockSpec((1, dm, d), lambda i: (layer, 0, 0), pipeline_mode=pl.Buffered(1)),
            pl.BlockSpec((tm, d), lambda i: (i, 0)),
            pl.BlockSpec((1, d), lambda i: (0, 0)),
            pl.BlockSpec((1, d), lambda i: (0, 0)),
        ],
        out_specs=pl.BlockSpec((tm, d), lambda i: (i, 0)),
        out_shape=jax.ShapeDtypeStruct((s, d), F32),
        compiler_params=_params(("arbitrary",)),
        name="outproj",
    )(ya, rest, rest, rest, y_ssm, rest, w_pool, pool_scale, w_glu, b_glu, w_out, x, gate,
      final_g)


def kernel(x, c, norm_g, w_ada, b_ada, w_in, b_f, w_pool, pool_scale, lam_re, lam_im,
           ssm_b_re, ssm_b_im, ssm_c_re, ssm_c_im, ssm_d, log_dt, w_glu, b_glu, w_out,
           final_g):
    b, s, d = x.shape
    assert b == 1
    depth = w_in.shape[0]
    d_pool = pool_scale.shape[1]
    d_ssm = b_glu.shape[1]
    d_attn = N_HEADS * HEAD_DIM
    n_f = b_f.shape[1]
    assert n_f == N_HEADS and w_in.shape[2] == 4 * d_attn + n_f + 2 * d_pool + 2 * d_ssm

    w_all = w_in.astype(BF16)
    w_tail = w_all[:, :, 4 * d_attn + n_f:]
    w_f = jnp.pad(w_in[:, :, 4 * d_attn:4 * d_attn + n_f],
                  ((0, 0), (0, 0), (0, LANES - n_f))).astype(BF16)
    b_f_row = jnp.pad(b_f, ((0, 0), (0, LANES - n_f))).reshape(depth, 1, LANES)
    col_up = d_attn
    col_gp = col_up + d_pool
    col_us = col_gp + d_pool
    col_gs = col_us + d_ssm

    mod = _ada_mod(c, w_ada, b_ada).reshape(depth, 3, d)
    t_all, p_all, e_all, al_all = _ssm_prep(lam_re, lam_im, log_dt, ssm_b_re, ssm_b_im,
                                            ssm_c_re, ssm_c_im)
    w_pool_b = w_pool.astype(BF16)
    w_glu_b = w_glu.astype(BF16)
    w_out_b = w_out.astype(BF16)

    xs = x.reshape(s, d)
    for l in range(depth):
        qkv, rest, ft = _inproj(xs, norm_g[l].reshape(1, d), mod[l], w_all, w_tail, w_f[l],
                                b_f_row[l], layer=l, d_attn=d_attn)
        ya = _attention(qkv, ft, rest, d_attn=d_attn)
        y_ssm = _ssm(rest, t_all, p_all, e_all, al_all, ssm_d[l].reshape(1, d_ssm),
                     layer=l, col_u=col_us)
        xs = _outproj(ya, rest, y_ssm, w_pool_b[l], pool_scale[l].reshape(1, d_pool),
                      w_glu_b[l], b_glu[l].reshape(1, d_ssm), w_out_b, xs, mod[l, 2:3, :],
                      final_g.reshape(1, d), layer=l, final=(l == depth - 1),
                      col_up=col_up, col_gp=col_gp, col_gs=col_gs)
    return xs.reshape(b, s, d).astype(x.dtype)
```

```python
import functools
import math

import jax
import jax.numpy as jnp
from jax import lax
from jax.experimental import pallas as pl
from jax.experimental.pallas import tpu as pltpu

F32 = jnp.float32
BF16 = jnp.bfloat16

N_HEADS = 8
HEAD_DIM = 128
POOL_WINDOWS = (2, 4, 8, 16)
POOL_GROUP = 128
POOL_HALO = 16
SSM_GROUP = 16
SSM_STATE = 64
SSM_SLAB_GROUPS = 8
NORM_EPS = 1e-6
LANES = 128
VMEM_LIMIT = 56 * 1024 * 1024
PROJ_TN = 512


def _params(sem, vmem=VMEM_LIMIT):
    return pltpu.CompilerParams(dimension_semantics=sem, vmem_limit_bytes=vmem)


def _sigmoid(x):
    return 1.0 / (1.0 + jnp.exp(-x))


def _silu(x):
    return x * _sigmoid(x)


def _pick(n, pref):
    t = min(n, pref)
    while n % t:
        t //= 2
    return t


def _ada_kernel(c_ref, w_ref, b_ref, o_ref):
    ca = _silu(c_ref[...])
    o_ref[0] = jnp.sum(w_ref[0] * ca, axis=0, keepdims=True) + b_ref[0]


def _ada_mod(c, w_ada, b_ada):
    depth, d, n = w_ada.shape
    tn = _pick(n, 2048)
    return pl.pallas_call(
        _ada_kernel,
        grid=(depth, n // tn),
        in_specs=[
            pl.BlockSpec((d, 1), lambda l, j: (0, 0)),
            pl.BlockSpec((1, d, tn), lambda l, j: (l, 0, j)),
            pl.BlockSpec((1, 1, tn), lambda l, j: (l, 0, j)),
        ],
        out_specs=pl.BlockSpec((1, 1, tn), lambda l, j: (l, 0, j)),
        out_shape=jax.ShapeDtypeStruct((depth, 1, n), F32),
        compiler_params=_params(("arbitrary", "arbitrary")),
        name="ada_mod",
    )(c.reshape(d, 1), w_ada, b_ada.reshape(depth, 1, n))


def _inproj_kernel(x_ref, g_ref, mod_ref, wa_ref, wt_ref, wf_ref, bf_ref,
                   qkv_ref, rest_ref, ft_ref, h_ref, carry_ref, *, d_attn, q_scale):
    i = pl.program_id(0)
    tm = x_ref.shape[0]
    n_qkv = qkv_ref.shape[0] * HEAD_DIM
    n_a = wa_ref.shape[2]
    n = n_a + wt_ref.shape[2]
    tn = PROJ_TN

    def put_heads(c0, val):
        for hh in range(tn // HEAD_DIM):
            qkv_ref[c0 // HEAD_DIM + hh] = val[:, hh * HEAD_DIM:(hh + 1) * HEAD_DIM]

    x = x_ref[...]
    ms = jnp.mean(x * x, axis=-1, keepdims=True)
    shift = mod_ref[0:1, :]
    scale = mod_ref[1:2, :]
    h = (x * lax.rsqrt(ms + NORM_EPS) * g_ref[...]) * (1.0 + scale) + shift
    h_ref[...] = h.astype(BF16)

    for c0 in range(0, n, tn):
        w = wa_ref[0, :, c0:c0 + tn] if c0 < n_a else wt_ref[0, :, c0 - n_a:c0 - n_a + tn]
        proj = jnp.dot(h_ref[...], w, preferred_element_type=F32)
        if c0 < d_attn:
            put_heads(c0, (proj * q_scale).astype(BF16))
        elif c0 < n_qkv:
            put_heads(c0, proj.astype(BF16))
        else:
            rest_ref[:, c0 - n_qkv:c0 - n_qkv + tn] = proj

    f = jnp.dot(h_ref[...], wf_ref[...], preferred_element_type=F32) + bf_ref[...]
    logf = -(jnp.maximum(-f, 0.0) + jnp.log1p(jnp.exp(-jnp.abs(f))))
    row = lax.broadcasted_iota(jnp.int32, logf.shape, 0)
    cum = logf
    d = 1
    while d < tm:
        cum = cum + jnp.where(row >= d, pltpu.roll(cum, d, axis=0), 0.0)
        d *= 2

    @pl.when(i == 0)
    def _():
        carry_ref[...] = jnp.zeros_like(carry_ref)

    cum = cum + carry_ref[0:1, :]
    carry_ref[...] = jnp.broadcast_to(cum[tm - 1:tm, :], carry_ref.shape)
    cum_t = cum.T
    for hh in range(N_HEADS):
        ft_ref[hh] = cum_t[hh:hh + 1, :]


def _inproj(x, g, mod, w_all, w_tail, w_f, b_f_row, *, layer, d_attn):
    s, d = x.shape
    n_a = 4 * d_attn
    n = n_a + w_tail.shape[2]
    n_qkv = 3 * d_attn
    tm = _pick(s, 256)
    kern = functools.partial(_inproj_kernel, d_attn=d_attn,
                             q_scale=HEAD_DIM ** -0.5 * math.log2(math.e))
    once = pl.Buffered(1)
    return pl.pallas_call(
        kern,
        grid=(s // tm,),
        in_specs=[
            pl.BlockSpec((tm, d), lambda i: (i, 0)),
            pl.BlockSpec((1, d), lambda i: (0, 0)),
            pl.BlockSpec((3, d), lambda i: (0, 0)),
            pl.BlockSpec((1, d, n_a), lambda i: (layer, 0, 0), pipeline_mode=once),
            pl.BlockSpec((1, d, n - n_a), lambda i: (layer, 0, 0), pipeline_mode=once),
            pl.BlockSpec((d, LANES), lambda i: (0, 0), pipeline_mode=once),
            pl.BlockSpec((1, LANES), lambda i: (0, 0)),
        ],
        out_specs=[
            pl.BlockSpec((n_qkv // HEAD_DIM, tm, HEAD_DIM), lambda i: (0, i, 0)),
            pl.BlockSpec((tm, n - n_qkv), lambda i: (i, 0)),
            pl.BlockSpec((N_HEADS, 1, tm), lambda i: (0, 0, i)),
        ],
        out_shape=[
            jax.ShapeDtypeStruct((n_qkv // HEAD_DIM, s, HEAD_DIM), BF16),
            jax.ShapeDtypeStruct((s, n - n_qkv), F32),
            jax.ShapeDtypeStruct((N_HEADS, 1, s), F32),
        ],
        scratch_shapes=[pltpu.VMEM((tm, d), BF16), pltpu.VMEM((8, LANES), F32)],
        compiler_params=_params(("arbitrary",)),
        name="inproj",
    )(x, g, mod, w_all, w_tail, w_f, b_f_row)


AUG_TERMS = 3
ONES_ROWS = 16
Q_STRIP = 1024
LOG2E = math.log2(math.e)


SAFE_EXP = 60.0
ZERO_EXP = -136.0
NORM_SLACK = 1.02


def _tile_lanes(row, n):
    return jnp.concatenate([row] * (n // LANES), axis=1)


def _attn_kernel(q_ref, k_ref, v_ref, ft_ref, g_ref, o_ref,
                 kaug_ref, vt_ref, base_ref, bend_ref, kall_ref, acc_ref, *, tq, tk):
    i = pl.program_id(1)
    seq = k_ref.shape[1]
    dh = HEAD_DIM
    sub = lax.broadcasted_iota(jnp.int32, (LANES, LANES), 0)
    lane_row = lax.broadcasted_iota(jnp.int32, (1, LANES), 1)
    ones_sq = jnp.ones((LANES, LANES), BF16)

    @pl.when(i == 0)
    def _():
        vt_ref[dh:, :] = jnp.ones((ONES_ROWS, seq), BF16)
        kall_ref[...] = jnp.zeros_like(kall_ref)
        bend_ref[...] = jnp.zeros_like(bend_ref)

        def fill(c, carry):
            c_off = pl.multiple_of(c * tk, tk)
            b_row = (-LOG2E) * ft_ref[0, :, pl.ds(c_off, tk)]
            base = jnp.broadcast_to(b_row[:, 0:1], (1, LANES))
            base_ref[pl.ds(c, 1), :] = base
            bend_ref[0:1, :] = jnp.where(
                lane_row == c, jnp.broadcast_to(b_row[:, tk - 1:tk], (1, LANES)),
                bend_ref[0:1, :])
            rel = b_row - _tile_lanes(base, tk)
            hi = rel.astype(BF16).astype(F32)
            mid = (rel - hi).astype(BF16).astype(F32)
            lo = (rel - hi - mid).astype(BF16).astype(F32)
            for bb in range(tk // LANES):
                off = pl.multiple_of(c_off + bb * LANES, LANES)
                cs = slice(bb * LANES, (bb + 1) * LANES)
                terms = jnp.where(sub == 0, hi[:, cs], jnp.where(
                    sub == 1, mid[:, cs], jnp.where(sub == 2, lo[:, cs], 0.0)))
                kb = k_ref[0, pl.ds(off, LANES), :]
                kaug_ref[pl.ds(off, LANES), 0:dh] = kb
                kaug_ref[pl.ds(off, LANES), dh:] = terms.T.astype(BF16)
                vt_ref[0:dh, pl.ds(off, LANES)] = (
                    v_ref[0, pl.ds(off, LANES), :].astype(F32).T.astype(BF16))
                kf = kb.astype(F32)
                n2 = jnp.dot((kf * kf).astype(BF16), ones_sq, preferred_element_type=F32)
                kall_ref[0:1, :] = jnp.maximum(kall_ref[0:1, :],
                                               jnp.max(n2, axis=0, keepdims=True))
            return carry
        lax.fori_loop(0, seq // tk, fill, 0)

    lane_q = lax.broadcasted_iota(jnp.int32, (tq, LANES), 1)
    q = q_ref[0]
    q_aug = jnp.concatenate(
        [q, jnp.where(lane_q < AUG_TERMS, 1.0, 0.0).astype(BF16)], axis=1)
    acc_ref[...] = jnp.zeros_like(acc_ref)
    base_q = base_ref[pl.ds(i, 1), :]
    n_strip = tq // Q_STRIP

    def chunk(j):
        k_off = pl.multiple_of(j * tk, tk)
        kc = kaug_ref[pl.ds(k_off, tk), :]
        vc = vt_ref[:, pl.ds(k_off, tk)]
        delta = base_ref[pl.ds(j, 1), :] - base_q
        return kc, vc, delta

    def scores(kc, lo, hi):
        return lax.dot_general(kc, q_aug[lo:hi, :], (((1,), (1,)), ((), ())),
                               preferred_element_type=F32)

    def online_step(j, m, masked):
        kc, vc, delta = chunk(j)
        delta = _tile_lanes(delta, Q_STRIP)
        m_out = []
        for st in range(n_strip):
            lo, hi = st * Q_STRIP, (st + 1) * Q_STRIP
            s = scores(kc, lo, hi)
            if masked:
                key = lax.broadcasted_iota(jnp.int32, (tk, Q_STRIP), 0)
                qry = lax.broadcasted_iota(jnp.int32, (tk, Q_STRIP), 1) + lo
                s = jnp.where(key <= qry, s, -jnp.inf)
            m_old = m[st] - delta
            m_new = jnp.maximum(m_old, jnp.max(s, axis=0, keepdims=True))
            p = jnp.exp2(s - m_new).astype(BF16)
            corr = jnp.exp2(m_old - m_new)
            pv = jnp.dot(vc, p, preferred_element_type=F32)
            acc_ref[:, lo:hi] = acc_ref[:, lo:hi] * corr + pv
            m_out.append(m_new + delta)
        return tuple(m_out)

    q_off = pl.multiple_of(i * tq, tq)
    ones_row = jnp.ones((8, dh), BF16)
    row_sum = lambda a: lax.dot_general(ones_row, a.astype(BF16), (((1,), (1,)), ((), ())),
                                        preferred_element_type=F32)[0:1, :]
    qf = q.astype(F32)
    qk_self = row_sum(qf * kaug_ref[pl.ds(q_off, tq), 0:dh].astype(F32))
    rel_q = (-LOG2E) * ft_ref[0, :, pl.ds(q_off, tq)] - _tile_lanes(base_q, tq)
    m_row = qk_self + rel_q
    qk_bound = jnp.sqrt(row_sum(qf * qf) * _tile_lanes(kall_ref[0:1, :], tq)) * NORM_SLACK + 1.0
    slack_diag = jnp.max(qk_bound - qk_self)
    slack = jnp.max(qk_bound - m_row)
    live = jnp.logical_and(slack + (bend_ref[0:1, :] - base_q) >= ZERO_EXP, lane_row < i)
    n_live = jnp.sum(live.astype(jnp.int32))

    def one_pass(j):
        kc, vc, delta = chunk(j)
        ref = m_row - _tile_lanes(delta, tq)
        for st in range(n_strip):
            lo, hi = st * Q_STRIP, (st + 1) * Q_STRIP
            x = scores(kc, lo, hi) - ref[:, lo:hi]
            acc_ref[:, lo:hi] += jnp.dot(vc, jnp.exp2(x).astype(BF16),
                                         preferred_element_type=F32)

    def diag_pass():
        kc, vc, _ = chunk(i)
        half = tk // 2
        tri = (lax.broadcasted_iota(jnp.int32, (half, half), 0)
               <= lax.broadcasted_iota(jnp.int32, (half, half), 1))
        x = lax.dot_general(kc[0:half, :], q_aug, (((1,), (1,)), ((), ())),
                            preferred_element_type=F32) - m_row
        x = jnp.concatenate([jnp.where(tri, x[:, 0:half], -jnp.inf), x[:, half:]], axis=1)
        acc_ref[...] += jnp.dot(vc[:, 0:half], jnp.exp2(x).astype(BF16),
                                preferred_element_type=F32)
        x = lax.dot_general(kc[half:, :], q_aug[half:, :], (((1,), (1,)), ((), ())),
                            preferred_element_type=F32) - m_row[:, half:]
        acc_ref[:, half:] += jnp.dot(vc[:, half:], jnp.exp2(jnp.where(tri, x, -jnp.inf)).astype(BF16),
                                     preferred_element_type=F32)

    diag_pass()

    @pl.when(slack_diag <= SAFE_EXP)
    def _():
        def fast_pair(jj, carry):
            one_pass(i - 1 - 2 * jj)
            one_pass(i - 2 - 2 * jj)
            return carry
        lax.fori_loop(0, n_live // 2, fast_pair, 0)

        @pl.when(n_live % 2 == 1)
        def _():
            one_pass(i - n_live)

    @pl.when(slack_diag > SAFE_EXP)
    def _():
        acc_ref[...] = jnp.zeros_like(acc_ref)
        m0 = tuple(jnp.full((1, Q_STRIP), -jnp.inf, F32) for _ in range(n_strip))
        m_d = online_step(i, m0, True)
        lax.fori_loop(0, i, lambda jj, m: online_step(i - 1 - jj, m, False), m_d)

    y = (acc_ref[0:dh, :] / acc_ref[dh:dh + 1, :]).T
    o_ref[...] = (y * _silu(g_ref[...])).astype(o_ref.dtype)


def _attention(qkv, ft, rest, *, d_attn):
    s = qkv.shape[1]
    tq = _pick(s, 1024)
    nh = d_attn // HEAD_DIM
    kern = functools.partial(_attn_kernel, tq=tq, tk=tq)
    return pl.pallas_call(
        kern,
        grid=(nh, s // tq),
        in_specs=[
            pl.BlockSpec((1, tq, HEAD_DIM), lambda h, i: (h, i, 0)),
            pl.BlockSpec((1, s, HEAD_DIM), lambda h, i: (nh + h, 0, 0)),
            pl.BlockSpec((1, s, HEAD_DIM), lambda h, i: (2 * nh + h, 0, 0)),
            pl.BlockSpec((1, 1, s), lambda h, i: (h, 0, 0)),
            pl.BlockSpec((tq, HEAD_DIM), lambda h, i: (i, h)),
        ],
        out_specs=pl.BlockSpec((tq, HEAD_DIM), lambda h, i: (i, h)),
        out_shape=jax.ShapeDtypeStruct((s, d_attn), BF16),
        scratch_shapes=[
            pltpu.VMEM((s, 2 * HEAD_DIM), BF16),
            pltpu.VMEM((HEAD_DIM + ONES_ROWS, s), BF16),
            pltpu.VMEM((max(s // tq, 8), LANES), F32),
            pltpu.VMEM((8, LANES), F32),
            pltpu.VMEM((8, LANES), F32),
            pltpu.VMEM((HEAD_DIM + ONES_ROWS, tq), F32),
        ],
        compiler_params=_params(("arbitrary", "arbitrary")),
        name="fox_attention",
    )(qkv, qkv, qkv, ft, rest)


def _pool_branch(i, u, prev, gate, w_ref, scale):
    tp = u.shape[0]
    prev = jnp.where(i > 0, prev, 0.0)
    t1 = lax.broadcasted_iota(jnp.int32, (tp, POOL_GROUP), 0) + (i * tp + 1)
    outs = []
    for g, w in enumerate(POOL_WINDOWS):
        lo, hi = g * POOL_GROUP, (g + 1) * POOL_GROUP
        ug = u[:, lo:hi]
        ext = jnp.concatenate([prev[:, lo:hi], ug], axis=0)
        win = ext
        span = 1
        while span < w:
            win = win + pltpu.roll(win, span, axis=0)
            span *= 2
        win = win[POOL_HALO:]
        cnt = jnp.minimum(t1, w).astype(F32)
        pooled = win / cnt - ug
        mixed = jnp.dot(pooled.astype(BF16), w_ref[g], preferred_element_type=F32)
        outs.append(mixed)
    mixed = jnp.concatenate(outs, axis=1) * scale
    return (mixed * _silu(gate)).astype(BF16)


SSM_CHUNK = 8


def _discretise(lr, li, ldt):
    dt = jnp.exp(ldt)
    mag = jnp.exp(lr * dt)
    ab_re = mag * jnp.cos(li * dt)
    ab_im = mag * jnp.sin(li * dt)
    den = lr * lr + li * li
    nr = ab_re - 1.0
    ni = ab_im
    z_re = (nr * lr + ni * li) / den
    z_im = (ni * lr - nr * li) / den
    return ab_re, ab_im, z_re, z_im


def _powers(a_re, a_im, n):
    out = [(jnp.ones_like(a_re), jnp.zeros_like(a_im))]
    for _ in range(n):
        p_re, p_im = out[-1]
        out.append((p_re * a_re - p_im * a_im, p_re * a_im + p_im * a_re))
    return out


def _ssm_prep_kernel(lr_ref, li_ref, ldt_ref, lrc_ref, lic_ref, ldtc_ref,
                     br_ref, bi_ref, cr_ref, ci_ref, t_ref, p_ref, e_ref, al_ref):
    nl = SSM_CHUNK
    cw = br_ref.shape[2]
    a_re, a_im, z_re, z_im = _discretise(lr_ref[0, 0], li_ref[0, 0], ldt_ref[0, 0])
    pw = _powers(a_re, a_im, nl)
    al_ref[0, 0] = jnp.concatenate([pw[nl][0], pw[nl][1]], axis=0)
    br = br_ref[0, 0]
    bi = bi_ref[0, 0]
    bb_re = z_re * br - z_im * bi
    bb_im = z_re * bi + z_im * br
    bb = jnp.concatenate([bb_re, bb_im], axis=1)
    ac_re, ac_im, _, _ = _discretise(lrc_ref[0, 0], lic_ref[0, 0], ldtc_ref[0, 0])
    pwc = _powers(ac_re, ac_im, nl)
    cr = cr_ref[0, 0]
    ci = ci_ref[0, 0]
    ca = [jnp.concatenate([cr * q_re - ci * q_im, -(cr * q_im + ci * q_re)], axis=0)
          for q_re, q_im in pwc]
    kd_all = jnp.dot(bb, jnp.concatenate(ca[:nl], axis=1), preferred_element_type=F32,
                     precision=lax.Precision.HIGHEST).astype(BF16)
    kd = [kd_all[:, d * cw:(d + 1) * cw] for d in range(nl)]
    zero = jnp.zeros((cw, cw), BF16)
    for src in range(nl):
        for dst in range(nl):
            t_ref[0, 0, src * cw:(src + 1) * cw, dst * cw:(dst + 1) * cw] = (
                kd[dst - src] if dst >= src else zero)
        q_re, q_im = pw[nl - 1 - src]
        p_ref[0, 0, src * cw:(src + 1) * cw, :] = jnp.concatenate(
            [bb_re * q_re - bb_im * q_im, bb_re * q_im + bb_im * q_re], axis=1).astype(BF16)
        e_ref[0, 0, :, src * cw:(src + 1) * cw] = ca[src + 1].astype(BF16)


def _ssm_prep(lam_re, lam_im, log_dt, b_re, b_im, c_re, c_im):
    depth, ng, ns = lam_re.shape
    gc = b_re.shape[-1]
    n_slab = ng // SSM_SLAB_GROUPS
    eye = jnp.eye(SSM_SLAB_GROUPS, dtype=F32)
    sw = SSM_SLAB_GROUPS * ns

    def place_b(b):
        b = b.reshape(depth, n_slab, SSM_SLAB_GROUPS, ns, gc).transpose(0, 1, 2, 4, 3)
        return (b[:, :, :, :, None, :] * eye[None, None, :, None, :, None]).reshape(
            depth, n_slab, SSM_SLAB_GROUPS * gc, sw)

    def place_c(c):
        c = c.reshape(depth, n_slab, SSM_SLAB_GROUPS, gc, ns).transpose(0, 1, 2, 4, 3)
        return (c[:, :, :, :, None, :] * eye[None, None, :, None, :, None]).reshape(
            depth, n_slab, sw, SSM_SLAB_GROUPS * gc)

    cw = SSM_SLAB_GROUPS * gc
    nl = SSM_CHUNK
    ldt = jnp.broadcast_to(log_dt[:, :, None], (depth, ng, ns))
    row = lambda v: v.reshape(depth, n_slab, 1, sw)
    col = lambda v: jnp.broadcast_to(v.reshape(depth, n_slab, sw, 1), (depth, n_slab, sw, cw))
    blk = lambda *shape: pl.BlockSpec((1, 1) + shape, lambda l, k: (l, k, 0, 0))
    return pl.pallas_call(
        _ssm_prep_kernel,
        grid=(depth, n_slab),
        in_specs=[blk(1, sw)] * 3 + [blk(sw, cw)] * 3 + [blk(cw, sw)] * 2 + [blk(sw, cw)] * 2,
        out_specs=[blk(nl * cw, nl * cw), blk(nl * cw, 2 * sw), blk(2 * sw, nl * cw),
                   blk(2, sw)],
        out_shape=[
            jax.ShapeDtypeStruct((depth, n_slab, nl * cw, nl * cw), BF16),
            jax.ShapeDtypeStruct((depth, n_slab, nl * cw, 2 * sw), BF16),
            jax.ShapeDtypeStruct((depth, n_slab, 2 * sw, nl * cw), BF16),
            jax.ShapeDtypeStruct((depth, n_slab, 2, sw), F32),
        ],
        compiler_params=_params(("arbitrary", "arbitrary")),
        name="ssm_prep",
    )(row(lam_re), row(lam_im), row(ldt), col(lam_re), col(lam_im), col(ldt),
      place_b(b_re), place_b(b_im), place_c(c_re), place_c(c_im))


def _gelu_tanh(y):
    c = math.sqrt(2.0 / math.pi)
    return 0.5 * y * (1.0 + jnp.tanh(c * (y + 0.044715 * (y * y * y))))


SSM_SLABS_PER_STEP = 2


def _ssm_kernel(*refs):
    per = SSM_SLABS_PER_STEP
    u_refs = refs[:per]
    t_ref, p_ref, e_ref, al_ref, d_ref, y_ref, uc_ref, carry_ref = refs[per:]
    i = pl.program_id(1)
    nl = SSM_CHUNK
    ts, cw = u_refs[0].shape
    rows = ts // nl
    sw = al_ref.shape[3]

    @pl.when(i == 0)
    def _():
        carry_ref[...] = jnp.zeros_like(carry_ref)

    row = lax.broadcasted_iota(jnp.int32, (rows, sw), 0)
    first = row == 0
    vs, yts = [], []
    for k in range(per):
        for tau in range(nl):
            uc_ref[k, :, tau * cw:(tau + 1) * cw] = (
                u_refs[k][pl.ds(tau, rows, stride=nl), :].astype(BF16))
        vs.append(jnp.dot(uc_ref[k], p_ref[0, k], preferred_element_type=F32))
    for k in range(per):
        yts.append(jnp.dot(uc_ref[k], t_ref[0, k], preferred_element_type=F32))
    for k in range(per):
        u_ref = u_refs[k]
        xr = vs[k][:, :sw]
        xi = vs[k][:, sw:]
        a_re = al_ref[0, k, 0:1, :]
        a_im = al_ref[0, k, 1:2, :]
        c_re = carry_ref[k, 0:1, :]
        c_im = carry_ref[k, 1:2, :]
        xr = xr + jnp.where(first, a_re * c_re - a_im * c_im, 0.0)
        xi = xi + jnp.where(first, a_re * c_im + a_im * c_re, 0.0)
        q_re, q_im = a_re, a_im
        d = 1
        while d < rows:
            if d % 8:
                keep = row >= d
                sr = jnp.where(keep, pltpu.roll(xr, d, axis=0), 0.0)
                si = jnp.where(keep, pltpu.roll(xi, d, axis=0), 0.0)
                xr, xi = xr + (q_re * sr - q_im * si), xi + (q_re * si + q_im * sr)
            else:
                sr, si = xr[:rows - d], xi[:rows - d]
                xr, xi = (
                    jnp.concatenate([xr[:d], xr[d:] + (q_re * sr - q_im * si)], axis=0),
                    jnp.concatenate([xi[:d], xi[d:] + (q_re * si + q_im * sr)], axis=0))
            q_re, q_im = q_re * q_re - q_im * q_im, 2.0 * (q_re * q_im)
            d *= 2
        pr = jnp.where(first, c_re, pltpu.roll(xr, 1, axis=0))
        pi = jnp.where(first, c_im, pltpu.roll(xi, 1, axis=0))
        carry_ref[k, 0:1, :] = xr[rows - 1:rows, :]
        carry_ref[k, 1:2, :] = xi[rows - 1:rows, :]
        xp = jnp.concatenate([pr, pi], axis=1).astype(BF16)
        yc = yts[k] + jnp.dot(xp, e_ref[0, k], preferred_element_type=F32)
        for tau in range(nl):
            y_ref[k, pl.ds(tau, rows, stride=nl), :] = (
                yc[:, tau * cw:(tau + 1) * cw]
                + d_ref[:, k * cw:(k + 1) * cw] * u_ref[pl.ds(tau, rows, stride=nl), :])


def _ssm(rest, t_mat, p_mat, e_mat, a_l, d_row, *, layer, col_u):
    s = rest.shape[0]
    n_slab, cw = t_mat.shape[1], d_row.shape[1] // t_mat.shape[1]
    per = SSM_SLABS_PER_STEP
    ts = _pick(s, 4096)
    cu = col_u // cw
    sw = a_l.shape[3]
    mat = lambda m: pl.BlockSpec((1, per) + m.shape[2:], lambda k, i: (layer, k, 0, 0))
    u_spec = lambda n: pl.BlockSpec((ts, cw), lambda k, i: (i, cu + per * k + n))
    return pl.pallas_call(
        _ssm_kernel,
        grid=(n_slab // per, s // ts),
        in_specs=[u_spec(n) for n in range(per)] + [
            mat(t_mat), mat(p_mat), mat(e_mat), mat(a_l),
            pl.BlockSpec((1, per * cw), lambda k, i: (0, k)),
        ],
        out_specs=pl.BlockSpec((per, ts, cw), lambda k, i: (k, i, 0)),
        out_shape=jax.ShapeDtypeStruct((n_slab, s, cw), F32),
        scratch_shapes=[pltpu.VMEM((per, ts // SSM_CHUNK, SSM_CHUNK * cw), BF16),
                        pltpu.VMEM((per, 2, sw), F32)],
        compiler_params=_params(("arbitrary", "arbitrary")),
        name="ssm",
    )(*([rest] * per), t_mat, p_mat, e_mat, a_l, d_row)


def _glu_branch(y_raw, gate, wg_ref, bg):
    y = _gelu_tanh(y_raw)
    z = jnp.dot(y.astype(BF16), wg_ref[...], preferred_element_type=F32) + bg
    return (y * _sigmoid(z) * _silu(gate)).astype(BF16)


def _outproj_kernel(ya_ref, up_ref, prev_ref, gp_ref, ysr_ref, gs_ref, wp_ref, sc_ref,
                    wg_ref, bg_ref, w_ref, x_ref, mod_ref, fg_ref, o_ref, *, final):
    i = pl.program_id(0)
    da = ya_ref.shape[1]
    dp = up_ref.shape[1]
    d = x_ref.shape[1]
    tn = PROJ_TN
    ya = ya_ref[...]
    parts = [jnp.dot(ya, w_ref[0, 0:da, c0:c0 + tn], preferred_element_type=F32)
             for c0 in range(0, d, tn)]
    yp = _pool_branch(i, up_ref[...], prev_ref[...], gp_ref[...], wp_ref, sc_ref[...])
    y_raw = jnp.concatenate([ysr_ref[k] for k in range(ysr_ref.shape[0])], axis=1)
    ys = _glu_branch(y_raw, gs_ref[...], wg_ref, bg_ref[...])
    for n, c0 in enumerate(range(0, d, tn)):
        out = (parts[n]
               + jnp.dot(yp, w_ref[0, da:da + dp, c0:c0 + tn], preferred_element_type=F32)
               + jnp.dot(ys, w_ref[0, da + dp:, c0:c0 + tn], preferred_element_type=F32))
        o_ref[:, c0:c0 + tn] = x_ref[:, c0:c0 + tn] + mod_ref[:, c0:c0 + tn] * out
    if final:
        xn = o_ref[...]
        ms = jnp.mean(xn * xn, axis=-1, keepdims=True)
        o_ref[...] = xn * lax.rsqrt(ms + NORM_EPS) * fg_ref[...]


def _outproj(ya, rest, y_ssm, w_pool, pool_scale, w_glu, b_glu, w_out, x, gate, final_g,
             *, layer, final, col_up, col_gp, col_gs):
    s, d = x.shape
    dm = w_out.shape[1]
    d_pool = pool_scale.shape[1]
    d_ssm = b_glu.shape[1]
    tm = _pick(s, 512)
    halo_blocks = tm // POOL_HALO
    cu, cg, cs = col_up // d_pool, col_gp // d_pool, col_gs // d_ssm
    const = lambda a: pl.BlockSpec(a.shape, lambda i: (0,) * a.ndim)
    return pl.pallas_call(
        functools.partial(_outproj_kernel, final=final),
        grid=(s // tm,),
        in_specs=[
            pl.BlockSpec((tm, ya.shape[1]), lambda i: (i, 0)),
            pl.BlockSpec((tm, d_pool), lambda i: (i, cu)),
            pl.BlockSpec((POOL_HALO, d_pool),
                         lambda i: (jnp.maximum(i * halo_blocks - 1, 0), cu)),
            pl.BlockSpec((tm, d_pool), lambda i: (i, cg)),
            pl.BlockSpec((y_ssm.shape[0], tm, y_ssm.shape[2]), lambda i: (0, i, 0)),
            pl.BlockSpec((tm, d_ssm), lambda i: (i, cs)),
            const(w_pool), const(pool_scale), const(w_glu), const(b_glu),
            pl.BlockSpec((1, dm, d), lambda i: (layer, 0, 0), pipeline_mode=pl.Buffered(1)),
            pl.BlockSpec((tm, d), lambda i: (i, 0)),
            pl.BlockSpec((1, d), lambda i: (0, 0)),
            pl.BlockSpec((1, d), lambda i: (0, 0)),
        ],
        out_specs=pl.BlockSpec((tm, d), lambda i: (i, 0)),
        out_shape=jax.ShapeDtypeStruct((s, d), F32),
        compiler_params=_params(("arbitrary",)),
        name="outproj",
    )(ya, rest, rest, rest, y_ssm, rest, w_pool, pool_scale, w_glu, b_glu, w_out, x, gate,
      final_g)


def kernel(x, c, norm_g, w_ada, b_ada, w_in, b_f, w_pool, pool_scale, lam_re, lam_im,
           ssm_b_re, ssm_b_im, ssm_c_re, ssm_c_im, ssm_d, log_dt, w_glu, b_glu, w_out,
           final_g):
    b, s, d = x.shape
    assert b == 1
    depth = w_in.shape[0]
    d_pool = pool_scale.shape[1]
    d_ssm = b_glu.shape[1]
    d_attn = N_HEADS * HEAD_DIM
    n_f = b_f.shape[1]
    assert n_f == N_HEADS and w_in.shape[2] == 4 * d_attn + n_f + 2 * d_pool + 2 * d_ssm

    w_all = w_in[:, :, :4 * d_attn].astype(BF16)
    w_tail = w_in[:, :, 4 * d_attn + n_f:].astype(BF16)
    w_f = jnp.pad(w_in[:, :, 4 * d_attn:4 * d_attn + n_f],
                  ((0, 0), (0, 0), (0, LANES - n_f))).astype(BF16)
    b_f_row = jnp.pad(b_f, ((0, 0), (0, LANES - n_f))).reshape(depth, 1, LANES)
    col_up = d_attn
    col_gp = col_up + d_pool
    col_us = col_gp + d_pool
    col_gs = col_us + d_ssm

    mod = _ada_mod(c, w_ada, b_ada).reshape(depth, 3, d)
    t_all, p_all, e_all, al_all = _ssm_prep(lam_re, lam_im, log_dt, ssm_b_re, ssm_b_im,
                                            ssm_c_re, ssm_c_im)
    w_pool_b = w_pool.astype(BF16)
    w_glu_b = w_glu.astype(BF16)
    w_out_b = w_out.astype(BF16)

    xs = x.reshape(s, d)
    for l in range(depth):
        qkv, rest, ft = _inproj(xs, norm_g[l].reshape(1, d), mod[l], w_all, w_tail, w_f[l],
                                b_f_row[l], layer=l, d_attn=d_attn)
        ya = _attention(qkv, ft, rest, d_attn=d_attn)
        y_ssm = _ssm(rest, t_all, p_all, e_all, al_all, ssm_d[l].reshape(1, d_ssm),
                     layer=l, col_u=col_us)
        xs = _outproj(ya, rest, y_ssm, w_pool_b[l], pool_scale[l].reshape(1, d_pool),
                      w_glu_b[l], b_glu[l].reshape(1, d_ssm), w_out_b, xs, mod[l, 2:3, :],
                      final_g.reshape(1, d), layer=l, final=(l == depth - 1),
                      col_up=col_up, col_gp=col_gp, col_gs=col_gs)
    return xs.reshape(b, s, d).astype(x.dtype)
```

```python
import functools
import math

import jax
import jax.numpy as jnp
from jax import lax
from jax.experimental import pallas as pl
from jax.experimental.pallas import tpu as pltpu

F32 = jnp.float32
BF16 = jnp.bfloat16

N_HEADS = 8
HEAD_DIM = 128
POOL_WINDOWS = (2, 4, 8, 16)
POOL_GROUP = 128
POOL_HALO = 16
SSM_GROUP = 16
SSM_STATE = 64
SSM_SLAB_GROUPS = 8
NORM_EPS = 1e-6
LANES = 128
VMEM_LIMIT = 56 * 1024 * 1024
PROJ_TN = 512


def _params(sem, vmem=VMEM_LIMIT):
    return pltpu.CompilerParams(dimension_semantics=sem, vmem_limit_bytes=vmem)


def _sigmoid(x):
    return 1.0 / (1.0 + jnp.exp(-x))


def _silu(x):
    return x * _sigmoid(x)


def _pick(n, pref):
    t = min(n, pref)
    while n % t:
        t //= 2
    return t


def _ada_kernel(c_ref, w_ref, b_ref, o_ref):
    ca = _silu(c_ref[...])
    o_ref[0] = jnp.sum(w_ref[0] * ca, axis=0, keepdims=True) + b_ref[0]


def _ada_mod(c, w_ada, b_ada):
    depth, d, n = w_ada.shape
    tn = _pick(n, 2048)
    return pl.pallas_call(
        _ada_kernel,
        grid=(depth, n // tn),
        in_specs=[
            pl.BlockSpec((d, 1), lambda l, j: (0, 0)),
            pl.BlockSpec((1, d, tn), lambda l, j: (l, 0, j)),
            pl.BlockSpec((1, 1, tn), lambda l, j: (l, 0, j)),
        ],
        out_specs=pl.BlockSpec((1, 1, tn), lambda l, j: (l, 0, j)),
        out_shape=jax.ShapeDtypeStruct((depth, 1, n), F32),
        compiler_params=_params(("arbitrary", "arbitrary")),
        name="ada_mod",
    )(c.reshape(d, 1), w_ada, b_ada.reshape(depth, 1, n))


def _wprep_kernel(w_ref, head_ref, f_ref, tail_ref, prev_ref, *, n_head_tiles, n_f):
    j = pl.program_id(1)
    cur = w_ref[0]
    tn = cur.shape[1]

    @pl.when(j < n_head_tiles)
    def _():
        head_ref[0] = cur.astype(BF16)

    @pl.when(j == n_head_tiles)
    def _():
        lane = lax.broadcasted_iota(jnp.int32, (cur.shape[0], LANES), 1)
        f_ref[0] = jnp.where(lane < n_f, cur[:, 0:LANES], 0.0).astype(BF16)

    @pl.when(j > n_head_tiles)
    def _():
        lane = lax.broadcasted_iota(jnp.int32, cur.shape, 1)
        tail_ref[0] = jnp.where(lane < tn - n_f, pltpu.roll(prev_ref[...], tn - n_f, axis=1),
                                pltpu.roll(cur, tn - n_f, axis=1)).astype(BF16)

    @pl.when(j >= n_head_tiles)
    def _():
        prev_ref[...] = cur


def _wprep(w_in, *, n_head, n_f):
    depth, d, n_all = w_in.shape
    tn = PROJ_TN
    n_tail = n_all - n_head - n_f
    nh, nt = n_head // tn, n_tail // tn
    assert nh * tn == n_head and nt * tn == n_tail and n_f < LANES
    kern = functools.partial(_wprep_kernel, n_head_tiles=nh, n_f=n_f)
    return pl.pallas_call(
        kern,
        grid=(depth, nh + nt + 1),
        in_specs=[pl.BlockSpec((1, d, tn), lambda l, j: (l, 0, j))],
        out_specs=[
            pl.BlockSpec((1, d, tn), lambda l, j: (l, 0, jnp.minimum(j, nh - 1))),
            pl.BlockSpec((1, d, LANES), lambda l, j: (l, 0, 0)),
            pl.BlockSpec((1, d, tn), lambda l, j: (l, 0, jnp.clip(j - nh - 1, 0, nt - 1))),
        ],
        out_shape=[
            jax.ShapeDtypeStruct((depth, d, n_head), BF16),
            jax.ShapeDtypeStruct((depth, d, LANES), BF16),
            jax.ShapeDtypeStruct((depth, d, n_tail), BF16),
        ],
        scratch_shapes=[pltpu.VMEM((d, tn), F32)],
        compiler_params=_params(("arbitrary", "arbitrary")),
        name="w_prep",
    )(w_in)


def _inproj_kernel(x_ref, g_ref, mod_ref, wa_ref, wt_ref, wf_ref, bf_ref,
                   qkv_ref, rest_ref, ft_ref, h_ref, carry_ref, *, d_attn, q_scale):
    i = pl.program_id(0)
    tm = x_ref.shape[0]
    n_qkv = qkv_ref.shape[0] * HEAD_DIM
    n_a = wa_ref.shape[2]
    n = n_a + wt_ref.shape[2]
    tn = PROJ_TN

    def put_heads(c0, val):
        for hh in range(tn // HEAD_DIM):
            qkv_ref[c0 // HEAD_DIM + hh] = val[:, hh * HEAD_DIM:(hh + 1) * HEAD_DIM]

    x = x_ref[...]
    ms = jnp.mean(x * x, axis=-1, keepdims=True)
    shift = mod_ref[0:1, :]
    scale = mod_ref[1:2, :]
    h = (x * lax.rsqrt(ms + NORM_EPS) * g_ref[...]) * (1.0 + scale) + shift
    h_ref[...] = h.astype(BF16)

    for c0 in range(0, n, tn):
        w = wa_ref[0, :, c0:c0 + tn] if c0 < n_a else wt_ref[0, :, c0 - n_a:c0 - n_a + tn]
        proj = jnp.dot(h_ref[...], w, preferred_element_type=F32)
        if c0 < d_attn:
            put_heads(c0, (proj * q_scale).astype(BF16))
        elif c0 < n_qkv:
            put_heads(c0, proj.astype(BF16))
        else:
            rest_ref[:, c0 - n_qkv:c0 - n_qkv + tn] = proj

    f = jnp.dot(h_ref[...], wf_ref[...], preferred_element_type=F32) + bf_ref[...]
    logf = -(jnp.maximum(-f, 0.0) + jnp.log1p(jnp.exp(-jnp.abs(f))))
    row = lax.broadcasted_iota(jnp.int32, logf.shape, 0)
    cum = logf
    d = 1
    while d < tm:
        cum = cum + jnp.where(row >= d, pltpu.roll(cum, d, axis=0), 0.0)
        d *= 2

    @pl.when(i == 0)
    def _():
        carry_ref[...] = jnp.zeros_like(carry_ref)

    cum = cum + carry_ref[0:1, :]
    carry_ref[...] = jnp.broadcast_to(cum[tm - 1:tm, :], carry_ref.shape)
    cum_t = cum.T
    for hh in range(N_HEADS):
        ft_ref[hh] = cum_t[hh:hh + 1, :]


def _inproj(x, g, mod, w_all, w_tail, w_f, b_f_row, *, layer, d_attn):
    s, d = x.shape
    n_a = 4 * d_attn
    n = n_a + w_tail.shape[2]
    n_qkv = 3 * d_attn
    tm = _pick(s, 256)
    kern = functools.partial(_inproj_kernel, d_attn=d_attn,
                             q_scale=HEAD_DIM ** -0.5 * math.log2(math.e))
    once = pl.Buffered(1)
    return pl.pallas_call(
        kern,
        grid=(s // tm,),
        in_specs=[
            pl.BlockSpec((tm, d), lambda i: (i, 0)),
            pl.BlockSpec((1, d), lambda i: (0, 0)),
            pl.BlockSpec((3, d), lambda i: (0, 0)),
            pl.BlockSpec((1, d, n_a), lambda i: (layer, 0, 0), pipeline_mode=once),
            pl.BlockSpec((1, d, n - n_a), lambda i: (layer, 0, 0), pipeline_mode=once),
            pl.BlockSpec((d, LANES), lambda i: (0, 0), pipeline_mode=once),
            pl.BlockSpec((1, LANES), lambda i: (0, 0)),
        ],
        out_specs=[
            pl.BlockSpec((n_qkv // HEAD_DIM, tm, HEAD_DIM), lambda i: (0, i, 0)),
            pl.BlockSpec((tm, n - n_qkv), lambda i: (i, 0)),
            pl.BlockSpec((N_HEADS, 1, tm), lambda i: (0, 0, i)),
        ],
        out_shape=[
            jax.ShapeDtypeStruct((n_qkv // HEAD_DIM, s, HEAD_DIM), BF16),
            jax.ShapeDtypeStruct((s, n - n_qkv), F32),
            jax.ShapeDtypeStruct((N_HEADS, 1, s), F32),
        ],
        scratch_shapes=[pltpu.VMEM((tm, d), BF16), pltpu.VMEM((8, LANES), F32)],
        compiler_params=_params(("arbitrary",)),
        name="inproj",
    )(x, g, mod, w_all, w_tail, w_f, b_f_row)


AUG_TERMS = 3
ONES_ROWS = 16
Q_STRIP = 1024
LOG2E = math.log2(math.e)


SAFE_EXP = 60.0
ZERO_EXP = -136.0
NORM_SLACK = 1.02


def _tile_lanes(row, n):
    return jnp.concatenate([row] * (n // LANES), axis=1)


def _attn_kernel(q_ref, k_ref, v_ref, ft_ref, g_ref, o_ref,
                 kaug_ref, vt_ref, base_ref, bend_ref, kall_ref, acc_ref, *, tq, tk):
    i = pl.program_id(1)
    seq = k_ref.shape[1]
    dh = HEAD_DIM
    sub = lax.broadcasted_iota(jnp.int32, (LANES, LANES), 0)
    lane_row = lax.broadcasted_iota(jnp.int32, (1, LANES), 1)
    ones_sq = jnp.ones((LANES, LANES), BF16)

    @pl.when(i == 0)
    def _():
        vt_ref[dh:, :] = jnp.ones((ONES_ROWS, seq), BF16)
        kall_ref[...] = jnp.zeros_like(kall_ref)
        bend_ref[...] = jnp.zeros_like(bend_ref)

        def fill(c, carry):
            c_off = pl.multiple_of(c * tk, tk)
            b_row = (-LOG2E) * ft_ref[0, :, pl.ds(c_off, tk)]
            base = jnp.broadcast_to(b_row[:, 0:1], (1, LANES))
            base_ref[pl.ds(c, 1), :] = base
            bend_ref[0:1, :] = jnp.where(
                lane_row == c, jnp.broadcast_to(b_row[:, tk - 1:tk], (1, LANES)),
                bend_ref[0:1, :])
            rel = b_row - _tile_lanes(base, tk)
            hi = rel.astype(BF16).astype(F32)
            mid = (rel - hi).astype(BF16).astype(F32)
            lo = (rel - hi - mid).astype(BF16).astype(F32)
            for bb in range(tk // LANES):
                off = pl.multiple_of(c_off + bb * LANES, LANES)
                cs = slice(bb * LANES, (bb + 1) * LANES)
                terms = jnp.where(sub == 0, hi[:, cs], jnp.where(
                    sub == 1, mid[:, cs], jnp.where(sub == 2, lo[:, cs], 0.0)))
                kb = k_ref[0, pl.ds(off, LANES), :]
                kaug_ref[pl.ds(off, LANES), 0:dh] = kb
                kaug_ref[pl.ds(off, LANES), dh:] = terms.T.astype(BF16)
                vt_ref[0:dh, pl.ds(off, LANES)] = (
                    v_ref[0, pl.ds(off, LANES), :].astype(F32).T.astype(BF16))
                kf = kb.astype(F32)
                n2 = jnp.dot((kf * kf).astype(BF16), ones_sq, preferred_element_type=F32)
                kall_ref[0:1, :] = jnp.maximum(kall_ref[0:1, :],
                                               jnp.max(n2, axis=0, keepdims=True))
            return carry
        lax.fori_loop(0, seq // tk, fill, 0)

    lane_q = lax.broadcasted_iota(jnp.int32, (tq, LANES), 1)
    q = q_ref[0]
    q_aug = jnp.concatenate(
        [q, jnp.where(lane_q < AUG_TERMS, 1.0, 0.0).astype(BF16)], axis=1)
    acc_ref[...] = jnp.zeros_like(acc_ref)
    base_q = base_ref[pl.ds(i, 1), :]
    n_strip = tq // Q_STRIP

    def chunk(j):
        k_off = pl.multiple_of(j * tk, tk)
        kc = kaug_ref[pl.ds(k_off, tk), :]
        vc = vt_ref[:, pl.ds(k_off, tk)]
        delta = base_ref[pl.ds(j, 1), :] - base_q
        return kc, vc, delta

    def scores(kc, lo, hi):
        return lax.dot_general(kc, q_aug[lo:hi, :], (((1,), (1,)), ((), ())),
                               preferred_element_type=F32)

    def online_step(j, m, masked):
        kc, vc, delta = chunk(j)
        delta = _tile_lanes(delta, Q_STRIP)
        m_out = []
        for st in range(n_strip):
            lo, hi = st * Q_STRIP, (st + 1) * Q_STRIP
            s = scores(kc, lo, hi)
            if masked:
                key = lax.broadcasted_iota(jnp.int32, (tk, Q_STRIP), 0)
                qry = lax.broadcasted_iota(jnp.int32, (tk, Q_STRIP), 1) + lo
                s = jnp.where(key <= qry, s, -jnp.inf)
            m_old = m[st] - delta
            m_new = jnp.maximum(m_old, jnp.max(s, axis=0, keepdims=True))
            p = jnp.exp2(s - m_new).astype(BF16)
            corr = jnp.exp2(m_old - m_new)
            pv = jnp.dot(vc, p, preferred_element_type=F32)
            acc_ref[:, lo:hi] = acc_ref[:, lo:hi] * corr + pv
            m_out.append(m_new + delta)
        return tuple(m_out)

    q_off = pl.multiple_of(i * tq, tq)
    ones_row = jnp.ones((8, dh), BF16)
    row_sum = lambda a: lax.dot_general(ones_row, a.astype(BF16), (((1,), (1,)), ((), ())),
                                        preferred_element_type=F32)[0:1, :]
    qf = q.astype(F32)
    qk_self = row_sum(qf * kaug_ref[pl.ds(q_off, tq), 0:dh].astype(F32))
    rel_q = (-LOG2E) * ft_ref[0, :, pl.ds(q_off, tq)] - _tile_lanes(base_q, tq)
    m_row = qk_self + rel_q
    qk_bound = jnp.sqrt(row_sum(qf * qf) * _tile_lanes(kall_ref[0:1, :], tq)) * NORM_SLACK + 1.0
    slack_diag = jnp.max(qk_bound - qk_self)
    slack = jnp.max(qk_bound - m_row)
    live = jnp.logical_and(slack + (bend_ref[0:1, :] - base_q) >= ZERO_EXP, lane_row < i)
    n_live = jnp.sum(live.astype(jnp.int32))

    def one_pass(j):
        kc, vc, delta = chunk(j)
        ref = m_row - _tile_lanes(delta, tq)
        for st in range(n_strip):
            lo, hi = st * Q_STRIP, (st + 1) * Q_STRIP
            x = scores(kc, lo, hi) - ref[:, lo:hi]
            acc_ref[:, lo:hi] += jnp.dot(vc, jnp.exp2(x).astype(BF16),
                                         preferred_element_type=F32)

    def diag_pass():
        kc, vc, _ = chunk(i)
        half = tk // 2
        tri = (lax.broadcasted_iota(jnp.int32, (half, half), 0)
               <= lax.broadcasted_iota(jnp.int32, (half, half), 1))
        x = lax.dot_general(kc[0:half, :], q_aug, (((1,), (1,)), ((), ())),
                            preferred_element_type=F32) - m_row
        x = jnp.concatenate([jnp.where(tri, x[:, 0:half], -jnp.inf), x[:, half:]], axis=1)
        acc_ref[...] += jnp.dot(vc[:, 0:half], jnp.exp2(x).astype(BF16),
                                preferred_element_type=F32)
        x = lax.dot_general(kc[half:, :], q_aug[half:, :], (((1,), (1,)), ((), ())),
                            preferred_element_type=F32) - m_row[:, half:]
        acc_ref[:, half:] += jnp.dot(vc[:, half:], jnp.exp2(jnp.where(tri, x, -jnp.inf)).astype(BF16),
                                     preferred_element_type=F32)

    diag_pass()

    @pl.when(slack_diag <= SAFE_EXP)
    def _():
        def fast_pair(jj, carry):
            one_pass(i - 1 - 2 * jj)
            one_pass(i - 2 - 2 * jj)
            return carry
        lax.fori_loop(0, n_live // 2, fast_pair, 0)

        @pl.when(n_live % 2 == 1)
        def _():
            one_pass(i - n_live)

    @pl.when(slack_diag > SAFE_EXP)
    def _():
        acc_ref[...] = jnp.zeros_like(acc_ref)
        m0 = tuple(jnp.full((1, Q_STRIP), -jnp.inf, F32) for _ in range(n_strip))
        m_d = online_step(i, m0, True)
        lax.fori_loop(0, i, lambda jj, m: online_step(i - 1 - jj, m, False), m_d)

    y = (acc_ref[0:dh, :] / acc_ref[dh:dh + 1, :]).T
    o_ref[...] = (y * _silu(g_ref[...])).astype(o_ref.dtype)


def _attention(qkv, ft, rest, *, d_attn):
    s = qkv.shape[1]
    tq = _pick(s, 1024)
    nh = d_attn // HEAD_DIM
    kern = functools.partial(_attn_kernel, tq=tq, tk=tq)
    return pl.pallas_call(
        kern,
        grid=(nh, s // tq),
        in_specs=[
            pl.BlockSpec((1, tq, HEAD_DIM), lambda h, i: (h, i, 0)),
            pl.BlockSpec((1, s, HEAD_DIM), lambda h, i: (nh + h, 0, 0)),
            pl.BlockSpec((1, s, HEAD_DIM), lambda h, i: (2 * nh + h, 0, 0)),
            pl.BlockSpec((1, 1, s), lambda h, i: (h, 0, 0)),
            pl.BlockSpec((tq, HEAD_DIM), lambda h, i: (i, h)),
        ],
        out_specs=pl.BlockSpec((tq, HEAD_DIM), lambda h, i: (i, h)),
        out_shape=jax.ShapeDtypeStruct((s, d_attn), BF16),
        scratch_shapes=[
            pltpu.VMEM((s, 2 * HEAD_DIM), BF16),
            pltpu.VMEM((HEAD_DIM + ONES_ROWS, s), BF16),
            pltpu.VMEM((max(s // tq, 8), LANES), F32),
            pltpu.VMEM((8, LANES), F32),
            pltpu.VMEM((8, LANES), F32),
            pltpu.VMEM((HEAD_DIM + ONES_ROWS, tq), F32),
        ],
        compiler_params=_params(("arbitrary", "arbitrary")),
        name="fox_attention",
    )(qkv, qkv, qkv, ft, rest)


def _pool_branch(i, u, prev, gate, w_ref, scale):
    tp = u.shape[0]
    prev = jnp.where(i > 0, prev, 0.0)
    t1 = lax.broadcasted_iota(jnp.int32, (tp, POOL_GROUP), 0) + (i * tp + 1)
    outs = []
    for g, w in enumerate(POOL_WINDOWS):
        lo, hi = g * POOL_GROUP, (g + 1) * POOL_GROUP
        ug = u[:, lo:hi]
        ext = jnp.concatenate([prev[:, lo:hi], ug], axis=0)
        win = ext
        span = 1
        while span < w:
            win = win + pltpu.roll(win, span, axis=0)
            span *= 2
        win = win[POOL_HALO:]
        cnt = jnp.minimum(t1, w).astype(F32)
        pooled = win / cnt - ug
        mixed = jnp.dot(pooled.astype(BF16), w_ref[g], preferred_element_type=F32)
        outs.append(mixed)
    mixed = jnp.concatenate(outs, axis=1) * scale
    return (mixed * _silu(gate)).astype(BF16)


SSM_CHUNK = 8


def _discretise(lr, li, ldt):
    dt = jnp.exp(ldt)
    mag = jnp.exp(lr * dt)
    ab_re = mag * jnp.cos(li * dt)
    ab_im = mag * jnp.sin(li * dt)
    den = lr * lr + li * li
    nr = ab_re - 1.0
    ni = ab_im
    z_re = (nr * lr + ni * li) / den
    z_im = (ni * lr - nr * li) / den
    return ab_re, ab_im, z_re, z_im


def _powers(a_re, a_im, n):
    out = [(jnp.ones_like(a_re), jnp.zeros_like(a_im))]
    for _ in range(n):
        p_re, p_im = out[-1]
        out.append((p_re * a_re - p_im * a_im, p_re * a_im + p_im * a_re))
    return out


def _ssm_prep_kernel(lr_ref, li_ref, ldt_ref, lrc_ref, lic_ref, ldtc_ref,
                     br_ref, bi_ref, cr_ref, ci_ref, t_ref, p_ref, e_ref, al_ref):
    nl = SSM_CHUNK
    cw = br_ref.shape[2]
    a_re, a_im, z_re, z_im = _discretise(lr_ref[0, 0], li_ref[0, 0], ldt_ref[0, 0])
    pw = _powers(a_re, a_im, nl)
    al_ref[0, 0] = jnp.concatenate([pw[nl][0], pw[nl][1]], axis=0)
    br = br_ref[0, 0]
    bi = bi_ref[0, 0]
    bb_re = z_re * br - z_im * bi
    bb_im = z_re * bi + z_im * br
    bb = jnp.concatenate([bb_re, bb_im], axis=1)
    ac_re, ac_im, _, _ = _discretise(lrc_ref[0, 0], lic_ref[0, 0], ldtc_ref[0, 0])
    pwc = _powers(ac_re, ac_im, nl)
    cr = cr_ref[0, 0]
    ci = ci_ref[0, 0]
    ca = [jnp.concatenate([cr * q_re - ci * q_im, -(cr * q_im + ci * q_re)], axis=0)
          for q_re, q_im in pwc]
    kd_all = jnp.dot(bb, jnp.concatenate(ca[:nl], axis=1), preferred_element_type=F32,
                     precision=lax.Precision.HIGHEST).astype(BF16)
    kd = [kd_all[:, d * cw:(d + 1) * cw] for d in range(nl)]
    zero = jnp.zeros((cw, cw), BF16)
    for src in range(nl):
        for dst in range(nl):
            t_ref[0, 0, src * cw:(src + 1) * cw, dst * cw:(dst + 1) * cw] = (
                kd[dst - src] if dst >= src else zero)
        q_re, q_im = pw[nl - 1 - src]
        p_ref[0, 0, src * cw:(src + 1) * cw, :] = jnp.concatenate(
            [bb_re * q_re - bb_im * q_im, bb_re * q_im + bb_im * q_re], axis=1).astype(BF16)
        e_ref[0, 0, :, src * cw:(src + 1) * cw] = ca[src + 1].astype(BF16)


def _ssm_prep(lam_re, lam_im, log_dt, b_re, b_im, c_re, c_im):
    depth, ng, ns = lam_re.shape
    gc = b_re.shape[-1]
    n_slab = ng // SSM_SLAB_GROUPS
    eye = jnp.eye(SSM_SLAB_GROUPS, dtype=F32)
    sw = SSM_SLAB_GROUPS * ns

    def place_b(b):
        b = b.reshape(depth, n_slab, SSM_SLAB_GROUPS, ns, gc).transpose(0, 1, 2, 4, 3)
        return (b[:, :, :, :, None, :] * eye[None, None, :, None, :, None]).reshape(
            depth, n_slab, SSM_SLAB_GROUPS * gc, sw)

    def place_c(c):
        c = c.reshape(depth, n_slab, SSM_SLAB_GROUPS, gc, ns).transpose(0, 1, 2, 4, 3)
        return (c[:, :, :, :, None, :] * eye[None, None, :, None, :, None]).reshape(
            depth, n_slab, sw, SSM_SLAB_GROUPS * gc)

    cw = SSM_SLAB_GROUPS * gc
    nl = SSM_CHUNK
    ldt = jnp.broadcast_to(log_dt[:, :, None], (depth, ng, ns))
    row = lambda v: v.reshape(depth, n_slab, 1, sw)
    col = lambda v: jnp.broadcast_to(v.reshape(depth, n_slab, sw, 1), (depth, n_slab, sw, cw))
    blk = lambda *shape: pl.BlockSpec((1, 1) + shape, lambda l, k: (l, k, 0, 0))
    return pl.pallas_call(
        _ssm_prep_kernel,
        grid=(depth, n_slab),
        in_specs=[blk(1, sw)] * 3 + [blk(sw, cw)] * 3 + [blk(cw, sw)] * 2 + [blk(sw, cw)] * 2,
        out_specs=[blk(nl * cw, nl * cw), blk(nl * cw, 2 * sw), blk(2 * sw, nl * cw),
                   blk(2, sw)],
        out_shape=[
            jax.ShapeDtypeStruct((depth, n_slab, nl * cw, nl * cw), BF16),
            jax.ShapeDtypeStruct((depth, n_slab, nl * cw, 2 * sw), BF16),
            jax.ShapeDtypeStruct((depth, n_slab, 2 * sw, nl * cw), BF16),
            jax.ShapeDtypeStruct((depth, n_slab, 2, sw), F32),
        ],
        compiler_params=_params(("arbitrary", "arbitrary")),
        name="ssm_prep",
    )(row(lam_re), row(lam_im), row(ldt), col(lam_re), col(lam_im), col(ldt),
      place_b(b_re), place_b(b_im), place_c(c_re), place_c(c_im))


def _gelu_tanh(y):
    c = math.sqrt(2.0 / math.pi)
    return 0.5 * y * (1.0 + jnp.tanh(c * (y + 0.044715 * (y * y * y))))


SSM_SLABS_PER_STEP = 2


def _ssm_kernel(*refs):
    per = SSM_SLABS_PER_STEP
    u_refs = refs[:per]
    t_ref, p_ref, e_ref, al_ref, d_ref, y_ref, uc_ref, carry_ref = refs[per:]
    i = pl.program_id(1)
    nl = SSM_CHUNK
    ts, cw = u_refs[0].shape
    rows = ts // nl
    sw = al_ref.shape[3]

    @pl.when(i == 0)
    def _():
        carry_ref[...] = jnp.zeros_like(carry_ref)

    row = lax.broadcasted_iota(jnp.int32, (rows, sw), 0)
    first = row == 0
    vs, yts = [], []
    for k in range(per):
        for tau in range(nl):
            uc_ref[k, :, tau * cw:(tau + 1) * cw] = (
                u_refs[k][pl.ds(tau, rows, stride=nl), :].astype(BF16))
        vs.append(jnp.dot(uc_ref[k], p_ref[0, k], preferred_element_type=F32))
    for k in range(per):
        yts.append(jnp.dot(uc_ref[k], t_ref[0, k], preferred_element_type=F32))
    for k in range(per):
        u_ref = u_refs[k]
        xr = vs[k][:, :sw]
        xi = vs[k][:, sw:]
        a_re = al_ref[0, k, 0:1, :]
        a_im = al_ref[0, k, 1:2, :]
        c_re = carry_ref[k, 0:1, :]
        c_im = carry_ref[k, 1:2, :]
        xr = xr + jnp.where(first, a_re * c_re - a_im * c_im, 0.0)
        xi = xi + jnp.where(first, a_re * c_im + a_im * c_re, 0.0)
        q_re, q_im = a_re, a_im
        d = 1
        while d < rows:
            if d % 8:
                keep = row >= d
                sr = jnp.where(keep, pltpu.roll(xr, d, axis=0), 0.0)
                si = jnp.where(keep, pltpu.roll(xi, d, axis=0), 0.0)
                xr, xi = xr + (q_re * sr - q_im * si), xi + (q_re * si + q_im * sr)
            else:
                sr, si = xr[:rows - d], xi[:rows - d]
                xr, xi = (
                    jnp.concatenate([xr[:d], xr[d:] + (q_re * sr - q_im * si)], axis=0),
                    jnp.concatenate([xi[:d], xi[d:] + (q_re * si + q_im * sr)], axis=0))
            q_re, q_im = q_re * q_re - q_im * q_im, 2.0 * (q_re * q_im)
            d *= 2
        pr = jnp.where(first, c_re, pltpu.roll(xr, 1, axis=0))
        pi = jnp.where(first, c_im, pltpu.roll(xi, 1, axis=0))
        carry_ref[k, 0:1, :] = xr[rows - 1:rows, :]
        carry_ref[k, 1:2, :] = xi[rows - 1:rows, :]
        xp = jnp.concatenate([pr, pi], axis=1).astype(BF16)
        yc = yts[k] + jnp.dot(xp, e_ref[0, k], preferred_element_type=F32)
        for tau in range(nl):
            y_ref[k, pl.ds(tau, rows, stride=nl), :] = (
                yc[:, tau * cw:(tau + 1) * cw]
                + d_ref[:, k * cw:(k + 1) * cw] * u_ref[pl.ds(tau, rows, stride=nl), :])


def _ssm(rest, t_mat, p_mat, e_mat, a_l, d_row, *, layer, col_u):
    s = rest.shape[0]
    n_slab, cw = t_mat.shape[1], d_row.shape[1] // t_mat.shape[1]
    per = SSM_SLABS_PER_STEP
    ts = _pick(s, 4096)
    cu = col_u // cw
    sw = a_l.shape[3]
    mat = lambda m: pl.BlockSpec((1, per) + m.shape[2:], lambda k, i: (layer, k, 0, 0))
    u_spec = lambda n: pl.BlockSpec((ts, cw), lambda k, i: (i, cu + per * k + n))
    return pl.pallas_call(
        _ssm_kernel,
        grid=(n_slab // per, s // ts),
        in_specs=[u_spec(n) for n in range(per)] + [
            mat(t_mat), mat(p_mat), mat(e_mat), mat(a_l),
            pl.BlockSpec((1, per * cw), lambda k, i: (0, k)),
        ],
        out_specs=pl.BlockSpec((per, ts, cw), lambda k, i: (k, i, 0)),
        out_shape=jax.ShapeDtypeStruct((n_slab, s, cw), F32),
        scratch_shapes=[pltpu.VMEM((per, ts // SSM_CHUNK, SSM_CHUNK * cw), BF16),
                        pltpu.VMEM((per, 2, sw), F32)],
        compiler_params=_params(("arbitrary", "arbitrary")),
        name="ssm",
    )(*([rest] * per), t_mat, p_mat, e_mat, a_l, d_row)


def _glu_branch(y_raw, gate, wg_ref, bg):
    y = _gelu_tanh(y_raw)
    z = jnp.dot(y.astype(BF16), wg_ref[...], preferred_element_type=F32) + bg
    return (y * _sigmoid(z) * _silu(gate)).astype(BF16)


def _outproj_kernel(ya_ref, up_ref, prev_ref, gp_ref, ysr_ref, gs_ref, wp_ref, sc_ref,
                    wg_ref, bg_ref, w_ref, x_ref, mod_ref, fg_ref, o_ref, *, final):
    i = pl.program_id(0)
    da = ya_ref.shape[1]
    dp = up_ref.shape[1]
    d = x_ref.shape[1]
    tn = PROJ_TN
    ya = ya_ref[...]
    parts = [jnp.dot(ya, w_ref[0, 0:da, c0:c0 + tn], preferred_element_type=F32)
             for c0 in range(0, d, tn)]
    yp = _pool_branch(i, up_ref[...], prev_ref[...], gp_ref[...], wp_ref, sc_ref[...])
    y_raw = jnp.concatenate([ysr_ref[k] for k in range(ysr_ref.shape[0])], axis=1)
    ys = _glu_branch(y_raw, gs_ref[...], wg_ref, bg_ref[...])
    for n, c0 in enumerate(range(0, d, tn)):
        out = (parts[n]
               + jnp.dot(yp, w_ref[0, da:da + dp, c0:c0 + tn], preferred_element_type=F32)
               + jnp.dot(ys, w_ref[0, da + dp:, c0:c0 + tn], preferred_element_type=F32))
        o_ref[:, c0:c0 + tn] = x_ref[:, c0:c0 + tn] + mod_ref[:, c0:c0 + tn] * out
    if final:
        xn = o_ref[...]
        ms = jnp.mean(xn * xn, axis=-1, keepdims=True)
        o_ref[...] = xn * lax.rsqrt(ms + NORM_EPS) * fg_ref[...]


def _outproj(ya, rest, y_ssm, w_pool, pool_scale, w_glu, b_glu, w_out, x, gate, final_g,
             *, layer, final, col_up, col_gp, col_gs):
    s, d = x.shape
    dm = w_out.shape[1]
    d_pool = pool_scale.shape[1]
    d_ssm = b_glu.shape[1]
    tm = _pick(s, 512)
    halo_blocks = tm // POOL_HALO
    cu, cg, cs = col_up // d_pool, col_gp // d_pool, col_gs // d_ssm
    const = lambda a: pl.BlockSpec(a.shape, lambda i: (0,) * a.ndim)
    return pl.pallas_call(
        functools.partial(_outproj_kernel, final=final),
        grid=(s // tm,),
        in_specs=[
            pl.BlockSpec((tm, ya.shape[1]), lambda i: (i, 0)),
            pl.BlockSpec((tm, d_pool), lambda i: (i, cu)),
            pl.BlockSpec((POOL_HALO, d_pool),
                         lambda i: (jnp.maximum(i * halo_blocks - 1, 0), cu)),
            pl.BlockSpec((tm, d_pool), lambda i: (i, cg)),
            pl.BlockSpec((y_ssm.shape[0], tm, y_ssm.shape[2]), lambda i: (0, i, 0)),
            pl.BlockSpec((tm, d_ssm), lambda i: (i, cs)),
            const(w_pool), const(pool_scale), const(w_glu), const(b_glu),
            pl.BlockSpec((1, dm, d), lambda i: (layer, 0, 0), pipeline_mode=pl.Buffered(1)),
            pl.BlockSpec((tm, d), lambda i: (i, 0)),
            pl.BlockSpec((1, d), lambda i: (0, 0)),
            pl.BlockSpec((1, d), lambda i: (0, 0)),
        ],
        out_specs=pl.BlockSpec((tm, d), lambda i: (i, 0)),
        out_shape=jax.ShapeDtypeStruct((s, d), F32),
        compiler_params=_params(("arbitrary",)),
        name="outproj",
    )(ya, rest, rest, rest, y_ssm, rest, w_pool, pool_scale, w_glu, b_glu, w_out, x, gate,
      final_g)


def kernel(x, c, norm_g, w_ada, b_ada, w_in, b_f, w_pool, pool_scale, lam_re, lam_im,
           ssm_b_re, ssm_b_im, ssm_c_re, ssm_c_im, ssm_d, log_dt, w_glu, b_glu, w_out,
           final_g):
    b, s, d = x.shape
    assert b == 1
    depth = w_in.shape[0]
    d_pool = pool_scale.shape[1]
    d_ssm = b_glu.shape[1]
    d_attn = N_HEADS * HEAD_DIM
    n_f = b_f.shape[1]
    assert n_f == N_HEADS and w_in.shape[2] == 4 * d_attn + n_f + 2 * d_pool + 2 * d_ssm

    w_all, w_f, w_tail = _wprep(w_in, n_head=4 * d_attn, n_f=n_f)
    b_f_row = jnp.pad(b_f, ((0, 0), (0, LANES - n_f))).reshape(depth, 1, LANES)
    col_up = d_attn
    col_gp = col_up + d_pool
    col_us = col_gp + d_pool
    col_gs = col_us + d_ssm

    mod = _ada_mod(c, w_ada, b_ada).reshape(depth, 3, d)
    t_all, p_all, e_all, al_all = _ssm_prep(lam_re, lam_im, log_dt, ssm_b_re, ssm_b_im,
                                            ssm_c_re, ssm_c_im)
    w_pool_b = w_pool.astype(BF16)
    w_glu_b = w_glu.astype(BF16)
    w_out_b = w_out.astype(BF16)

    xs = x.reshape(s, d)
    for l in range(depth):
        qkv, rest, ft = _inproj(xs, norm_g[l].reshape(1, d), mod[l], w_all, w_tail, w_f[l],
                                b_f_row[l], layer=l, d_attn=d_attn)
        ya = _attention(qkv, ft, rest, d_attn=d_attn)
        y_ssm = _ssm(rest, t_all, p_all, e_all, al_all, ssm_d[l].reshape(1, d_ssm),
                     layer=l, col_u=col_us)
        xs = _outproj(ya, rest, y_ssm, w_pool_b[l], pool_scale[l].reshape(1, d_pool),
                      w_glu_b[l], b_glu[l].reshape(1, d_ssm), w_out_b, xs, mod[l, 2:3, :],
                      final_g.reshape(1, d), layer=l, final=(l == depth - 1),
                      col_up=col_up, col_gp=col_gp, col_gs=col_gs)
    return xs.reshape(b, s, d).astype(x.dtype)
```

```python
import functools
import math

import jax
import jax.numpy as jnp
from jax import lax
from jax.experimental import pallas as pl
from jax.experimental.pallas import tpu as pltpu

F32 = jnp.float32
BF16 = jnp.bfloat16

N_HEADS = 8
HEAD_DIM = 128
POOL_WINDOWS = (2, 4, 8, 16)
POOL_GROUP = 128
POOL_HALO = 16
SSM_GROUP = 16
SSM_STATE = 64
SSM_SLAB_GROUPS = 8
NORM_EPS = 1e-6
LANES = 128
VMEM_LIMIT = 56 * 1024 * 1024
PROJ_TN = 512


def _params(sem, vmem=VMEM_LIMIT):
    return pltpu.CompilerParams(dimension_semantics=sem, vmem_limit_bytes=vmem)


def _sigmoid(x):
    return 1.0 / (1.0 + jnp.exp(-x))


def _silu(x):
    return x * _sigmoid(x)


def _pick(n, pref):
    t = min(n, pref)
    while n % t:
        t //= 2
    return t


def _ada_kernel(c_ref, w_ref, b_ref, o_ref):
    ca = _silu(c_ref[...])
    o_ref[0] = jnp.sum(w_ref[0] * ca, axis=0, keepdims=True) + b_ref[0]


def _ada_mod(c, w_ada, b_ada):
    depth, d, n = w_ada.shape
    tn = _pick(n, 2048)
    return pl.pallas_call(
        _ada_kernel,
        grid=(depth, n // tn),
        in_specs=[
            pl.BlockSpec((d, 1), lambda l, j: (0, 0)),
            pl.BlockSpec((1, d, tn), lambda l, j: (l, 0, j)),
            pl.BlockSpec((1, 1, tn), lambda l, j: (l, 0, j)),
        ],
        out_specs=pl.BlockSpec((1, 1, tn), lambda l, j: (l, 0, j)),
        out_shape=jax.ShapeDtypeStruct((depth, 1, n), F32),
        compiler_params=_params(("arbitrary", "arbitrary")),
        name="ada_mod",
    )(c.reshape(d, 1), w_ada, b_ada.reshape(depth, 1, n))


def _inproj_kernel(x_ref, g_ref, mod_ref, wa_ref, wt_ref, wf_ref, bf_ref,
                   qkv_ref, rest_ref, ft_ref, h_ref, carry_ref, *, d_attn, q_scale):
    i = pl.program_id(0)
    tm = x_ref.shape[0]
    n_qkv = qkv_ref.shape[0] * HEAD_DIM
    n_a = wa_ref.shape[2]
    n = n_a + wt_ref.shape[2]
    tn = PROJ_TN

    def put_heads(c0, val):
        for hh in range(tn // HEAD_DIM):
            qkv_ref[c0 // HEAD_DIM + hh] = val[:, hh * HEAD_DIM:(hh + 1) * HEAD_DIM]

    x = x_ref[...]
    ms = jnp.mean(x * x, axis=-1, keepdims=True)
    shift = mod_ref[0:1, :]
    scale = mod_ref[1:2, :]
    h = (x * lax.rsqrt(ms + NORM_EPS) * g_ref[...]) * (1.0 + scale) + shift
    h_ref[...] = h.astype(BF16)

    for c0 in range(0, n, tn):
        w = wa_ref[0, :, c0:c0 + tn] if c0 < n_a else wt_ref[0, :, c0 - n_a:c0 - n_a + tn]
        proj = jnp.dot(h_ref[...], w, preferred_element_type=F32)
        if c0 < d_attn:
            put_heads(c0, (proj * q_scale).astype(BF16))
        elif c0 < n_qkv:
            put_heads(c0, proj.astype(BF16))
        else:
            rest_ref[:, c0 - n_qkv:c0 - n_qkv + tn] = proj

    f = jnp.dot(h_ref[...], wf_ref[...], preferred_element_type=F32) + bf_ref[...]
    logf = -(jnp.maximum(-f, 0.0) + jnp.log1p(jnp.exp(-jnp.abs(f))))
    row = lax.broadcasted_iota(jnp.int32, logf.shape, 0)
    cum = logf
    d = 1
    while d < tm:
        cum = cum + jnp.where(row >= d, pltpu.roll(cum, d, axis=0), 0.0)
        d *= 2

    @pl.when(i == 0)
    def _():
        carry_ref[...] = jnp.zeros_like(carry_ref)

    cum = cum + carry_ref[0:1, :]
    carry_ref[...] = jnp.broadcast_to(cum[tm - 1:tm, :], carry_ref.shape)
    cum_t = cum.T
    for hh in range(N_HEADS):
        ft_ref[hh] = cum_t[hh:hh + 1, :]


def _inproj(x, g, mod, w_all, w_tail, w_f, b_f_row, *, layer, d_attn):
    s, d = x.shape
    n_a = 4 * d_attn
    n = n_a + w_tail.shape[2]
    n_qkv = 3 * d_attn
    tm = _pick(s, 256)
    kern = functools.partial(_inproj_kernel, d_attn=d_attn,
                             q_scale=HEAD_DIM ** -0.5 * math.log2(math.e))
    once = pl.Buffered(1)
    return pl.pallas_call(
        kern,
        grid=(s // tm,),
        in_specs=[
            pl.BlockSpec((tm, d), lambda i: (i, 0)),
            pl.BlockSpec((1, d), lambda i: (0, 0)),
            pl.BlockSpec((3, d), lambda i: (0, 0)),
            pl.BlockSpec((1, d, n_a), lambda i: (layer, 0, 0), pipeline_mode=once),
            pl.BlockSpec((1, d, n - n_a), lambda i: (layer, 0, 0), pipeline_mode=once),
            pl.BlockSpec((d, LANES), lambda i: (0, 0), pipeline_mode=once),
            pl.BlockSpec((1, LANES), lambda i: (0, 0)),
        ],
        out_specs=[
            pl.BlockSpec((n_qkv // HEAD_DIM, tm, HEAD_DIM), lambda i: (0, i, 0)),
            pl.BlockSpec((tm, n - n_qkv), lambda i: (i, 0)),
            pl.BlockSpec((N_HEADS, 1, tm), lambda i: (0, 0, i)),
        ],
        out_shape=[
            jax.ShapeDtypeStruct((n_qkv // HEAD_DIM, s, HEAD_DIM), BF16),
            jax.ShapeDtypeStruct((s, n - n_qkv), F32),
            jax.ShapeDtypeStruct((N_HEADS, 1, s), F32),
        ],
        scratch_shapes=[pltpu.VMEM((tm, d), BF16), pltpu.VMEM((8, LANES), F32)],
        compiler_params=_params(("arbitrary",)),
        name="inproj",
    )(x, g, mod, w_all, w_tail, w_f, b_f_row)


AUG_TERMS = 3
ONES_ROWS = 16
Q_STRIP = 1024
LOG2E = math.log2(math.e)


SAFE_EXP = 60.0
ZERO_EXP = -136.0
NORM_SLACK = 1.02


def _tile_lanes(row, n):
    return jnp.concatenate([row] * (n // LANES), axis=1)


def _attn_kernel(q_ref, k_ref, v_ref, ft_ref, g_ref, o_ref,
                 kaug_ref, vt_ref, base_ref, bend_ref, kall_ref, acc_ref, *, tq, tk):
    i = pl.program_id(1)
    seq = k_ref.shape[1]
    dh = HEAD_DIM
    sub = lax.broadcasted_iota(jnp.int32, (LANES, LANES), 0)
    lane_row = lax.broadcasted_iota(jnp.int32, (1, LANES), 1)
    ones_sq = jnp.ones((LANES, LANES), BF16)

    @pl.when(i == 0)
    def _():
        vt_ref[dh:, :] = jnp.ones((ONES_ROWS, seq), BF16)
        kall_ref[...] = jnp.zeros_like(kall_ref)
        bend_ref[...] = jnp.zeros_like(bend_ref)

        def fill(c, carry):
            c_off = pl.multiple_of(c * tk, tk)
            b_row = (-LOG2E) * ft_ref[0, :, pl.ds(c_off, tk)]
            base = jnp.broadcast_to(b_row[:, 0:1], (1, LANES))
            base_ref[pl.ds(c, 1), :] = base
            bend_ref[0:1, :] = jnp.where(
                lane_row == c, jnp.broadcast_to(b_row[:, tk - 1:tk], (1, LANES)),
                bend_ref[0:1, :])
            rel = b_row - _tile_lanes(base, tk)
            hi = rel.astype(BF16).astype(F32)
            mid = (rel - hi).astype(BF16).astype(F32)
            lo = (rel - hi - mid).astype(BF16).astype(F32)
            for bb in range(tk // LANES):
                off = pl.multiple_of(c_off + bb * LANES, LANES)
                cs = slice(bb * LANES, (bb + 1) * LANES)
                terms = jnp.where(sub == 0, hi[:, cs], jnp.where(
                    sub == 1, mid[:, cs], jnp.where(sub == 2, lo[:, cs], 0.0)))
                kb = k_ref[0, pl.ds(off, LANES), :]
                kaug_ref[pl.ds(off, LANES), 0:dh] = kb
                kaug_ref[pl.ds(off, LANES), dh:] = terms.T.astype(BF16)
                vt_ref[0:dh, pl.ds(off, LANES)] = (
                    v_ref[0, pl.ds(off, LANES), :].astype(F32).T.astype(BF16))
                kf = kb.astype(F32)
                n2 = jnp.dot((kf * kf).astype(BF16), ones_sq, preferred_element_type=F32)
                kall_ref[0:1, :] = jnp.maximum(kall_ref[0:1, :],
                                               jnp.max(n2, axis=0, keepdims=True))
            return carry
        lax.fori_loop(0, seq // tk, fill, 0)

    lane_q = lax.broadcasted_iota(jnp.int32, (tq, LANES), 1)
    q = q_ref[0]
    q_aug = jnp.concatenate(
        [q, jnp.where(lane_q < AUG_TERMS, 1.0, 0.0).astype(BF16)], axis=1)
    acc_ref[...] = jnp.zeros_like(acc_ref)
    base_q = base_ref[pl.ds(i, 1), :]
    n_strip = tq // Q_STRIP

    def chunk(j):
        k_off = pl.multiple_of(j * tk, tk)
        kc = kaug_ref[pl.ds(k_off, tk), :]
        vc = vt_ref[:, pl.ds(k_off, tk)]
        delta = base_ref[pl.ds(j, 1), :] - base_q
        return kc, vc, delta

    def scores(kc, lo, hi):
        return lax.dot_general(kc, q_aug[lo:hi, :], (((1,), (1,)), ((), ())),
                               preferred_element_type=F32)

    def online_step(j, m, masked):
        kc, vc, delta = chunk(j)
        delta = _tile_lanes(delta, Q_STRIP)
        m_out = []
        for st in range(n_strip):
            lo, hi = st * Q_STRIP, (st + 1) * Q_STRIP
            s = scores(kc, lo, hi)
            if masked:
                key = lax.broadcasted_iota(jnp.int32, (tk, Q_STRIP), 0)
                qry = lax.broadcasted_iota(jnp.int32, (tk, Q_STRIP), 1) + lo
                s = jnp.where(key <= qry, s, -jnp.inf)
            m_old = m[st] - delta
            m_new = jnp.maximum(m_old, jnp.max(s, axis=0, keepdims=True))
            p = jnp.exp2(s - m_new).astype(BF16)
            corr = jnp.exp2(m_old - m_new)
            pv = jnp.dot(vc, p, preferred_element_type=F32)
            acc_ref[:, lo:hi] = acc_ref[:, lo:hi] * corr + pv
            m_out.append(m_new + delta)
        return tuple(m_out)

    q_off = pl.multiple_of(i * tq, tq)
    ones_row = jnp.ones((8, dh), BF16)
    row_sum = lambda a: lax.dot_general(ones_row, a.astype(BF16), (((1,), (1,)), ((), ())),
                                        preferred_element_type=F32)[0:1, :]
    qf = q.astype(F32)
    qk_self = row_sum(qf * kaug_ref[pl.ds(q_off, tq), 0:dh].astype(F32))
    rel_q = (-LOG2E) * ft_ref[0, :, pl.ds(q_off, tq)] - _tile_lanes(base_q, tq)
    m_row = qk_self + rel_q
    qk_bound = jnp.sqrt(row_sum(qf * qf) * _tile_lanes(kall_ref[0:1, :], tq)) * NORM_SLACK + 1.0
    slack_diag = jnp.max(qk_bound - qk_self)
    slack = jnp.max(qk_bound - m_row)
    live = jnp.logical_and(slack + (bend_ref[0:1, :] - base_q) >= ZERO_EXP, lane_row < i)
    n_live = jnp.sum(live.astype(jnp.int32))

    def one_pass(j):
        kc, vc, delta = chunk(j)
        ref = m_row - _tile_lanes(delta, tq)
        for st in range(n_strip):
            lo, hi = st * Q_STRIP, (st + 1) * Q_STRIP
            x = scores(kc, lo, hi) - ref[:, lo:hi]
            acc_ref[:, lo:hi] += jnp.dot(vc, jnp.exp2(x).astype(BF16),
                                         preferred_element_type=F32)

    def diag_pass():
        kc, vc, _ = chunk(i)
        half = tk // 2
        tri = (lax.broadcasted_iota(jnp.int32, (half, half), 0)
               <= lax.broadcasted_iota(jnp.int32, (half, half), 1))
        x = lax.dot_general(kc[0:half, :], q_aug, (((1,), (1,)), ((), ())),
                            preferred_element_type=F32) - m_row
        x = jnp.concatenate([jnp.where(tri, x[:, 0:half], -jnp.inf), x[:, half:]], axis=1)
        acc_ref[...] += jnp.dot(vc[:, 0:half], jnp.exp2(x).astype(BF16),
                                preferred_element_type=F32)
        x = lax.dot_general(kc[half:, :], q_aug[half:, :], (((1,), (1,)), ((), ())),
                            preferred_element_type=F32) - m_row[:, half:]
        acc_ref[:, half:] += jnp.dot(vc[:, half:], jnp.exp2(jnp.where(tri, x, -jnp.inf)).astype(BF16),
                                     preferred_element_type=F32)

    diag_pass()

    @pl.when(slack_diag <= SAFE_EXP)
    def _():
        def fast_pair(jj, carry):
            one_pass(i - 1 - 2 * jj)
            one_pass(i - 2 - 2 * jj)
            return carry
        lax.fori_loop(0, n_live // 2, fast_pair, 0)

        @pl.when(n_live % 2 == 1)
        def _():
            one_pass(i - n_live)

    @pl.when(slack_diag > SAFE_EXP)
    def _():
        acc_ref[...] = jnp.zeros_like(acc_ref)
        m0 = tuple(jnp.full((1, Q_STRIP), -jnp.inf, F32) for _ in range(n_strip))
        m_d = online_step(i, m0, True)
        lax.fori_loop(0, i, lambda jj, m: online_step(i - 1 - jj, m, False), m_d)

    y = (acc_ref[0:dh, :] / acc_ref[dh:dh + 1, :]).T
    o_ref[...] = (y * _silu(g_ref[...])).astype(o_ref.dtype)


def _attention(qkv, ft, rest, *, d_attn):
    s = qkv.shape[1]
    tq = _pick(s, 1024)
    nh = d_attn // HEAD_DIM
    kern = functools.partial(_attn_kernel, tq=tq, tk=tq)
    return pl.pallas_call(
        kern,
        grid=(nh, s // tq),
        in_specs=[
            pl.BlockSpec((1, tq, HEAD_DIM), lambda h, i: (h, i, 0)),
            pl.BlockSpec((1, s, HEAD_DIM), lambda h, i: (nh + h, 0, 0)),
            pl.BlockSpec((1, s, HEAD_DIM), lambda h, i: (2 * nh + h, 0, 0)),
            pl.BlockSpec((1, 1, s), lambda h, i: (h, 0, 0)),
            pl.BlockSpec((tq, HEAD_DIM), lambda h, i: (i, h)),
        ],
        out_specs=pl.BlockSpec((tq, HEAD_DIM), lambda h, i: (i, h)),
        out_shape=jax.ShapeDtypeStruct((s, d_attn), BF16),
        scratch_shapes=[
            pltpu.VMEM((s, 2 * HEAD_DIM), BF16),
            pltpu.VMEM((HEAD_DIM + ONES_ROWS, s), BF16),
            pltpu.VMEM((max(s // tq, 8), LANES), F32),
            pltpu.VMEM((8, LANES), F32),
            pltpu.VMEM((8, LANES), F32),
            pltpu.VMEM((HEAD_DIM + ONES_ROWS, tq), F32),
        ],
        compiler_params=_params(("arbitrary", "arbitrary")),
        name="fox_attention",
    )(qkv, qkv, qkv, ft, rest)


def _pool_branch(i, u, prev, gate, w_ref, scale):
    tp = u.shape[0]
    prev = jnp.where(i > 0, prev, 0.0)
    t1 = lax.broadcasted_iota(jnp.int32, (tp, POOL_GROUP), 0) + (i * tp + 1)
    outs = []
    for g, w in enumerate(POOL_WINDOWS):
        lo, hi = g * POOL_GROUP, (g + 1) * POOL_GROUP
        ug = u[:, lo:hi]
        ext = jnp.concatenate([prev[:, lo:hi], ug], axis=0)
        win = ext
        span = 1
        while span < w:
            win = win + pltpu.roll(win, span, axis=0)
            span *= 2
        win = win[POOL_HALO:]
        cnt = jnp.minimum(t1, w).astype(F32)
        pooled = win / cnt - ug
        mixed = jnp.dot(pooled.astype(BF16), w_ref[g], preferred_element_type=F32)
        outs.append(mixed)
    mixed = jnp.concatenate(outs, axis=1) * scale
    return (mixed * _silu(gate)).astype(BF16)


SSM_CHUNK = 8


def _discretise(lr, li, ldt):
    dt = jnp.exp(ldt)
    mag = jnp.exp(lr * dt)
    ab_re = mag * jnp.cos(li * dt)
    ab_im = mag * jnp.sin(li * dt)
    den = lr * lr + li * li
    nr = ab_re - 1.0
    ni = ab_im
    z_re = (nr * lr + ni * li) / den
    z_im = (ni * lr - nr * li) / den
    return ab_re, ab_im, z_re, z_im


def _powers(a_re, a_im, n):
    out = [(jnp.ones_like(a_re), jnp.zeros_like(a_im))]
    for _ in range(n):
        p_re, p_im = out[-1]
        out.append((p_re * a_re - p_im * a_im, p_re * a_im + p_im * a_re))
    return out


def _ssm_prep_kernel(lr_ref, li_ref, ldt_ref, lrc_ref, lic_ref, ldtc_ref,
                     br_ref, bi_ref, cr_ref, ci_ref, t_ref, p_ref, e_ref, al_ref):
    nl = SSM_CHUNK
    cw = br_ref.shape[2]
    a_re, a_im, z_re, z_im = _discretise(lr_ref[0, 0], li_ref[0, 0], ldt_ref[0, 0])
    pw = _powers(a_re, a_im, nl)
    al_ref[0, 0] = jnp.concatenate([pw[nl][0], pw[nl][1]], axis=0)
    br = br_ref[0, 0]
    bi = bi_ref[0, 0]
    bb_re = z_re * br - z_im * bi
    bb_im = z_re * bi + z_im * br
    bb = jnp.concatenate([bb_re, bb_im], axis=1)
    ac_re, ac_im, _, _ = _discretise(lrc_ref[0, 0], lic_ref[0, 0], ldtc_ref[0, 0])
    pwc = _powers(ac_re, ac_im, nl)
    cr = cr_ref[0, 0]
    ci = ci_ref[0, 0]
    ca = [jnp.concatenate([cr * q_re - ci * q_im, -(cr * q_im + ci * q_re)], axis=0)
          for q_re, q_im in pwc]
    kd_all = jnp.dot(bb, jnp.concatenate(ca[:nl], axis=1), preferred_element_type=F32,
                     precision=lax.Precision.HIGHEST).astype(BF16)
    kd = [kd_all[:, d * cw:(d + 1) * cw] for d in range(nl)]
    zero = jnp.zeros((cw, cw), BF16)
    for src in range(nl):
        for dst in range(nl):
            t_ref[0, 0, src * cw:(src + 1) * cw, dst * cw:(dst + 1) * cw] = (
                kd[dst - src] if dst >= src else zero)
        q_re, q_im = pw[nl - 1 - src]
        p_ref[0, 0, src * cw:(src + 1) * cw, :] = jnp.concatenate(
            [bb_re * q_re - bb_im * q_im, bb_re * q_im + bb_im * q_re], axis=1).astype(BF16)
        e_ref[0, 0, :, src * cw:(src + 1) * cw] = ca[src + 1].astype(BF16)


def _ssm_prep(lam_re, lam_im, log_dt, b_re, b_im, c_re, c_im):
    depth, ng, ns = lam_re.shape
    gc = b_re.shape[-1]
    n_slab = ng // SSM_SLAB_GROUPS
    eye = jnp.eye(SSM_SLAB_GROUPS, dtype=F32)
    sw = SSM_SLAB_GROUPS * ns

    def place_b(b):
        b = b.reshape(depth, n_slab, SSM_SLAB_GROUPS, ns, gc).transpose(0, 1, 2, 4, 3)
        return (b[:, :, :, :, None, :] * eye[None, None, :, None, :, None]).reshape(
            depth, n_slab, SSM_SLAB_GROUPS * gc, sw)

    def place_c(c):
        c = c.reshape(depth, n_slab, SSM_SLAB_GROUPS, gc, ns).transpose(0, 1, 2, 4, 3)
        return (c[:, :, :, :, None, :] * eye[None, None, :, None, :, None]).reshape(
            depth, n_slab, sw, SSM_SLAB_GROUPS * gc)

    cw = SSM_SLAB_GROUPS * gc
    nl = SSM_CHUNK
    ldt = jnp.broadcast_to(log_dt[:, :, None], (depth, ng, ns))
    row = lambda v: v.reshape(depth, n_slab, 1, sw)
    col = lambda v: jnp.broadcast_to(v.reshape(depth, n_slab, sw, 1), (depth, n_slab, sw, cw))
    blk = lambda *shape: pl.BlockSpec((1, 1) + shape, lambda l, k: (l, k, 0, 0))
    return pl.pallas_call(
        _ssm_prep_kernel,
        grid=(depth, n_slab),
        in_specs=[blk(1, sw)] * 3 + [blk(sw, cw)] * 3 + [blk(cw, sw)] * 2 + [blk(sw, cw)] * 2,
        out_specs=[blk(nl * cw, nl * cw), blk(nl * cw, 2 * sw), blk(2 * sw, nl * cw),
                   blk(2, sw)],
        out_shape=[
            jax.ShapeDtypeStruct((depth, n_slab, nl * cw, nl * cw), BF16),
            jax.ShapeDtypeStruct((depth, n_slab, nl * cw, 2 * sw), BF16),
            jax.ShapeDtypeStruct((depth, n_slab, 2 * sw, nl * cw), BF16),
            jax.ShapeDtypeStruct((depth, n_slab, 2, sw), F32),
        ],
        compiler_params=_params(("arbitrary", "arbitrary")),
        name="ssm_prep",
    )(row(lam_re), row(lam_im), row(ldt), col(lam_re), col(lam_im), col(ldt),
      place_b(b_re), place_b(b_im), place_c(c_re), place_c(c_im))


def _gelu_tanh(y):
    c = math.sqrt(2.0 / math.pi)
    return 0.5 * y * (1.0 + jnp.tanh(c * (y + 0.044715 * (y * y * y))))


SSM_SLABS_PER_STEP = 2


def _ssm_kernel(*refs):
    per = SSM_SLABS_PER_STEP
    u_refs = refs[:per]
    t_ref, p_ref, e_ref, al_ref, d_ref, y_ref, uc_ref, carry_ref = refs[per:]
    i = pl.program_id(1)
    nl = SSM_CHUNK
    ts, cw = u_refs[0].shape
    rows = ts // nl
    sw = al_ref.shape[3]

    @pl.when(i == 0)
    def _():
        carry_ref[...] = jnp.zeros_like(carry_ref)

    row = lax.broadcasted_iota(jnp.int32, (rows, sw), 0)
    first = row == 0
    vs, yts = [], []
    for k in range(per):
        for tau in range(nl):
            uc_ref[k, :, tau * cw:(tau + 1) * cw] = (
                u_refs[k][pl.ds(tau, rows, stride=nl), :].astype(BF16))
        vs.append(jnp.dot(uc_ref[k], p_ref[0, k], preferred_element_type=F32))
    for k in range(per):
        yts.append(jnp.dot(uc_ref[k], t_ref[0, k], preferred_element_type=F32))
    for k in range(per):
        u_ref = u_refs[k]
        xr = vs[k][:, :sw]
        xi = vs[k][:, sw:]
        a_re = al_ref[0, k, 0:1, :]
        a_im = al_ref[0, k, 1:2, :]
        c_re = carry_ref[k, 0:1, :]
        c_im = carry_ref[k, 1:2, :]
        xr = xr + jnp.where(first, a_re * c_re - a_im * c_im, 0.0)
        xi = xi + jnp.where(first, a_re * c_im + a_im * c_re, 0.0)
        q_re, q_im = a_re, a_im
        d = 1
        while d < rows:
            if d % 8:
                keep = row >= d
                sr = jnp.where(keep, pltpu.roll(xr, d, axis=0), 0.0)
                si = jnp.where(keep, pltpu.roll(xi, d, axis=0), 0.0)
                xr, xi = xr + (q_re * sr - q_im * si), xi + (q_re * si + q_im * sr)
            else:
                sr, si = xr[:rows - d], xi[:rows - d]
                xr, xi = (
                    jnp.concatenate([xr[:d], xr[d:] + (q_re * sr - q_im * si)], axis=0),
                    jnp.concatenate([xi[:d], xi[d:] + (q_re * si + q_im * sr)], axis=0))
            q_re, q_im = q_re * q_re - q_im * q_im, 2.0 * (q_re * q_im)
            d *= 2
        pr = jnp.where(first, c_re, pltpu.roll(xr, 1, axis=0))
        pi = jnp.where(first, c_im, pltpu.roll(xi, 1, axis=0))
        carry_ref[k, 0:1, :] = xr[rows - 1:rows, :]
        carry_ref[k, 1:2, :] = xi[rows - 1:rows, :]
        xp = jnp.concatenate([pr, pi], axis=1).astype(BF16)
        yc = yts[k] + jnp.dot(xp, e_ref[0, k], preferred_element_type=F32)
        for tau in range(nl):
            y_ref[k, pl.ds(tau, rows, stride=nl), :] = (
                yc[:, tau * cw:(tau + 1) * cw]
                + d_ref[:, k * cw:(k + 1) * cw] * u_ref[pl.ds(tau, rows, stride=nl), :])


def _ssm(rest, t_mat, p_mat, e_mat, a_l, d_row, *, layer, col_u):
    s = rest.shape[0]
    n_slab, cw = t_mat.shape[1], d_row.shape[1] // t_mat.shape[1]
    per = SSM_SLABS_PER_STEP
    ts = _pick(s, 4096)
    cu = col_u // cw
    sw = a_l.shape[3]
    mat = lambda m: pl.BlockSpec((1, per) + m.shape[2:], lambda k, i: (layer, k, 0, 0))
    u_spec = lambda n: pl.BlockSpec((ts, cw), lambda k, i: (i, cu + per * k + n))
    return pl.pallas_call(
        _ssm_kernel,
        grid=(n_slab // per, s // ts),
        in_specs=[u_spec(n) for n in range(per)] + [
            mat(t_mat), mat(p_mat), mat(e_mat), mat(a_l),
            pl.BlockSpec((1, per * cw), lambda k, i: (0, k)),
        ],
        out_specs=pl.BlockSpec((per, ts, cw), lambda k, i: (k, i, 0)),
        out_shape=jax.ShapeDtypeStruct((n_slab, s, cw), F32),
        scratch_shapes=[pltpu.VMEM((per, ts // SSM_CHUNK, SSM_CHUNK * cw), BF16),
                        pltpu.VMEM((per, 2, sw), F32)],
        compiler_params=_params(("arbitrary", "arbitrary")),
        name="ssm",
    )(*([rest] * per), t_mat, p_mat, e_mat, a_l, d_row)


def _glu_branch(y_raw, gate, wg_ref, bg):
    y = _gelu_tanh(y_raw)
    z = jnp.dot(y.astype(BF16), wg_ref[...], preferred_element_type=F32) + bg
    return (y * _sigmoid(z) * _silu(gate)).astype(BF16)


def _outproj_kernel(ya_ref, up_ref, prev_ref, gp_ref, ysr_ref, gs_ref, wp_ref, sc_ref,
                    wg_ref, bg_ref, w_ref, x_ref, mod_ref, fg_ref, o_ref, *, final):
    i = pl.program_id(0)
    da = ya_ref.shape[1]
    dp = up_ref.shape[1]
    d = x_ref.shape[1]
    tn = PROJ_TN
    ya = ya_ref[...]
    parts = [jnp.dot(ya, w_ref[0, 0:da, c0:c0 + tn], preferred_element_type=F32)
             for c0 in range(0, d, tn)]
    yp = _pool_branch(i, up_ref[...], prev_ref[...], gp_ref[...], wp_ref, sc_ref[...])
    y_raw = jnp.concatenate([ysr_ref[k] for k in range(ysr_ref.shape[0])], axis=1)
    ys = _glu_branch(y_raw, gs_ref[...], wg_ref, bg_ref[...])
    for n, c0 in enumerate(range(0, d, tn)):
        out = (parts[n]
               + jnp.dot(yp, w_ref[0, da:da + dp, c0:c0 + tn], preferred_element_type=F32)
               + jnp.dot(ys, w_ref[0, da + dp:, c0:c0 + tn], preferred_element_type=F32))
        o_ref[:, c0:c0 + tn] = x_ref[:, c0:c0 + tn] + mod_ref[:, c0:c0 + tn] * out
    if final:
        xn = o_ref[...]
        ms = jnp.mean(xn * xn, axis=-1, keepdims=True)
        o_ref[...] = xn * lax.rsqrt(ms + NORM_EPS) * fg_ref[...]


def _outproj(ya, rest, y_ssm, w_pool, pool_scale, w_glu, b_glu, w_out, x, gate, final_g,
             *, layer, final, col_up, col_gp, col_gs):
    s, d = x.shape
    dm = w_out.shape[1]
    d_pool = pool_scale.shape[1]
    d_ssm = b_glu.shape[1]
    tm = _pick(s, 512)
    halo_blocks = tm // POOL_HALO
    cu, cg, cs = col_up // d_pool, col_gp // d_pool, col_gs // d_ssm
    const = lambda a: pl.BlockSpec(a.shape, lambda i: (0,) * a.ndim)
    return pl.pallas_call(
        functools.partial(_outproj_kernel, final=final),
        grid=(s // tm,),
        in_specs=[
            pl.BlockSpec((tm, ya.shape[1]), lambda i: (i, 0)),
            pl.BlockSpec((tm, d_pool), lambda i: (i, cu)),
            pl.BlockSpec((POOL_HALO, d_pool),
                         lambda i: (jnp.maximum(i * halo_blocks - 1, 0), cu)),
            pl.BlockSpec((tm, d_pool), lambda i: (i, cg)),
            pl.BlockSpec((y_ssm.shape[0], tm, y_ssm.shape[2]), lambda i: (0, i, 0)),
            pl.BlockSpec((tm, d_ssm), lambda i: (i, cs)),
            const(w_pool), const(pool_scale), const(w_glu), const(b_glu),
            pl.BlockSpec((1, dm, d), lambda i: (layer, 0, 0), pipeline_mode=pl.Buffered(1)),
            pl.BlockSpec((tm, d), lambda i: (i, 0)),
            pl.BlockSpec((1, d), lambda i: (0, 0)),
            pl.BlockSpec((1, d), lambda i: (0, 0)),
        ],
        out_specs=pl.BlockSpec((tm, d), lambda i: (i, 0)),
        out_shape=jax.ShapeDtypeStruct((s, d), F32),
        compiler_params=_params(("arbitrary",)),
        name="outproj",
    )(ya, rest, rest, rest, y_ssm, rest, w_pool, pool_scale, w_glu, b_glu, w_out, x, gate,
      final_g)


def kernel(x, c, norm_g, w_ada, b_ada, w_in, b_f, w_pool, pool_scale, lam_re, lam_im,
           ssm_b_re, ssm_b_im, ssm_c_re, ssm_c_im, ssm_d, log_dt, w_glu, b_glu, w_out,
           final_g):
    b, s, d = x.shape
    assert b == 1
    depth = w_in.shape[0]
    d_pool = pool_scale.shape[1]
    d_ssm = b_glu.shape[1]
    d_attn = N_HEADS * HEAD_DIM
    n_f = b_f.shape[1]
    assert n_f == N_HEADS and w_in.shape[2] == 4 * d_attn + n_f + 2 * d_pool + 2 * d_ssm

    w_all = w_in.astype(BF16)
    w_tail = w_all[:, :, 4 * d_attn + n_f:]
    w_f = jnp.pad(w_in[:, :, 4 * d_attn:4 * d_attn + n_f],
                  ((0, 0), (0, 0), (0, LANES - n_f))).astype(BF16)
    b_f_row = jnp.pad(b_f, ((0, 0), (0, LANES - n_f))).reshape(depth, 1, LANES)
    col_up = d_attn
    col_gp = col_up + d_pool
    col_us = col_gp + d_pool
    col_gs = col_us + d_ssm

    mod = _ada_mod(c, w_ada, b_ada).reshape(depth, 3, d)
    t_all, p_all, e_all, al_all = _ssm_prep(lam_re, lam_im, log_dt, ssm_b_re, ssm_b_im,
                                            ssm_c_re, ssm_c_im)
    w_pool_b = w_pool.astype(BF16)
    w_glu_b = w_glu.astype(BF16)
    w_out_b = w_out.astype(BF16)

    xs = x.reshape(s, d)
    for l in range(depth):
        qkv, rest, ft = _inproj(xs, norm_g[l].reshape(1, d), mod[l], w_all, w_tail, w_f[l],
                                b_f_row[l], layer=l, d_attn=d_attn)
        ya = _attention(qkv, ft, rest, d_attn=d_attn)
        y_ssm = _ssm(rest, t_all, p_all, e_all, al_all, ssm_d[l].reshape(1, d_ssm),
                     layer=l, col_u=col_us)
        xs = _outproj(ya, rest, y_ssm, w_pool_b[l], pool_scale[l].reshape(1, d_pool),
                      w_glu_b[l], b_glu[l].reshape(1, d_ssm), w_out_b, xs, mod[l, 2:3, :],
                      final_g.reshape(1, d), layer=l, final=(l == depth - 1),
                      col_up=col_up, col_gp=col_gp, col_gs=col_gs)
    return xs.reshape(b, s, d).astype(x.dtype)
```

```python
import functools
import math

import jax
import jax.numpy as jnp
from jax import lax
from jax.experimental import pallas as pl
from jax.experimental.pallas import tpu as pltpu

F32 = jnp.float32
BF16 = jnp.bfloat16

N_HEADS = 8
HEAD_DIM = 128
POOL_WINDOWS = (2, 4, 8, 16)
POOL_GROUP = 128
POOL_HALO = 16
SSM_GROUP = 16
SSM_STATE = 64
SSM_SLAB_GROUPS = 8
NORM_EPS = 1e-6
LANES = 128
VMEM_LIMIT = 56 * 1024 * 1024
PROJ_TN = 512


def _params(sem, vmem=VMEM_LIMIT):
    return pltpu.CompilerParams(dimension_semantics=sem, vmem_limit_bytes=vmem)


def _sigmoid(x):
    return 1.0 / (1.0 + jnp.exp(-x))


def _silu(x):
    return x * _sigmoid(x)


def _pick(n, pref):
    t = min(n, pref)
    while n % t:
        t //= 2
    return t


def _ada_kernel(c_ref, w_ref, b_ref, o_ref):
    ca = _silu(c_ref[...])
    o_ref[0] = jnp.sum(w_ref[0] * ca, axis=0, keepdims=True) + b_ref[0]


def _ada_mod(c, w_ada, b_ada):
    depth, d, n = w_ada.shape
    tn = _pick(n, 2048)
    return pl.pallas_call(
        _ada_kernel,
        grid=(depth, n // tn),
        in_specs=[
            pl.BlockSpec((d, 1), lambda l, j: (0, 0)),
            pl.BlockSpec((1, d, tn), lambda l, j: (l, 0, j)),
            pl.BlockSpec((1, 1, tn), lambda l, j: (l, 0, j)),
        ],
        out_specs=pl.BlockSpec((1, 1, tn), lambda l, j: (l, 0, j)),
        out_shape=jax.ShapeDtypeStruct((depth, 1, n), F32),
        compiler_params=_params(("arbitrary", "arbitrary")),
        name="ada_mod",
    )(c.reshape(d, 1), w_ada, b_ada.reshape(depth, 1, n))


def _inproj_kernel(x_ref, g_ref, mod_ref, wa_ref, wt_ref, wf_ref, bf_ref,
                   qkv_ref, rest_ref, ft_ref, h_ref, carry_ref, *, d_attn, q_scale):
    i = pl.program_id(0)
    tm = x_ref.shape[0]
    n_qkv = qkv_ref.shape[0] * HEAD_DIM
    n_a = wa_ref.shape[2]
    n = n_a + wt_ref.shape[2]
    tn = PROJ_TN

    def put_heads(c0, val):
        for hh in range(tn // HEAD_DIM):
            qkv_ref[c0 // HEAD_DIM + hh] = val[:, hh * HEAD_DIM:(hh + 1) * HEAD_DIM]

    x = x_ref[...]
    ms = jnp.mean(x * x, axis=-1, keepdims=True)
    shift = mod_ref[0:1, :]
    scale = mod_ref[1:2, :]
    h = (x * lax.rsqrt(ms + NORM_EPS) * g_ref[...]) * (1.0 + scale) + shift
    h_ref[...] = h.astype(BF16)

    for c0 in range(0, n, tn):
        w = wa_ref[0, :, c0:c0 + tn] if c0 < n_a else wt_ref[0, :, c0 - n_a:c0 - n_a + tn]
        proj = jnp.dot(h_ref[...], w, preferred_element_type=F32)
        if c0 < d_attn:
            put_heads(c0, (proj * q_scale).astype(BF16))
        elif c0 < n_qkv:
            put_heads(c0, proj.astype(BF16))
        else:
            rest_ref[:, c0 - n_qkv:c0 - n_qkv + tn] = proj

    f = jnp.dot(h_ref[...], wf_ref[...], preferred_element_type=F32) + bf_ref[...]
    logf = -(jnp.maximum(-f, 0.0) + jnp.log1p(jnp.exp(-jnp.abs(f))))
    row = lax.broadcasted_iota(jnp.int32, logf.shape, 0)
    cum = logf
    d = 1
    while d < tm:
        cum = cum + jnp.where(row >= d, pltpu.roll(cum, d, axis=0), 0.0)
        d *= 2

    @pl.when(i == 0)
    def _():
        carry_ref[...] = jnp.zeros_like(carry_ref)

    cum = cum + carry_ref[0:1, :]
    carry_ref[...] = jnp.broadcast_to(cum[tm - 1:tm, :], carry_ref.shape)
    cum_t = cum.T
    for hh in range(N_HEADS):
        ft_ref[hh] = cum_t[hh:hh + 1, :]


def _inproj(x, g, mod, w_all, w_tail, w_f, b_f_row, *, layer, d_attn):
    s, d = x.shape
    n_a = 4 * d_attn
    n = n_a + w_tail.shape[2]
    n_qkv = 3 * d_attn
    tm = _pick(s, 256)
    kern = functools.partial(_inproj_kernel, d_attn=d_attn,
                             q_scale=HEAD_DIM ** -0.5 * math.log2(math.e))
    once = pl.Buffered(1)
    return pl.pallas_call(
        kern,
        grid=(s // tm,),
        in_specs=[
            pl.BlockSpec((tm, d), lambda i: (i, 0)),
            pl.BlockSpec((1, d), lambda i: (0, 0)),
            pl.BlockSpec((3, d), lambda i: (0, 0)),
            pl.BlockSpec((1, d, n_a), lambda i: (layer, 0, 0), pipeline_mode=once),
            pl.BlockSpec((1, d, n - n_a), lambda i: (layer, 0, 0), pipeline_mode=once),
            pl.BlockSpec((d, LANES), lambda i: (0, 0), pipeline_mode=once),
            pl.BlockSpec((1, LANES), lambda i: (0, 0)),
        ],
        out_specs=[
            pl.BlockSpec((n_qkv // HEAD_DIM, tm, HEAD_DIM), lambda i: (0, i, 0)),
            pl.BlockSpec((tm, n - n_qkv), lambda i: (i, 0)),
            pl.BlockSpec((N_HEADS, 1, tm), lambda i: (0, 0, i)),
        ],
        out_shape=[
            jax.ShapeDtypeStruct((n_qkv // HEAD_DIM, s, HEAD_DIM), BF16),
            jax.ShapeDtypeStruct((s, n - n_qkv), F32),
            jax.ShapeDtypeStruct((N_HEADS, 1, s), F32),
        ],
        scratch_shapes=[pltpu.VMEM((tm, d), BF16), pltpu.VMEM((8, LANES), F32)],
        compiler_params=_params(("arbitrary",)),
        name="inproj",
    )(x, g, mod, w_all, w_tail, w_f, b_f_row)


AUG_TERMS = 3
ONES_ROWS = 16
Q_STRIP = 1024
FAST_UNROLL = 4
LOG2E = math.log2(math.e)


SAFE_EXP = 60.0
ZERO_EXP = -136.0
NORM_SLACK = 1.02


def _tile_lanes(row, n):
    return jnp.concatenate([row] * (n // LANES), axis=1)


def _attn_kernel(q_ref, k_ref, v_ref, ft_ref, g_ref, o_ref,
                 kaug_ref, vt_ref, base_ref, bend_ref, kall_ref, acc_ref, *, tq, tk):
    i = pl.program_id(1)
    seq = k_ref.shape[1]
    dh = HEAD_DIM
    sub = lax.broadcasted_iota(jnp.int32, (LANES, LANES), 0)
    lane_row = lax.broadcasted_iota(jnp.int32, (1, LANES), 1)
    ones_sq = jnp.ones((LANES, LANES), BF16)

    @pl.when(i == 0)
    def _():
        vt_ref[dh:, :] = jnp.ones((ONES_ROWS, seq), BF16)
        kall_ref[...] = jnp.zeros_like(kall_ref)
        bend_ref[...] = jnp.zeros_like(bend_ref)

        def fill(c, carry):
            c_off = pl.multiple_of(c * tk, tk)
            b_row = (-LOG2E) * ft_ref[0, :, pl.ds(c_off, tk)]
            base = jnp.broadcast_to(b_row[:, 0:1], (1, LANES))
            base_ref[pl.ds(c, 1), :] = base
            bend_ref[0:1, :] = jnp.where(
                lane_row == c, jnp.broadcast_to(b_row[:, tk - 1:tk], (1, LANES)),
                bend_ref[0:1, :])
            rel = b_row - _tile_lanes(base, tk)
            hi = rel.astype(BF16).astype(F32)
            mid = (rel - hi).astype(BF16).astype(F32)
            lo = (rel - hi - mid).astype(BF16).astype(F32)
            for bb in range(tk // LANES):
                off = pl.multiple_of(c_off + bb * LANES, LANES)
                cs = slice(bb * LANES, (bb + 1) * LANES)
                terms = jnp.where(sub == 0, hi[:, cs], jnp.where(
                    sub == 1, mid[:, cs], jnp.where(sub == 2, lo[:, cs], 0.0)))
                kb = k_ref[0, pl.ds(off, LANES), :]
                kaug_ref[pl.ds(off, LANES), 0:dh] = kb
                kaug_ref[pl.ds(off, LANES), dh:] = terms.T.astype(BF16)
                vt_ref[0:dh, pl.ds(off, LANES)] = (
                    v_ref[0, pl.ds(off, LANES), :].astype(F32).T.astype(BF16))
                kf = kb.astype(F32)
                n2 = jnp.dot((kf * kf).astype(BF16), ones_sq, preferred_element_type=F32)
                kall_ref[0:1, :] = jnp.maximum(kall_ref[0:1, :],
                                               jnp.max(n2, axis=0, keepdims=True))
            return carry
        lax.fori_loop(0, seq // tk, fill, 0)

    lane_q = lax.broadcasted_iota(jnp.int32, (tq, LANES), 1)
    q = q_ref[0]
    q_aug = jnp.concatenate(
        [q, jnp.where(lane_q < AUG_TERMS, 1.0, 0.0).astype(BF16)], axis=1)
    base_q = base_ref[pl.ds(i, 1), :]
    n_strip = tq // Q_STRIP

    def chunk(j):
        k_off = pl.multiple_of(j * tk, tk)
        kc = kaug_ref[pl.ds(k_off, tk), :]
        vc = vt_ref[:, pl.ds(k_off, tk)]
        delta = base_ref[pl.ds(j, 1), :] - base_q
        return kc, vc, delta

    def scores(kc, lo, hi):
        return lax.dot_general(kc, q_aug[lo:hi, :], (((1,), (1,)), ((), ())),
                               preferred_element_type=F32)

    def online_step(j, m, masked):
        kc, vc, delta = chunk(j)
        delta = _tile_lanes(delta, Q_STRIP)
        m_out = []
        for st in range(n_strip):
            lo, hi = st * Q_STRIP, (st + 1) * Q_STRIP
            s = scores(kc, lo, hi)
            if masked:
                key = lax.broadcasted_iota(jnp.int32, (tk, Q_STRIP), 0)
                qry = lax.broadcasted_iota(jnp.int32, (tk, Q_STRIP), 1) + lo
                s = jnp.where(key <= qry, s, -jnp.inf)
            m_old = m[st] - delta
            m_new = jnp.maximum(m_old, jnp.max(s, axis=0, keepdims=True))
            p = jnp.exp2(s - m_new).astype(BF16)
            corr = jnp.exp2(m_old - m_new)
            pv = jnp.dot(vc, p, preferred_element_type=F32)
            acc_ref[:, lo:hi] = acc_ref[:, lo:hi] * corr + pv
            m_out.append(m_new + delta)
        return tuple(m_out)

    q_off = pl.multiple_of(i * tq, tq)
    ones_row = jnp.ones((8, dh), BF16)
    row_sum = lambda a: lax.dot_general(ones_row, a.astype(BF16), (((1,), (1,)), ((), ())),
                                        preferred_element_type=F32)[0:1, :]
    qf = q.astype(F32)
    qk_self = row_sum(qf * kaug_ref[pl.ds(q_off, tq), 0:dh].astype(F32))
    rel_q = (-LOG2E) * ft_ref[0, :, pl.ds(q_off, tq)] - _tile_lanes(base_q, tq)
    m_row = qk_self + rel_q
    qk_bound = jnp.sqrt(row_sum(qf * qf) * _tile_lanes(kall_ref[0:1, :], tq)) * NORM_SLACK + 1.0
    slack_diag = jnp.max(qk_bound - qk_self)
    slack = jnp.max(qk_bound - m_row)
    live = jnp.logical_and(slack + (bend_ref[0:1, :] - base_q) >= ZERO_EXP, lane_row < i)
    n_live = jnp.sum(live.astype(jnp.int32))

    def one_pass(j):
        kc, vc, delta = chunk(j)
        ref = m_row - _tile_lanes(delta, tq)
        for st in range(n_strip):
            lo, hi = st * Q_STRIP, (st + 1) * Q_STRIP
            x = scores(kc, lo, hi) - ref[:, lo:hi]
            acc_ref[:, lo:hi] += jnp.dot(vc, jnp.exp2(x).astype(BF16),
                                         preferred_element_type=F32)

    def diag_pass():
        kc, vc, _ = chunk(i)
        half = tk // 2
        tri = (lax.broadcasted_iota(jnp.int32, (half, half), 0)
               <= lax.broadcasted_iota(jnp.int32, (half, half), 1))
        x = lax.dot_general(kc[0:half, :], q_aug, (((1,), (1,)), ((), ())),
                            preferred_element_type=F32) - m_row
        x = jnp.concatenate([jnp.where(tri, x[:, 0:half], -jnp.inf), x[:, half:]], axis=1)
        acc_ref[...] = jnp.dot(vc[:, 0:half], jnp.exp2(x).astype(BF16),
                               preferred_element_type=F32)
        x = lax.dot_general(kc[half:, :], q_aug[half:, :], (((1,), (1,)), ((), ())),
                            preferred_element_type=F32) - m_row[:, half:]
        acc_ref[:, half:] += jnp.dot(vc[:, half:], jnp.exp2(jnp.where(tri, x, -jnp.inf)).astype(BF16),
                                     preferred_element_type=F32)

    diag_pass()

    @pl.when(slack_diag <= SAFE_EXP)
    def _():
        def fast_group(jj, carry):
            for u in range(FAST_UNROLL):
                one_pass(i - 1 - u - FAST_UNROLL * jj)
            return carry
        n_group = n_live // FAST_UNROLL
        lax.fori_loop(0, n_group, fast_group, 0)

        def fast_single(jj, carry):
            one_pass(i - 1 - FAST_UNROLL * n_group - jj)
            return carry
        lax.fori_loop(0, n_live - FAST_UNROLL * n_group, fast_single, 0)

    @pl.when(slack_diag > SAFE_EXP)
    def _():
        acc_ref[...] = jnp.zeros_like(acc_ref)
        m0 = tuple(jnp.full((1, Q_STRIP), -jnp.inf, F32) for _ in range(n_strip))
        m_d = online_step(i, m0, True)
        lax.fori_loop(0, i, lambda jj, m: online_step(i - 1 - jj, m, False), m_d)

    y = (acc_ref[0:dh, :] / acc_ref[dh:dh + 1, :]).T
    o_ref[...] = (y * _silu(g_ref[...])).astype(o_ref.dtype)


def _attention(qkv, ft, rest, *, d_attn):
    s = qkv.shape[1]
    tq = _pick(s, 1024)
    nh = d_attn // HEAD_DIM
    kern = functools.partial(_attn_kernel, tq=tq, tk=tq)
    return pl.pallas_call(
        kern,
        grid=(nh, s // tq),
        in_specs=[
            pl.BlockSpec((1, tq, HEAD_DIM), lambda h, i: (h, i, 0)),
            pl.BlockSpec((1, s, HEAD_DIM), lambda h, i: (nh + h, 0, 0)),
            pl.BlockSpec((1, s, HEAD_DIM), lambda h, i: (2 * nh + h, 0, 0)),
            pl.BlockSpec((1, 1, s), lambda h, i: (h, 0, 0)),
            pl.BlockSpec((tq, HEAD_DIM), lambda h, i: (i, h)),
        ],
        out_specs=pl.BlockSpec((tq, HEAD_DIM), lambda h, i: (i, h)),
        out_shape=jax.ShapeDtypeStruct((s, d_attn), BF16),
        scratch_shapes=[
            pltpu.VMEM((s, 2 * HEAD_DIM), BF16),
            pltpu.VMEM((HEAD_DIM + ONES_ROWS, s), BF16),
            pltpu.VMEM((max(s // tq, 8), LANES), F32),
            pltpu.VMEM((8, LANES), F32),
            pltpu.VMEM((8, LANES), F32),
            pltpu.VMEM((HEAD_DIM + ONES_ROWS, tq), F32),
        ],
        compiler_params=_params(("arbitrary", "arbitrary")),
        name="fox_attention",
    )(qkv, qkv, qkv, ft, rest)


def _pool_branch(i, u, prev, gate, w_ref, scale):
    tp = u.shape[0]
    prev = jnp.where(i > 0, prev, 0.0)
    t1 = lax.broadcasted_iota(jnp.int32, (tp, POOL_GROUP), 0) + (i * tp + 1)
    outs = []
    for g, w in enumerate(POOL_WINDOWS):
        lo, hi = g * POOL_GROUP, (g + 1) * POOL_GROUP
        ug = u[:, lo:hi]
        ext = jnp.concatenate([prev[:, lo:hi], ug], axis=0)
        win = ext
        span = 1
        while span < w:
            win = win + pltpu.roll(win, span, axis=0)
            span *= 2
        win = win[POOL_HALO:]
        cnt = jnp.minimum(t1, w).astype(F32)
        pooled = win / cnt - ug
        mixed = jnp.dot(pooled.astype(BF16), w_ref[g], preferred_element_type=F32)
        outs.append(mixed)
    mixed = jnp.concatenate(outs, axis=1) * scale
    return (mixed * _silu(gate)).astype(BF16)


SSM_CHUNK = 8


def _discretise(lr, li, ldt):
    dt = jnp.exp(ldt)
    mag = jnp.exp(lr * dt)
    ab_re = mag * jnp.cos(li * dt)
    ab_im = mag * jnp.sin(li * dt)
    den = lr * lr + li * li
    nr = ab_re - 1.0
    ni = ab_im
    z_re = (nr * lr + ni * li) / den
    z_im = (ni * lr - nr * li) / den
    return ab_re, ab_im, z_re, z_im


def _powers(a_re, a_im, n):
    out = [(jnp.ones_like(a_re), jnp.zeros_like(a_im))]
    for _ in range(n):
        p_re, p_im = out[-1]
        out.append((p_re * a_re - p_im * a_im, p_re * a_im + p_im * a_re))
    return out


def _ssm_prep_kernel(lr_ref, li_ref, ldt_ref, lrc_ref, lic_ref, ldtc_ref,
                     br_ref, bi_ref, cr_ref, ci_ref, t_ref, p_ref, e_ref, al_ref):
    nl = SSM_CHUNK
    cw = br_ref.shape[2]
    a_re, a_im, z_re, z_im = _discretise(lr_ref[0, 0], li_ref[0, 0], ldt_ref[0, 0])
    pw = _powers(a_re, a_im, nl)
    al_ref[0, 0] = jnp.concatenate([pw[nl][0], pw[nl][1]], axis=0)
    br = br_ref[0, 0]
    bi = bi_ref[0, 0]
    bb_re = z_re * br - z_im * bi
    bb_im = z_re * bi + z_im * br
    bb = jnp.concatenate([bb_re, bb_im], axis=1)
    ac_re, ac_im, _, _ = _discretise(lrc_ref[0, 0], lic_ref[0, 0], ldtc_ref[0, 0])
    pwc = _powers(ac_re, ac_im, nl)
    cr = cr_ref[0, 0]
    ci = ci_ref[0, 0]
    ca = [jnp.concatenate([cr * q_re - ci * q_im, -(cr * q_im + ci * q_re)], axis=0)
          for q_re, q_im in pwc]
    kd_all = jnp.dot(bb, jnp.concatenate(ca[:nl], axis=1), preferred_element_type=F32,
                     precision=lax.Precision.HIGHEST).astype(BF16)
    kd = [kd_all[:, d * cw:(d + 1) * cw] for d in range(nl)]
    zero = jnp.zeros((cw, cw), BF16)
    for src in range(nl):
        for dst in range(nl):
            t_ref[0, 0, src * cw:(src + 1) * cw, dst * cw:(dst + 1) * cw] = (
                kd[dst - src] if dst >= src else zero)
        q_re, q_im = pw[nl - 1 - src]
        p_ref[0, 0, src * cw:(src + 1) * cw, :] = jnp.concatenate(
            [bb_re * q_re - bb_im * q_im, bb_re * q_im + bb_im * q_re], axis=1).astype(BF16)
        e_ref[0, 0, :, src * cw:(src + 1) * cw] = ca[src + 1].astype(BF16)


def _ssm_prep(lam_re, lam_im, log_dt, b_re, b_im, c_re, c_im):
    depth, ng, ns = lam_re.shape
    gc = b_re.shape[-1]
    n_slab = ng // SSM_SLAB_GROUPS
    eye = jnp.eye(SSM_SLAB_GROUPS, dtype=F32)
    sw = SSM_SLAB_GROUPS * ns

    def place_b(b):
        b = b.reshape(depth, n_slab, SSM_SLAB_GROUPS, ns, gc).transpose(0, 1, 2, 4, 3)
        return (b[:, :, :, :, None, :] * eye[None, None, :, None, :, None]).reshape(
            depth, n_slab, SSM_SLAB_GROUPS * gc, sw)

    def place_c(c):
        c = c.reshape(depth, n_slab, SSM_SLAB_GROUPS, gc, ns).transpose(0, 1, 2, 4, 3)
        return (c[:, :, :, :, None, :] * eye[None, None, :, None, :, None]).reshape(
            depth, n_slab, sw, SSM_SLAB_GROUPS * gc)

    cw = SSM_SLAB_GROUPS * gc
    nl = SSM_CHUNK
    ldt = jnp.broadcast_to(log_dt[:, :, None], (depth, ng, ns))
    row = lambda v: v.reshape(depth, n_slab, 1, sw)
    col = lambda v: jnp.broadcast_to(v.reshape(depth, n_slab, sw, 1), (depth, n_slab, sw, cw))
    blk = lambda *shape: pl.BlockSpec((1, 1) + shape, lambda l, k: (l, k, 0, 0))
    return pl.pallas_call(
        _ssm_prep_kernel,
        grid=(depth, n_slab),
        in_specs=[blk(1, sw)] * 3 + [blk(sw, cw)] * 3 + [blk(cw, sw)] * 2 + [blk(sw, cw)] * 2,
        out_specs=[blk(nl * cw, nl * cw), blk(nl * cw, 2 * sw), blk(2 * sw, nl * cw),
                   blk(2, sw)],
        out_shape=[
            jax.ShapeDtypeStruct((depth, n_slab, nl * cw, nl * cw), BF16),
            jax.ShapeDtypeStruct((depth, n_slab, nl * cw, 2 * sw), BF16),
            jax.ShapeDtypeStruct((depth, n_slab, 2 * sw, nl * cw), BF16),
            jax.ShapeDtypeStruct((depth, n_slab, 2, sw), F32),
        ],
        compiler_params=_params(("arbitrary", "arbitrary")),
        name="ssm_prep",
    )(row(lam_re), row(lam_im), row(ldt), col(lam_re), col(lam_im), col(ldt),
      place_b(b_re), place_b(b_im), place_c(c_re), place_c(c_im))


def _gelu_tanh(y):
    c = math.sqrt(2.0 / math.pi)
    return 0.5 * y * (1.0 + jnp.tanh(c * (y + 0.044715 * (y * y * y))))


SSM_SLABS_PER_STEP = 2


def _ssm_kernel(*refs):
    per = SSM_SLABS_PER_STEP
    u_refs = refs[:per]
    t_ref, p_ref, e_ref, al_ref, d_ref, y_ref, uc_ref, carry_ref = refs[per:]
    i = pl.program_id(1)
    nl = SSM_CHUNK
    ts, cw = u_refs[0].shape
    rows = ts // nl
    sw = al_ref.shape[3]

    @pl.when(i == 0)
    def _():
        carry_ref[...] = jnp.zeros_like(carry_ref)

    row = lax.broadcasted_iota(jnp.int32, (rows, sw), 0)
    first = row == 0
    vs, yts = [], []
    for k in range(per):
        for tau in range(nl):
            uc_ref[k, :, tau * cw:(tau + 1) * cw] = (
                u_refs[k][pl.ds(tau, rows, stride=nl), :].astype(BF16))
        vs.append(jnp.dot(uc_ref[k], p_ref[0, k], preferred_element_type=F32))
    for k in range(per):
        yts.append(jnp.dot(uc_ref[k], t_ref[0, k], preferred_element_type=F32))
    for k in range(per):
        u_ref = u_refs[k]
        xr = vs[k][:, :sw]
        xi = vs[k][:, sw:]
        a_re = al_ref[0, k, 0:1, :]
        a_im = al_ref[0, k, 1:2, :]
        c_re = carry_ref[k, 0:1, :]
        c_im = carry_ref[k, 1:2, :]
        xr = xr + jnp.where(first, a_re * c_re - a_im * c_im, 0.0)
        xi = xi + jnp.where(first, a_re * c_im + a_im * c_re, 0.0)
        q_re, q_im = a_re, a_im
        d = 1
        while d < rows:
            if d % 8:
                keep = row >= d
                sr = jnp.where(keep, pltpu.roll(xr, d, axis=0), 0.0)
                si = jnp.where(keep, pltpu.roll(xi, d, axis=0), 0.0)
                xr, xi = xr + (q_re * sr - q_im * si), xi + (q_re * si + q_im * sr)
            else:
                sr, si = xr[:rows - d], xi[:rows - d]
                xr, xi = (
                    jnp.concatenate([xr[:d], xr[d:] + (q_re * sr - q_im * si)], axis=0),
                    jnp.concatenate([xi[:d], xi[d:] + (q_re * si + q_im * sr)], axis=0))
            q_re, q_im = q_re * q_re - q_im * q_im, 2.0 * (q_re * q_im)
            d *= 2
        pr = jnp.where(first, c_re, pltpu.roll(xr, 1, axis=0))
        pi = jnp.where(first, c_im, pltpu.roll(xi, 1, axis=0))
        carry_ref[k, 0:1, :] = xr[rows - 1:rows, :]
        carry_ref[k, 1:2, :] = xi[rows - 1:rows, :]
        xp = jnp.concatenate([pr, pi], axis=1).astype(BF16)
        yc = yts[k] + jnp.dot(xp, e_ref[0, k], preferred_element_type=F32)
        for tau in range(nl):
            y_ref[k, pl.ds(tau, rows, stride=nl), :] = (
                yc[:, tau * cw:(tau + 1) * cw]
                + d_ref[:, k * cw:(k + 1) * cw] * u_ref[pl.ds(tau, rows, stride=nl), :])


def _ssm(rest, t_mat, p_mat, e_mat, a_l, d_row, *, layer, col_u):
    s = rest.shape[0]
    n_slab, cw = t_mat.shape[1], d_row.shape[1] // t_mat.shape[1]
    per = SSM_SLABS_PER_STEP
    ts = _pick(s, 4096)
    cu = col_u // cw
    sw = a_l.shape[3]
    mat = lambda m: pl.BlockSpec((1, per) + m.shape[2:], lambda k, i: (layer, k, 0, 0))
    u_spec = lambda n: pl.BlockSpec((ts, cw), lambda k, i: (i, cu + per * k + n))
    return pl.pallas_call(
        _ssm_kernel,
        grid=(n_slab // per, s // ts),
        in_specs=[u_spec(n) for n in range(per)] + [
            mat(t_mat), mat(p_mat), mat(e_mat), mat(a_l),
            pl.BlockSpec((1, per * cw), lambda k, i: (0, k)),
        ],
        out_specs=pl.BlockSpec((per, ts, cw), lambda k, i: (k, i, 0)),
        out_shape=jax.ShapeDtypeStruct((n_slab, s, cw), F32),
        scratch_shapes=[pltpu.VMEM((per, ts // SSM_CHUNK, SSM_CHUNK * cw), BF16),
                        pltpu.VMEM((per, 2, sw), F32)],
        compiler_params=_params(("arbitrary", "arbitrary")),
        name="ssm",
    )(*([rest] * per), t_mat, p_mat, e_mat, a_l, d_row)


def _glu_branch(y_raw, gate, wg_ref, bg):
    y = _gelu_tanh(y_raw)
    z = jnp.dot(y.astype(BF16), wg_ref[...], preferred_element_type=F32) + bg
    return (y * _sigmoid(z) * _silu(gate)).astype(BF16)


def _outproj_kernel(ya_ref, up_ref, prev_ref, gp_ref, ysr_ref, gs_ref, wp_ref, sc_ref,
                    wg_ref, bg_ref, w_ref, x_ref, mod_ref, fg_ref, o_ref, *, final):
    i = pl.program_id(0)
    da = ya_ref.shape[1]
    dp = up_ref.shape[1]
    d = x_ref.shape[1]
    tn = PROJ_TN
    ya = ya_ref[...]
    parts = [jnp.dot(ya, w_ref[0, 0:da, c0:c0 + tn], preferred_element_type=F32)
             for c0 in range(0, d, tn)]
    yp = _pool_branch(i, up_ref[...], prev_ref[...], gp_ref[...], wp_ref, sc_ref[...])
    y_raw = jnp.concatenate([ysr_ref[k] for k in range(ysr_ref.shape[0])], axis=1)
    ys = _glu_branch(y_raw, gs_ref[...], wg_ref, bg_ref[...])
    for n, c0 in enumerate(range(0, d, tn)):
        out = (parts[n]
               + jnp.dot(yp, w_ref[0, da:da + dp, c0:c0 + tn], preferred_element_type=F32)
               + jnp.dot(ys, w_ref[0, da + dp:, c0:c0 + tn], preferred_element_type=F32))
        o_ref[:, c0:c0 + tn] = x_ref[:, c0:c0 + tn] + mod_ref[:, c0:c0 + tn] * out
    if final:
        xn = o_ref[...]
        ms = jnp.mean(xn * xn, axis=-1, keepdims=True)
        o_ref[...] = xn * lax.rsqrt(ms + NORM_EPS) * fg_ref[...]


def _outproj(ya, rest, y_ssm, w_pool, pool_scale, w_glu, b_glu, w_out, x, gate, final_g,
             *, layer, final, col_up, col_gp, col_gs):
    s, d = x.shape
    dm = w_out.shape[1]
    d_pool = pool_scale.shape[1]
    d_ssm = b_glu.shape[1]
    tm = _pick(s, 512)
    halo_blocks = tm // POOL_HALO
    cu, cg, cs = col_up // d_pool, col_gp // d_pool, col_gs // d_ssm
    const = lambda a: pl.BlockSpec(a.shape, lambda i: (0,) * a.ndim)
    return pl.pallas_call(
        functools.partial(_outproj_kernel, final=final),
        grid=(s // tm,),
        in_specs=[
            pl.BlockSpec((tm, ya.shape[1]), lambda i: (i, 0)),
            pl.BlockSpec((tm, d_pool), lambda i: (i, cu)),
            pl.BlockSpec((POOL_HALO, d_pool),
                         lambda i: (jnp.maximum(i * halo_blocks - 1, 0), cu)),
            pl.BlockSpec((tm, d_pool), lambda i: (i, cg)),
            pl.BlockSpec((y_ssm.shape[0], tm, y_ssm.shape[2]), lambda i: (0, i, 0)),
            pl.BlockSpec((tm, d_ssm), lambda i: (i, cs)),
            const(w_pool), const(pool_scale), const(w_glu), const(b_glu),
            pl.BlockSpec((1, dm, d), lambda i: (layer, 0, 0), pipeline_mode=pl.Buffered(1)),
            pl.BlockSpec((tm, d), lambda i: (i, 0)),
            pl.BlockSpec((1, d), lambda i: (0, 0)),
            pl.BlockSpec((1, d), lambda i: (0, 0)),
        ],
        out_specs=pl.BlockSpec((tm, d), lambda i: (i, 0)),
        out_shape=jax.ShapeDtypeStruct((s, d), F32),
        compiler_params=_params(("arbitrary",)),
        name="outproj",
    )(ya, rest, rest, rest, y_ssm, rest, w_pool, pool_scale, w_glu, b_glu, w_out, x, gate,
      final_g)


def kernel(x, c, norm_g, w_ada, b_ada, w_in, b_f, w_pool, pool_scale, lam_re, lam_im,
           ssm_b_re, ssm_b_im, ssm_c_re, ssm_c_im, ssm_d, log_dt, w_glu, b_glu, w_out,
           final_g):
    b, s, d = x.shape
    assert b == 1
    depth = w_in.shape[0]
    d_pool = pool_scale.shape[1]
    d_ssm = b_glu.shape[1]
    d_attn = N_HEADS * HEAD_DIM
    n_f = b_f.shape[1]
    assert n_f == N_HEADS and w_in.shape[2] == 4 * d_attn + n_f + 2 * d_pool + 2 * d_ssm

    w_all = w_in.astype(BF16)
    w_tail = w_all[:, :, 4 * d_attn + n_f:]
    w_f = jnp.pad(w_in[:, :, 4 * d_attn:4 * d_attn + n_f],
                  ((0, 0), (0, 0), (0, LANES - n_f))).astype(BF16)
    b_f_row = jnp.pad(b_f, ((0, 0), (0, LANES - n_f))).reshape(depth, 1, LANES)
    col_up = d_attn
    col_gp = col_up + d_pool
    col_us = col_gp + d_pool
    col_gs = col_us + d_ssm

    mod = _ada_mod(c, w_ada, b_ada).reshape(depth, 3, d)
    t_all, p_all, e_all, al_all = _ssm_prep(lam_re, lam_im, log_dt, ssm_b_re, ssm_b_im,
                                            ssm_c_re, ssm_c_im)
    w_pool_b = w_pool.astype(BF16)
    w_glu_b = w_glu.astype(BF16)
    w_out_b = w_out.astype(BF16)

    xs = x.reshape(s, d)
    for l in range(depth):
        qkv, rest, ft = _inproj(xs, norm_g[l].reshape(1, d), mod[l], w_all, w_tail, w_f[l],
                                b_f_row[l], layer=l, d_attn=d_attn)
        ya = _attention(qkv, ft, rest, d_attn=d_attn)
        y_ssm = _ssm(rest, t_all, p_all, e_all, al_all, ssm_d[l].reshape(1, d_ssm),
                     layer=l, col_u=col_us)
        xs = _outproj(ya, rest, y_ssm, w_pool_b[l], pool_scale[l].reshape(1, d_pool),
                      w_glu_b[l], b_glu[l].reshape(1, d_ssm), w_out_b, xs, mod[l, 2:3, :],
                      final_g.reshape(1, d), layer=l, final=(l == depth - 1),
                      col_up=col_up, col_gp=col_gp, col_gs=col_gs)
    return xs.reshape(b, s, d).astype(x.dtype)
```

```python
import functools
import math

import jax
import jax.numpy as jnp
from jax import lax
from jax.experimental import pallas as pl
from jax.experimental.pallas import tpu as pltpu

F32 = jnp.float32
BF16 = jnp.bfloat16

N_HEADS = 8
HEAD_DIM = 128
POOL_WINDOWS = (2, 4, 8, 16)
POOL_GROUP = 128
POOL_HALO = 16
SSM_GROUP = 16
SSM_STATE = 64
SSM_SLAB_GROUPS = 8
NORM_EPS = 1e-6
LANES = 128
VMEM_LIMIT = 56 * 1024 * 1024
PROJ_TN = 512


def _params(sem, vmem=VMEM_LIMIT):
    return pltpu.CompilerParams(dimension_semantics=sem, vmem_limit_bytes=vmem)


def _sigmoid(x):
    return 1.0 / (1.0 + jnp.exp(-x))


def _silu(x):
    return x * _sigmoid(x)


def _pick(n, pref):
    t = min(n, pref)
    while n % t:
        t //= 2
    return t


def _ada_kernel(c_ref, w_ref, b_ref, o_ref):
    ca = _silu(c_ref[...])
    o_ref[0] = jnp.sum(w_ref[0] * ca, axis=0, keepdims=True) + b_ref[0]


def _ada_mod(c, w_ada, b_ada):
    depth, d, n = w_ada.shape
    tn = _pick(n, 2048)
    return pl.pallas_call(
        _ada_kernel,
        grid=(depth, n // tn),
        in_specs=[
            pl.BlockSpec((d, 1), lambda l, j: (0, 0)),
            pl.BlockSpec((1, d, tn), lambda l, j: (l, 0, j)),
            pl.BlockSpec((1, 1, tn), lambda l, j: (l, 0, j)),
        ],
        out_specs=pl.BlockSpec((1, 1, tn), lambda l, j: (l, 0, j)),
        out_shape=jax.ShapeDtypeStruct((depth, 1, n), F32),
        compiler_params=_params(("arbitrary", "arbitrary")),
        name="ada_mod",
    )(c.reshape(d, 1), w_ada, b_ada.reshape(depth, 1, n))


def _inproj_kernel(x_ref, g_ref, mod_ref, wa_ref, wt_ref, wf_ref, bf_ref,
                   qkv_ref, rest_ref, ft_ref, h_ref, carry_ref, *, d_attn, q_scale):
    i = pl.program_id(0)
    tm = x_ref.shape[0]
    n_qkv = qkv_ref.shape[0] * HEAD_DIM
    n_a = wa_ref.shape[2]
    n = n_a + wt_ref.shape[2]
    tn = PROJ_TN

    def put_heads(c0, val):
        for hh in range(tn // HEAD_DIM):
            qkv_ref[c0 // HEAD_DIM + hh] = val[:, hh * HEAD_DIM:(hh + 1) * HEAD_DIM]

    x = x_ref[...]
    ms = jnp.mean(x * x, axis=-1, keepdims=True)
    shift = mod_ref[0:1, :]
    scale = mod_ref[1:2, :]
    h = (x * lax.rsqrt(ms + NORM_EPS) * g_ref[...]) * (1.0 + scale) + shift
    h_ref[...] = h.astype(BF16)

    for c0 in range(0, n, tn):
        w = wa_ref[0, :, c0:c0 + tn] if c0 < n_a else wt_ref[0, :, c0 - n_a:c0 - n_a + tn]
        proj = jnp.dot(h_ref[...], w, preferred_element_type=F32)
        if c0 < d_attn:
            put_heads(c0, (proj * q_scale).astype(BF16))
        elif c0 < n_qkv:
            put_heads(c0, proj.astype(BF16))
        else:
            rest_ref[:, c0 - n_qkv:c0 - n_qkv + tn] = proj

    f = jnp.dot(h_ref[...], wf_ref[...], preferred_element_type=F32) + bf_ref[...]
    logf = -(jnp.maximum(-f, 0.0) + jnp.log1p(jnp.exp(-jnp.abs(f))))
    row = lax.broadcasted_iota(jnp.int32, logf.shape, 0)
    cum = logf
    d = 1
    while d < tm:
        cum = cum + jnp.where(row >= d, pltpu.roll(cum, d, axis=0), 0.0)
        d *= 2

    @pl.when(i == 0)
    def _():
        carry_ref[...] = jnp.zeros_like(carry_ref)

    cum = cum + carry_ref[0:1, :]
    carry_ref[...] = jnp.broadcast_to(cum[tm - 1:tm, :], carry_ref.shape)
    cum_t = cum.T
    for hh in range(N_HEADS):
        ft_ref[hh] = cum_t[hh:hh + 1, :]


def _inproj(x, g, mod, w_all, w_tail, w_f, b_f_row, *, layer, d_attn):
    s, d = x.shape
    n_a = 4 * d_attn
    n = n_a + w_tail.shape[2]
    n_qkv = 3 * d_attn
    tm = _pick(s, 256)
    kern = functools.partial(_inproj_kernel, d_attn=d_attn,
                             q_scale=HEAD_DIM ** -0.5 * math.log2(math.e))
    once = pl.Buffered(1)
    return pl.pallas_call(
        kern,
        grid=(s // tm,),
        in_specs=[
            pl.BlockSpec((tm, d), lambda i: (i, 0)),
            pl.BlockSpec((1, d), lambda i: (0, 0)),
            pl.BlockSpec((3, d), lambda i: (0, 0)),
            pl.BlockSpec((1, d, n_a), lambda i: (layer, 0, 0), pipeline_mode=once),
            pl.BlockSpec((1, d, n - n_a), lambda i: (layer, 0, 0), pipeline_mode=once),
            pl.BlockSpec((d, LANES), lambda i: (0, 0), pipeline_mode=once),
            pl.BlockSpec((1, LANES), lambda i: (0, 0)),
        ],
        out_specs=[
            pl.BlockSpec((n_qkv // HEAD_DIM, tm, HEAD_DIM), lambda i: (0, i, 0)),
            pl.BlockSpec((tm, n - n_qkv), lambda i: (i, 0)),
            pl.BlockSpec((N_HEADS, 1, tm), lambda i: (0, 0, i)),
        ],
        out_shape=[
            jax.ShapeDtypeStruct((n_qkv // HEAD_DIM, s, HEAD_DIM), BF16),
            jax.ShapeDtypeStruct((s, n - n_qkv), F32),
            jax.ShapeDtypeStruct((N_HEADS, 1, s), F32),
        ],
        scratch_shapes=[pltpu.VMEM((tm, d), BF16), pltpu.VMEM((8, LANES), F32)],
        compiler_params=_params(("arbitrary",)),
        name="inproj",
    )(x, g, mod, w_all, w_tail, w_f, b_f_row)


AUG_TERMS = 3
ONES_ROWS = 16
Q_STRIP = 1024
FAST_UNROLL = 4
LOG2E = math.log2(math.e)


SAFE_EXP = 60.0
ZERO_EXP = -136.0
NORM_SLACK = 1.02


def _tile_lanes(row, n):
    return jnp.concatenate([row] * (n // LANES), axis=1)


def _attn_kernel(q_ref, k_ref, v_ref, ft_ref, g_ref, o_ref,
                 kaug_ref, vt_ref, base_ref, bend_ref, kall_ref, acc_ref, *, tq, tk):
    i = pl.program_id(1)
    seq = k_ref.shape[1]
    dh = HEAD_DIM
    sub = lax.broadcasted_iota(jnp.int32, (LANES, LANES), 0)
    lane_row = lax.broadcasted_iota(jnp.int32, (1, LANES), 1)
    ones_sq = jnp.ones((LANES, LANES), BF16)

    @pl.when(i == 0)
    def _():
        vt_ref[dh:, :] = jnp.ones((ONES_ROWS, seq), BF16)
        kall_ref[...] = jnp.zeros_like(kall_ref)
        bend_ref[...] = jnp.zeros_like(bend_ref)

        def fill(c, carry):
            c_off = pl.multiple_of(c * tk, tk)
            b_row = (-LOG2E) * ft_ref[0, :, pl.ds(c_off, tk)]
            base = jnp.broadcast_to(b_row[:, 0:1], (1, LANES))
            base_ref[pl.ds(c, 1), :] = base
            bend_ref[0:1, :] = jnp.where(
                lane_row == c, jnp.broadcast_to(b_row[:, tk - 1:tk], (1, LANES)),
                bend_ref[0:1, :])
            rel = b_row - _tile_lanes(base, tk)
            hi = rel.astype(BF16).astype(F32)
            mid = (rel - hi).astype(BF16).astype(F32)
            lo = (rel - hi - mid).astype(BF16).astype(F32)
            for bb in range(tk // LANES):
                off = pl.multiple_of(c_off + bb * LANES, LANES)
                cs = slice(bb * LANES, (bb + 1) * LANES)
                terms = jnp.where(sub == 0, hi[:, cs], jnp.where(
                    sub == 1, mid[:, cs], jnp.where(sub == 2, lo[:, cs], 0.0)))
                kb = k_ref[0, pl.ds(off, LANES), :]
                kaug_ref[pl.ds(off, LANES), 0:dh] = kb
                kaug_ref[pl.ds(off, LANES), dh:] = terms.T.astype(BF16)
                vt_ref[0:dh, pl.ds(off, LANES)] = (
                    v_ref[0, pl.ds(off, LANES), :].astype(F32).T.astype(BF16))
                kf = kb.astype(F32)
                n2 = jnp.dot((kf * kf).astype(BF16), ones_sq, preferred_element_type=F32)
                kall_ref[0:1, :] = jnp.maximum(kall_ref[0:1, :],
                                               jnp.max(n2, axis=0, keepdims=True))
            return carry
        lax.fori_loop(0, seq // tk, fill, 0)

    lane_q = lax.broadcasted_iota(jnp.int32, (tq, LANES), 1)
    q = q_ref[0]
    q_aug = jnp.concatenate(
        [q, jnp.where(lane_q < AUG_TERMS, 1.0, 0.0).astype(BF16)], axis=1)
    base_q = base_ref[pl.ds(i, 1), :]
    n_strip = tq // Q_STRIP

    def chunk(j):
        k_off = pl.multiple_of(j * tk, tk)
        kc = kaug_ref[pl.ds(k_off, tk), :]
        vc = vt_ref[:, pl.ds(k_off, tk)]
        delta = base_ref[pl.ds(j, 1), :] - base_q
        return kc, vc, delta

    def scores(kc, lo, hi):
        return lax.dot_general(kc, q_aug[lo:hi, :], (((1,), (1,)), ((), ())),
                               preferred_element_type=F32)

    def online_step(j, m, masked):
        kc, vc, delta = chunk(j)
        delta = _tile_lanes(delta, Q_STRIP)
        m_out = []
        for st in range(n_strip):
            lo, hi = st * Q_STRIP, (st + 1) * Q_STRIP
            s = scores(kc, lo, hi)
            if masked:
                key = lax.broadcasted_iota(jnp.int32, (tk, Q_STRIP), 0)
                qry = lax.broadcasted_iota(jnp.int32, (tk, Q_STRIP), 1) + lo
                s = jnp.where(key <= qry, s, -jnp.inf)
            m_old = m[st] - delta
            m_new = jnp.maximum(m_old, jnp.max(s, axis=0, keepdims=True))
            p = jnp.exp2(s - m_new).astype(BF16)
            corr = jnp.exp2(m_old - m_new)
            pv = jnp.dot(vc, p, preferred_element_type=F32)
            acc_ref[:, lo:hi] = acc_ref[:, lo:hi] * corr + pv
            m_out.append(m_new + delta)
        return tuple(m_out)

    q_off = pl.multiple_of(i * tq, tq)
    ones_row = jnp.ones((8, dh), BF16)
    row_sum = lambda a: lax.dot_general(ones_row, a.astype(BF16), (((1,), (1,)), ((), ())),
                                        preferred_element_type=F32)[0:1, :]
    qf = q.astype(F32)
    qk_self = row_sum(qf * kaug_ref[pl.ds(q_off, tq), 0:dh].astype(F32))
    rel_q = (-LOG2E) * ft_ref[0, :, pl.ds(q_off, tq)] - _tile_lanes(base_q, tq)
    m_row = qk_self + rel_q
    qk_bound = jnp.sqrt(row_sum(qf * qf) * _tile_lanes(kall_ref[0:1, :], tq)) * NORM_SLACK + 1.0
    slack_diag = jnp.max(qk_bound - qk_self)
    slack = jnp.max(qk_bound - m_row)
    live = jnp.logical_and(slack + (bend_ref[0:1, :] - base_q) >= ZERO_EXP, lane_row < i)
    n_live = jnp.sum(live.astype(jnp.int32))

    def one_pass(j):
        kc, vc, delta = chunk(j)
        ref = m_row - _tile_lanes(delta, tq)
        for st in range(n_strip):
            lo, hi = st * Q_STRIP, (st + 1) * Q_STRIP
            x = scores(kc, lo, hi) - ref[:, lo:hi]
            acc_ref[:, lo:hi] += jnp.dot(vc, jnp.exp2(x).astype(BF16),
                                         preferred_element_type=F32)

    def diag_pass():
        kc, vc, _ = chunk(i)
        half = tk // 2
        tri = (lax.broadcasted_iota(jnp.int32, (half, half), 0)
               <= lax.broadcasted_iota(jnp.int32, (half, half), 1))
        x = lax.dot_general(kc[0:half, :], q_aug, (((1,), (1,)), ((), ())),
                            preferred_element_type=F32) - m_row
        x = jnp.concatenate([jnp.where(tri, x[:, 0:half], -jnp.inf), x[:, half:]], axis=1)
        acc_ref[...] = jnp.dot(vc[:, 0:half], jnp.exp2(x).astype(BF16),
                               preferred_element_type=F32)
        x = lax.dot_general(kc[half:, :], q_aug[half:, :], (((1,), (1,)), ((), ())),
                            preferred_element_type=F32) - m_row[:, half:]
        acc_ref[:, half:] += jnp.dot(vc[:, half:], jnp.exp2(jnp.where(tri, x, -jnp.inf)).astype(BF16),
                                     preferred_element_type=F32)

    diag_pass()

    @pl.when(slack_diag <= SAFE_EXP)
    def _():
        def fast_group(jj, carry):
            for u in range(FAST_UNROLL):
                one_pass(i - 1 - u - FAST_UNROLL * jj)
            return carry
        n_group = n_live // FAST_UNROLL
        lax.fori_loop(0, n_group, fast_group, 0)

        def fast_single(jj, carry):
            one_pass(i - 1 - FAST_UNROLL * n_group - jj)
            return carry
        lax.fori_loop(0, n_live - FAST_UNROLL * n_group, fast_single, 0)

    @pl.when(slack_diag > SAFE_EXP)
    def _():
        acc_ref[...] = jnp.zeros_like(acc_ref)
        m0 = tuple(jnp.full((1, Q_STRIP), -jnp.inf, F32) for _ in range(n_strip))
        m_d = online_step(i, m0, True)
        lax.fori_loop(0, i, lambda jj, m: online_step(i - 1 - jj, m, False), m_d)

    y = (acc_ref[0:dh, :] / acc_ref[dh:dh + 1, :]).T
    o_ref[...] = (y * _silu(g_ref[...])).astype(o_ref.dtype)


def _attention(qkv, ft, rest, *, d_attn):
    s = qkv.shape[1]
    tq = _pick(s, 1024)
    nh = d_attn // HEAD_DIM
    kern = functools.partial(_attn_kernel, tq=tq, tk=tq)
    return pl.pallas_call(
        kern,
        grid=(nh, s // tq),
        in_specs=[
            pl.BlockSpec((1, tq, HEAD_DIM), lambda h, i: (h, i, 0)),
            pl.BlockSpec((1, s, HEAD_DIM), lambda h, i: (nh + h, 0, 0)),
            pl.BlockSpec((1, s, HEAD_DIM), lambda h, i: (2 * nh + h, 0, 0)),
            pl.BlockSpec((1, 1, s), lambda h, i: (h, 0, 0)),
            pl.BlockSpec((tq, HEAD_DIM), lambda h, i: (i, h)),
        ],
        out_specs=pl.BlockSpec((tq, HEAD_DIM), lambda h, i: (i, h)),
        out_shape=jax.ShapeDtypeStruct((s, d_attn), BF16),
        scratch_shapes=[
            pltpu.VMEM((s, 2 * HEAD_DIM), BF16),
            pltpu.VMEM((HEAD_DIM + ONES_ROWS, s), BF16),
            pltpu.VMEM((max(s // tq, 8), LANES), F32),
            pltpu.VMEM((8, LANES), F32),
            pltpu.VMEM((8, LANES), F32),
            pltpu.VMEM((HEAD_DIM + ONES_ROWS, tq), F32),
        ],
        compiler_params=_params(("arbitrary", "arbitrary")),
        name="fox_attention",
    )(qkv, qkv, qkv, ft, rest)


def _pool_branch(i, u, prev, gate, w_ref, scale):
    tp = u.shape[0]
    prev = jnp.where(i > 0, prev, 0.0)
    t1 = lax.broadcasted_iota(jnp.int32, (tp, POOL_GROUP), 0) + (i * tp + 1)
    outs = []
    for g, w in enumerate(POOL_WINDOWS):
        lo, hi = g * POOL_GROUP, (g + 1) * POOL_GROUP
        ug = u[:, lo:hi]
        ext = jnp.concatenate([prev[:, lo:hi], ug], axis=0)
        win = ext
        span = 1
        while span < w:
            win = win + pltpu.roll(win, span, axis=0)
            span *= 2
        win = win[POOL_HALO:]
        cnt = jnp.minimum(t1, w).astype(F32)
        pooled = win / cnt - ug
        mixed = jnp.dot(pooled.astype(BF16), w_ref[g], preferred_element_type=F32)
        outs.append(mixed)
    mixed = jnp.concatenate(outs, axis=1) * scale
    return (mixed * _silu(gate)).astype(BF16)


SSM_CHUNK = 8


def _discretise(lr, li, ldt):
    dt = jnp.exp(ldt)
    mag = jnp.exp(lr * dt)
    ab_re = mag * jnp.cos(li * dt)
    ab_im = mag * jnp.sin(li * dt)
    den = lr * lr + li * li
    nr = ab_re - 1.0
    ni = ab_im
    z_re = (nr * lr + ni * li) / den
    z_im = (ni * lr - nr * li) / den
    return ab_re, ab_im, z_re, z_im


def _powers(a_re, a_im, n):
    out = [(jnp.ones_like(a_re), jnp.zeros_like(a_im))]
    for _ in range(n):
        p_re, p_im = out[-1]
        out.append((p_re * a_re - p_im * a_im, p_re * a_im + p_im * a_re))
    return out


def _ssm_prep_kernel(lr_ref, li_ref, ldt_ref, lrc_ref, lic_ref, ldtc_ref,
                     br_ref, bi_ref, cr_ref, ci_ref, t_ref, p_ref, e_ref, al_ref):
    nl = SSM_CHUNK
    cw = br_ref.shape[2]
    a_re, a_im, z_re, z_im = _discretise(lr_ref[0, 0], li_ref[0, 0], ldt_ref[0, 0])
    pw = _powers(a_re, a_im, nl)
    al_ref[0, 0] = jnp.concatenate([pw[nl][0], pw[nl][1]], axis=0)
    br = br_ref[0, 0]
    bi = bi_ref[0, 0]
    bb_re = z_re * br - z_im * bi
    bb_im = z_re * bi + z_im * br
    bb = jnp.concatenate([bb_re, bb_im], axis=1)
    ac_re, ac_im, _, _ = _discretise(lrc_ref[0, 0], lic_ref[0, 0], ldtc_ref[0, 0])
    pwc = _powers(ac_re, ac_im, nl)
    cr = cr_ref[0, 0]
    ci = ci_ref[0, 0]
    ca = [jnp.concatenate([cr * q_re - ci * q_im, -(cr * q_im + ci * q_re)], axis=0)
          for q_re, q_im in pwc]
    kd_all = jnp.dot(bb, jnp.concatenate(ca[:nl], axis=1), preferred_element_type=F32,
                     precision=lax.Precision.HIGHEST).astype(BF16)
    kd = [kd_all[:, d * cw:(d + 1) * cw] for d in range(nl)]
    zero = jnp.zeros((cw, cw), BF16)
    for src in range(nl):
        for dst in range(nl):
            t_ref[0, 0, src * cw:(src + 1) * cw, dst * cw:(dst + 1) * cw] = (
                kd[dst - src] if dst >= src else zero)
        q_re, q_im = pw[nl - 1 - src]
        p_ref[0, 0, src * cw:(src + 1) * cw, :] = jnp.concatenate(
            [bb_re * q_re - bb_im * q_im, bb_re * q_im + bb_im * q_re], axis=1).astype(BF16)
        e_ref[0, 0, :, src * cw:(src + 1) * cw] = ca[src + 1].astype(BF16)


def _ssm_prep(lam_re, lam_im, log_dt, b_re, b_im, c_re, c_im):
    depth, ng, ns = lam_re.shape
    gc = b_re.shape[-1]
    n_slab = ng // SSM_SLAB_GROUPS
    eye = jnp.eye(SSM_SLAB_GROUPS, dtype=F32)
    sw = SSM_SLAB_GROUPS * ns

    def place_b(b):
        b = b.reshape(depth, n_slab, SSM_SLAB_GROUPS, ns, gc).transpose(0, 1, 2, 4, 3)
        return (b[:, :, :, :, None, :] * eye[None, None, :, None, :, None]).reshape(
            depth, n_slab, SSM_SLAB_GROUPS * gc, sw)

    def place_c(c):
        c = c.reshape(depth, n_slab, SSM_SLAB_GROUPS, gc, ns).transpose(0, 1, 2, 4, 3)
        return (c[:, :, :, :, None, :] * eye[None, None, :, None, :, None]).reshape(
            depth, n_slab, sw, SSM_SLAB_GROUPS * gc)

    cw = SSM_SLAB_GROUPS * gc
    nl = SSM_CHUNK
    ldt = jnp.broadcast_to(log_dt[:, :, None], (depth, ng, ns))
    row = lambda v: v.reshape(depth, n_slab, 1, sw)
    col = lambda v: jnp.broadcast_to(v.reshape(depth, n_slab, sw, 1), (depth, n_slab, sw, cw))
    blk = lambda *shape: pl.BlockSpec((1, 1) + shape, lambda l, k: (l, k, 0, 0))
    return pl.pallas_call(
        _ssm_prep_kernel,
        grid=(depth, n_slab),
        in_specs=[blk(1, sw)] * 3 + [blk(sw, cw)] * 3 + [blk(cw, sw)] * 2 + [blk(sw, cw)] * 2,
        out_specs=[blk(nl * cw, nl * cw), blk(nl * cw, 2 * sw), blk(2 * sw, nl * cw),
                   blk(2, sw)],
        out_shape=[
            jax.ShapeDtypeStruct((depth, n_slab, nl * cw, nl * cw), BF16),
            jax.ShapeDtypeStruct((depth, n_slab, nl * cw, 2 * sw), BF16),
            jax.ShapeDtypeStruct((depth, n_slab, 2 * sw, nl * cw), BF16),
            jax.ShapeDtypeStruct((depth, n_slab, 2, sw), F32),
        ],
        compiler_params=_params(("arbitrary", "arbitrary")),
        name="ssm_prep",
    )(row(lam_re), row(lam_im), row(ldt), col(lam_re), col(lam_im), col(ldt),
      place_b(b_re), place_b(b_im), place_c(c_re), place_c(c_im))


def _gelu_tanh(y):
    c = math.sqrt(2.0 / math.pi)
    return 0.5 * y * (1.0 + jnp.tanh(c * (y + 0.044715 * (y * y * y))))


SSM_SLABS_PER_STEP = 2


def _ssm_kernel(*refs):
    per = SSM_SLABS_PER_STEP
    u_refs = refs[:per]
    t_ref, p_ref, e_ref, al_ref, d_ref, y_ref, uc_ref, carry_ref = refs[per:]
    i = pl.program_id(1)
    nl = SSM_CHUNK
    ts, cw = u_refs[0].shape
    rows = ts // nl
    sw = al_ref.shape[3]

    @pl.when(i == 0)
    def _():
        carry_ref[...] = jnp.zeros_like(carry_ref)

    row = lax.broadcasted_iota(jnp.int32, (rows, sw), 0)
    first = row == 0
    vs, yts = [], []
    for k in range(per):
        for tau in range(nl):
            uc_ref[k, :, tau * cw:(tau + 1) * cw] = (
                u_refs[k][pl.ds(tau, rows, stride=nl), :].astype(BF16))
        vs.append(jnp.dot(uc_ref[k], p_ref[0, k], preferred_element_type=F32))
    for k in range(per):
        yts.append(jnp.dot(uc_ref[k], t_ref[0, k], preferred_element_type=F32))
    for k in range(per):
        u_ref = u_refs[k]
        xr = vs[k][:, :sw]
        xi = vs[k][:, sw:]
        a_re = al_ref[0, k, 0:1, :]
        a_im = al_ref[0, k, 1:2, :]
        c_re = carry_ref[k, 0:1, :]
        c_im = carry_ref[k, 1:2, :]
        xr = xr + jnp.where(first, a_re * c_re - a_im * c_im, 0.0)
        xi = xi + jnp.where(first, a_re * c_im + a_im * c_re, 0.0)
        q_re, q_im = a_re, a_im
        d = 1
        while d < rows:
            if d % 8:
                keep = row >= d
                sr = jnp.where(keep, pltpu.roll(xr, d, axis=0), 0.0)
                si = jnp.where(keep, pltpu.roll(xi, d, axis=0), 0.0)
                xr, xi = xr + (q_re * sr - q_im * si), xi + (q_re * si + q_im * sr)
            else:
                sr, si = xr[:rows - d], xi[:rows - d]
                xr, xi = (
                    jnp.concatenate([xr[:d], xr[d:] + (q_re * sr - q_im * si)], axis=0),
                    jnp.concatenate([xi[:d], xi[d:] + (q_re * si + q_im * sr)], axis=0))
            q_re, q_im = q_re * q_re - q_im * q_im, 2.0 * (q_re * q_im)
            d *= 2
        pr = jnp.where(first, c_re, pltpu.roll(xr, 1, axis=0))
        pi = jnp.where(first, c_im, pltpu.roll(xi, 1, axis=0))
        carry_ref[k, 0:1, :] = xr[rows - 1:rows, :]
        carry_ref[k, 1:2, :] = xi[rows - 1:rows, :]
        xp = jnp.concatenate([pr, pi], axis=1).astype(BF16)
        yc = yts[k] + jnp.dot(xp, e_ref[0, k], preferred_element_type=F32)
        for tau in range(nl):
            y_ref[k, pl.ds(tau, rows, stride=nl), :] = (
                yc[:, tau * cw:(tau + 1) * cw]
                + d_ref[:, k * cw:(k + 1) * cw] * u_ref[pl.ds(tau, rows, stride=nl), :])


def _ssm(rest, t_mat, p_mat, e_mat, a_l, d_row, *, layer, col_u):
    s = rest.shape[0]
    n_slab, cw = t_mat.shape[1], d_row.shape[1] // t_mat.shape[1]
    per = SSM_SLABS_PER_STEP
    ts = _pick(s, 4096)
    cu = col_u // cw
    sw = a_l.shape[3]
    mat = lambda m: pl.BlockSpec((1, per) + m.shape[2:], lambda k, i: (layer, k, 0, 0))
    u_spec = lambda n: pl.BlockSpec((ts, cw), lambda k, i: (i, cu + per * k + n))
    return pl.pallas_call(
        _ssm_kernel,
        grid=(n_slab // per, s // ts),
        in_specs=[u_spec(n) for n in range(per)] + [
            mat(t_mat), mat(p_mat), mat(e_mat), mat(a_l),
            pl.BlockSpec((1, per * cw), lambda k, i: (0, k)),
        ],
        out_specs=pl.BlockSpec((per, ts, cw), lambda k, i: (k, i, 0)),
        out_shape=jax.ShapeDtypeStruct((n_slab, s, cw), F32),
        scratch_shapes=[pltpu.VMEM((per, ts // SSM_CHUNK, SSM_CHUNK * cw), BF16),
                        pltpu.VMEM((per, 2, sw), F32)],
        compiler_params=_params(("arbitrary", "arbitrary")),
        name="ssm",
    )(*([rest] * per), t_mat, p_mat, e_mat, a_l, d_row)


def _glu_branch(y_raw, gate, wg_ref, bg):
    y = _gelu_tanh(y_raw)
    z = jnp.dot(y.astype(BF16), wg_ref[...], preferred_element_type=F32) + bg
    return (y * _sigmoid(z) * _silu(gate)).astype(BF16)


def _outproj_kernel(ya_ref, up_ref, prev_ref, gp_ref, ysr_ref, gs_ref, wp_ref, sc_ref,
                    wg_ref, bg_ref, w_ref, x_ref, mod_ref, fg_ref, o_ref, *, final):
    i = pl.program_id(0)
    da = ya_ref.shape[1]
    dp = up_ref.shape[1]
    d = x_ref.shape[1]
    tn = PROJ_TN
    ya = ya_ref[...]
    parts = [jnp.dot(ya, w_ref[0, 0:da, c0:c0 + tn], preferred_element_type=F32)
             for c0 in range(0, d, tn)]
    yp = _pool_branch(i, up_ref[...], prev_ref[...], gp_ref[...], wp_ref, sc_ref[...])
    y_raw = jnp.concatenate([ysr_ref[k] for k in range(ysr_ref.shape[0])], axis=1)
    ys = _glu_branch(y_raw, gs_ref[...], wg_ref, bg_ref[...])
    for n, c0 in enumerate(range(0, d, tn)):
        out = (parts[n]
               + jnp.dot(yp, w_ref[0, da:da + dp, c0:c0 + tn], preferred_element_type=F32)
               + jnp.dot(ys, w_ref[0, da + dp:, c0:c0 + tn], preferred_element_type=F32))
        o_ref[:, c0:c0 + tn] = x_ref[:, c0:c0 + tn] + mod_ref[:, c0:c0 + tn] * out
    if final:
        xn = o_ref[...]
        ms = jnp.mean(xn * xn, axis=-1, keepdims=True)
        o_ref[...] = xn * lax.rsqrt(ms + NORM_EPS) * fg_ref[...]


def _outproj(ya, rest, y_ssm, w_pool, pool_scale, w_glu, b_glu, w_out, x, gate, final_g,
             *, layer, final, col_up, col_gp, col_gs):
    s, d = x.shape
    dm = w_out.shape[1]
    d_pool = pool_scale.shape[1]
    d_ssm = b_glu.shape[1]
    tm = _pick(s, 512)
    halo_blocks = tm // POOL_HALO
    cu, cg, cs = col_up // d_pool, col_gp // d_pool, col_gs // d_ssm
    const = lambda a: pl.BlockSpec(a.shape, lambda i: (0,) * a.ndim)
    return pl.pallas_call(
        functools.partial(_outproj_kernel, final=final),
        grid=(s // tm,),
        in_specs=[
            pl.BlockSpec((tm, ya.shape[1]), lambda i: (i, 0)),
            pl.BlockSpec((tm, d_pool), lambda i: (i, cu)),
            pl.BlockSpec((POOL_HALO, d_pool),
                         lambda i: (jnp.maximum(i * halo_blocks - 1, 0), cu)),
            pl.BlockSpec((tm, d_pool), lambda i: (i, cg)),
            pl.BlockSpec((y_ssm.shape[0], tm, y_ssm.shape[2]), lambda i: (0, i, 0)),
            pl.BlockSpec((tm, d_ssm), lambda i: (i, cs)),
            const(w_pool), const(pool_scale), const(w_glu), const(b_glu),
            pl.BlockSpec((1, dm, d), lambda i: (layer, 0, 0), pipeline_mode=pl.Buffered(1)),
            pl.BlockSpec((tm, d), lambda i: (i, 0)),
            pl.BlockSpec((1, d), lambda i: (0, 0)),
            pl.BlockSpec((1, d), lambda i: (0, 0)),
        ],
        out_specs=pl.BlockSpec((tm, d), lambda i: (i, 0)),
        out_shape=jax.ShapeDtypeStruct((s, d), F32),
        compiler_params=_params(("arbitrary",)),
        name="outproj",
    )(ya, rest, rest, rest, y_ssm, rest, w_pool, pool_scale, w_glu, b_glu, w_out, x, gate,
      final_g)


def kernel(x, c, norm_g, w_ada, b_ada, w_in, b_f, w_pool, pool_scale, lam_re, lam_im,
           ssm_b_re, ssm_b_im, ssm_c_re, ssm_c_im, ssm_d, log_dt, w_glu, b_glu, w_out,
           final_g):
    b, s, d = x.shape
    assert b == 1
    depth = w_in.shape[0]
    d_pool = pool_scale.shape[1]
    d_ssm = b_glu.shape[1]
    d_attn = N_HEADS * HEAD_DIM
    n_f = b_f.shape[1]
    assert n_f == N_HEADS and w_in.shape[2] == 4 * d_attn + n_f + 2 * d_pool + 2 * d_ssm

    n_al = w_in.shape[2] // LANES * LANES
    w_all = w_in[:, :, :n_al].astype(BF16)
    w_tail = jnp.concatenate([w_all[:, :, 4 * d_attn + n_f:],
                              w_in[:, :, n_al:].astype(BF16)], axis=2)
    w_f = jnp.pad(w_in[:, :, 4 * d_attn:4 * d_attn + n_f],
                  ((0, 0), (0, 0), (0, LANES - n_f))).astype(BF16)
    b_f_row = jnp.pad(b_f, ((0, 0), (0, LANES - n_f))).reshape(depth, 1, LANES)
    col_up = d_attn
    col_gp = col_up + d_pool
    col_us = col_gp + d_pool
    col_gs = col_us + d_ssm

    mod = _ada_mod(c, w_ada, b_ada).reshape(depth, 3, d)
    t_all, p_all, e_all, al_all = _ssm_prep(lam_re, lam_im, log_dt, ssm_b_re, ssm_b_im,
                                            ssm_c_re, ssm_c_im)
    w_pool_b = w_pool.astype(BF16)
    w_glu_b = w_glu.astype(BF16)
    w_out_b = w_out.astype(BF16)

    xs = x.reshape(s, d)
    for l in range(depth):
        qkv, rest, ft = _inproj(xs, norm_g[l].reshape(1, d), mod[l], w_all, w_tail, w_f[l],
                                b_f_row[l], layer=l, d_attn=d_attn)
        ya = _attention(qkv, ft, rest, d_attn=d_attn)
        y_ssm = _ssm(rest, t_all, p_all, e_all, al_all, ssm_d[l].reshape(1, d_ssm),
                     layer=l, col_u=col_us)
        xs = _outproj(ya, rest, y_ssm, w_pool_b[l], pool_scale[l].reshape(1, d_pool),
                      w_glu_b[l], b_glu[l].reshape(1, d_ssm), w_out_b, xs, mod[l, 2:3, :],
                      final_g.reshape(1, d), layer=l, final=(l == depth - 1),
                      col_up=col_up, col_gp=col_gp, col_gs=col_gs)
    return xs.reshape(b, s, d).astype(x.dtype)
```

```python
import functools
import math

import jax
import jax.numpy as jnp
from jax import lax
from jax.experimental import pallas as pl
from jax.experimental.pallas import tpu as pltpu

F32 = jnp.float32
BF16 = jnp.bfloat16

N_HEADS = 8
HEAD_DIM = 128
POOL_WINDOWS = (2, 4, 8, 16)
POOL_GROUP = 128
POOL_HALO = 16
SSM_GROUP = 16
SSM_STATE = 64
SSM_SLAB_GROUPS = 8
NORM_EPS = 1e-6
LANES = 128
VMEM_LIMIT = 56 * 1024 * 1024
PROJ_TN = 512


def _params(sem, vmem=VMEM_LIMIT):
    return pltpu.CompilerParams(dimension_semantics=sem, vmem_limit_bytes=vmem)


def _sigmoid(x):
    return 1.0 / (1.0 + jnp.exp(-x))


def _silu(x):
    return x * _sigmoid(x)


def _pick(n, pref):
    t = min(n, pref)
    while n % t:
        t //= 2
    return t


def _ada_kernel(c_ref, w_ref, b_ref, o_ref):
    ca = _silu(c_ref[...])
    o_ref[0] = jnp.sum(w_ref[0] * ca, axis=0, keepdims=True) + b_ref[0]


def _inproj_kernel(x_ref, g_ref, mod_ref, wa_ref, wt_ref, wf_ref, bf_ref,
                   qkv_ref, rest_ref, ft_ref, h_ref, carry_ref, *, d_attn, q_scale):
    i = pl.program_id(0)
    tm = x_ref.shape[0]
    n_qkv = qkv_ref.shape[0] * HEAD_DIM
    n_a = wa_ref.shape[2]
    n = n_a + wt_ref.shape[2]
    tn = PROJ_TN

    def put_heads(c0, val):
        for hh in range(tn // HEAD_DIM):
            qkv_ref[c0 // HEAD_DIM + hh] = val[:, hh * HEAD_DIM:(hh + 1) * HEAD_DIM]

    x = x_ref[...]
    ms = jnp.mean(x * x, axis=-1, keepdims=True)
    shift = mod_ref[0:1, :]
    scale = mod_ref[1:2, :]
    h = (x * lax.rsqrt(ms + NORM_EPS) * g_ref[...]) * (1.0 + scale) + shift
    h_ref[...] = h.astype(BF16)

    for c0 in range(0, n, tn):
        w = wa_ref[0, :, c0:c0 + tn] if c0 < n_a else wt_ref[0, :, c0 - n_a:c0 - n_a + tn]
        proj = jnp.dot(h_ref[...], w, preferred_element_type=F32)
        if c0 < d_attn:
            put_heads(c0, (proj * q_scale).astype(BF16))
        elif c0 < n_qkv:
            put_heads(c0, proj.astype(BF16))
        else:
            rest_ref[:, c0 - n_qkv:c0 - n_qkv + tn] = proj

    f = jnp.dot(h_ref[...], wf_ref[...], preferred_element_type=F32) + bf_ref[...]
    logf = -(jnp.maximum(-f, 0.0) + jnp.log1p(jnp.exp(-jnp.abs(f))))
    row = lax.broadcasted_iota(jnp.int32, logf.shape, 0)
    cum = logf
    d = 1
    while d < tm:
        cum = cum + jnp.where(row >= d, pltpu.roll(cum, d, axis=0), 0.0)
        d *= 2

    @pl.when(i == 0)
    def _():
        carry_ref[...] = jnp.zeros_like(carry_ref)

    cum = cum + carry_ref[0:1, :]
    carry_ref[...] = jnp.broadcast_to(cum[tm - 1:tm, :], carry_ref.shape)
    cum_t = cum.T
    for hh in range(N_HEADS):
        ft_ref[hh] = cum_t[hh:hh + 1, :]


def _inproj(x, g, mod, w_all, w_tail, w_f, b_f_row, *, layer, d_attn):
    s, d = x.shape
    n_a = 4 * d_attn
    n = n_a + w_tail.shape[2]
    n_qkv = 3 * d_attn
    tm = _pick(s, 256)
    kern = functools.partial(_inproj_kernel, d_attn=d_attn,
                             q_scale=HEAD_DIM ** -0.5 * math.log2(math.e))
    once = pl.Buffered(1)
    return pl.pallas_call(
        kern,
        grid=(s // tm,),
        in_specs=[
            pl.BlockSpec((tm, d), lambda i: (i, 0)),
            pl.BlockSpec((1, d), lambda i: (0, 0)),
            pl.BlockSpec((3, d), lambda i: (0, 0)),
            pl.BlockSpec((1, d, n_a), lambda i: (layer, 0, 0), pipeline_mode=once),
            pl.BlockSpec((1, d, n - n_a), lambda i: (layer, 0, 0), pipeline_mode=once),
            pl.BlockSpec((d, LANES), lambda i: (0, 0), pipeline_mode=once),
            pl.BlockSpec((1, LANES), lambda i: (0, 0)),
        ],
        out_specs=[
            pl.BlockSpec((n_qkv // HEAD_DIM, tm, HEAD_DIM), lambda i: (0, i, 0)),
            pl.BlockSpec((tm, n - n_qkv), lambda i: (i, 0)),
            pl.BlockSpec((N_HEADS, 1, tm), lambda i: (0, 0, i)),
        ],
        out_shape=[
            jax.ShapeDtypeStruct((n_qkv // HEAD_DIM, s, HEAD_DIM), BF16),
            jax.ShapeDtypeStruct((s, n - n_qkv), F32),
            jax.ShapeDtypeStruct((N_HEADS, 1, s), F32),
        ],
        scratch_shapes=[pltpu.VMEM((tm, d), BF16), pltpu.VMEM((8, LANES), F32)],
        compiler_params=_params(("arbitrary",)),
        name="inproj",
    )(x, g, mod, w_all, w_tail, w_f, b_f_row)


AUG_TERMS = 3
ONES_ROWS = 16
Q_STRIP = 1024
FAST_UNROLL = 4
LOG2E = math.log2(math.e)


SAFE_EXP = 60.0
ZERO_EXP = -136.0
NORM_SLACK = 1.02


def _tile_lanes(row, n):
    return jnp.concatenate([row] * (n // LANES), axis=1)


def _attn_kernel(q_ref, k_ref, v_ref, ft_ref, g_ref, o_ref,
                 kaug_ref, vt_ref, base_ref, bend_ref, kall_ref, acc_ref, *, tq, tk):
    i = pl.program_id(1)
    seq = k_ref.shape[1]
    dh = HEAD_DIM
    sub = lax.broadcasted_iota(jnp.int32, (LANES, LANES), 0)
    lane_row = lax.broadcasted_iota(jnp.int32, (1, LANES), 1)
    ones_sq = jnp.ones((LANES, LANES), BF16)

    @pl.when(i == 0)
    def _():
        vt_ref[dh:, :] = jnp.ones((ONES_ROWS, seq), BF16)
        kall_ref[...] = jnp.zeros_like(kall_ref)
        bend_ref[...] = jnp.zeros_like(bend_ref)

        def fill(c, carry):
            c_off = pl.multiple_of(c * tk, tk)
            b_row = (-LOG2E) * ft_ref[0, :, pl.ds(c_off, tk)]
            base = jnp.broadcast_to(b_row[:, 0:1], (1, LANES))
            base_ref[pl.ds(c, 1), :] = base
            bend_ref[0:1, :] = jnp.where(
                lane_row == c, jnp.broadcast_to(b_row[:, tk - 1:tk], (1, LANES)),
                bend_ref[0:1, :])
            rel = b_row - _tile_lanes(base, tk)
            hi = rel.astype(BF16).astype(F32)
            mid = (rel - hi).astype(BF16).astype(F32)
            lo = (rel - hi - mid).astype(BF16).astype(F32)
            for bb in range(tk // LANES):
                off = pl.multiple_of(c_off + bb * LANES, LANES)
                cs = slice(bb * LANES, (bb + 1) * LANES)
                terms = jnp.where(sub == 0, hi[:, cs], jnp.where(
                    sub == 1, mid[:, cs], jnp.where(sub == 2, lo[:, cs], 0.0)))
                kb = k_ref[0, pl.ds(off, LANES), :]
                kaug_ref[pl.ds(off, LANES), 0:dh] = kb
                kaug_ref[pl.ds(off, LANES), dh:] = terms.T.astype(BF16)
                vt_ref[0:dh, pl.ds(off, LANES)] = (
                    v_ref[0, pl.ds(off, LANES), :].astype(F32).T.astype(BF16))
                kf = kb.astype(F32)
                n2 = jnp.dot((kf * kf).astype(BF16), ones_sq, preferred_element_type=F32)
                kall_ref[0:1, :] = jnp.maximum(kall_ref[0:1, :],
                                               jnp.max(n2, axis=0, keepdims=True))
            return carry
        lax.fori_loop(0, seq // tk, fill, 0)

    lane_q = lax.broadcasted_iota(jnp.int32, (tq, LANES), 1)
    q = q_ref[0]
    q_aug = jnp.concatenate(
        [q, jnp.where(lane_q < AUG_TERMS, 1.0, 0.0).astype(BF16)], axis=1)
    base_q = base_ref[pl.ds(i, 1), :]
    n_strip = tq // Q_STRIP

    def chunk(j):
        k_off = pl.multiple_of(j * tk, tk)
        kc = kaug_ref[pl.ds(k_off, tk), :]
        vc = vt_ref[:, pl.ds(k_off, tk)]
        delta = base_ref[pl.ds(j, 1), :] - base_q
        return kc, vc, delta

    def scores(kc, lo, hi):
        return lax.dot_general(kc, q_aug[lo:hi, :], (((1,), (1,)), ((), ())),
                               preferred_element_type=F32)

    def online_step(j, m, masked):
        kc, vc, delta = chunk(j)
        delta = _tile_lanes(delta, Q_STRIP)
        m_out = []
        for st in range(n_strip):
            lo, hi = st * Q_STRIP, (st + 1) * Q_STRIP
            s = scores(kc, lo, hi)
            if masked:
                key = lax.broadcasted_iota(jnp.int32, (tk, Q_STRIP), 0)
                qry = lax.broadcasted_iota(jnp.int32, (tk, Q_STRIP), 1) + lo
                s = jnp.where(key <= qry, s, -jnp.inf)
            m_old = m[st] - delta
            m_new = jnp.maximum(m_old, jnp.max(s, axis=0, keepdims=True))
            p = jnp.exp2(s - m_new).astype(BF16)
            corr = jnp.exp2(m_old - m_new)
            pv = jnp.dot(vc, p, preferred_element_type=F32)
            acc_ref[:, lo:hi] = acc_ref[:, lo:hi] * corr + pv
            m_out.append(m_new + delta)
        return tuple(m_out)

    q_off = pl.multiple_of(i * tq, tq)
    ones_row = jnp.ones((8, dh), BF16)
    row_sum = lambda a: lax.dot_general(ones_row, a.astype(BF16), (((1,), (1,)), ((), ())),
                                        preferred_element_type=F32)[0:1, :]
    qf = q.astype(F32)
    qk_self = row_sum(qf * kaug_ref[pl.ds(q_off, tq), 0:dh].astype(F32))
    rel_q = (-LOG2E) * ft_ref[0, :, pl.ds(q_off, tq)] - _tile_lanes(base_q, tq)
    m_row = qk_self + rel_q
    qk_bound = jnp.sqrt(row_sum(qf * qf) * _tile_lanes(kall_ref[0:1, :], tq)) * NORM_SLACK + 1.0
    slack_diag = jnp.max(qk_bound - qk_self)
    slack = jnp.max(qk_bound - m_row)
    live = jnp.logical_and(slack + (bend_ref[0:1, :] - base_q) >= ZERO_EXP, lane_row < i)
    n_live = jnp.sum(live.astype(jnp.int32))

    def one_pass(j):
        kc, vc, delta = chunk(j)
        ref = m_row - _tile_lanes(delta, tq)
        for st in range(n_strip):
            lo, hi = st * Q_STRIP, (st + 1) * Q_STRIP
            x = scores(kc, lo, hi) - ref[:, lo:hi]
            acc_ref[:, lo:hi] += jnp.dot(vc, jnp.exp2(x).astype(BF16),
                                         preferred_element_type=F32)

    def diag_pass():
        kc, vc, _ = chunk(i)
        half = tk // 2
        tri = (lax.broadcasted_iota(jnp.int32, (half, half), 0)
               <= lax.broadcasted_iota(jnp.int32, (half, half), 1))
        x = lax.dot_general(kc[0:half, :], q_aug, (((1,), (1,)), ((), ())),
                            preferred_element_type=F32) - m_row
        x = jnp.concatenate([jnp.where(tri, x[:, 0:half], -jnp.inf), x[:, half:]], axis=1)
        acc_ref[...] = jnp.dot(vc[:, 0:half], jnp.exp2(x).astype(BF16),
                               preferred_element_type=F32)
        x = lax.dot_general(kc[half:, :], q_aug[half:, :], (((1,), (1,)), ((), ())),
                            preferred_element_type=F32) - m_row[:, half:]
        acc_ref[:, half:] += jnp.dot(vc[:, half:], jnp.exp2(jnp.where(tri, x, -jnp.inf)).astype(BF16),
                                     preferred_element_type=F32)

    diag_pass()

    @pl.when(slack_diag <= SAFE_EXP)
    def _():
        def fast_group(jj, carry):
            for u in range(FAST_UNROLL):
                one_pass(i - 1 - u - FAST_UNROLL * jj)
            return carry
        n_group = n_live // FAST_UNROLL
        lax.fori_loop(0, n_group, fast_group, 0)

        def fast_single(jj, carry):
            one_pass(i - 1 - FAST_UNROLL * n_group - jj)
            return carry
        lax.fori_loop(0, n_live - FAST_UNROLL * n_group, fast_single, 0)

    @pl.when(slack_diag > SAFE_EXP)
    def _():
        acc_ref[...] = jnp.zeros_like(acc_ref)
        m0 = tuple(jnp.full((1, Q_STRIP), -jnp.inf, F32) for _ in range(n_strip))
        m_d = online_step(i, m0, True)
        lax.fori_loop(0, i, lambda jj, m: online_step(i - 1 - jj, m, False), m_d)

    y = (acc_ref[0:dh, :] / acc_ref[dh:dh + 1, :]).T
    o_ref[...] = (y * _silu(g_ref[...])).astype(o_ref.dtype)


def _attention(qkv, ft, rest, *, d_attn):
    s = qkv.shape[1]
    tq = _pick(s, 1024)
    nh = d_attn // HEAD_DIM
    kern = functools.partial(_attn_kernel, tq=tq, tk=tq)
    return pl.pallas_call(
        kern,
        grid=(nh, s // tq),
        in_specs=[
            pl.BlockSpec((1, tq, HEAD_DIM), lambda h, i: (h, i, 0)),
            pl.BlockSpec((1, s, HEAD_DIM), lambda h, i: (nh + h, 0, 0)),
            pl.BlockSpec((1, s, HEAD_DIM), lambda h, i: (2 * nh + h, 0, 0)),
            pl.BlockSpec((1, 1, s), lambda h, i: (h, 0, 0)),
            pl.BlockSpec((tq, HEAD_DIM), lambda h, i: (i, h)),
        ],
        out_specs=pl.BlockSpec((tq, HEAD_DIM), lambda h, i: (i, h)),
        out_shape=jax.ShapeDtypeStruct((s, d_attn), BF16),
        scratch_shapes=[
            pltpu.VMEM((s, 2 * HEAD_DIM), BF16),
            pltpu.VMEM((HEAD_DIM + ONES_ROWS, s), BF16),
            pltpu.VMEM((max(s // tq, 8), LANES), F32),
            pltpu.VMEM((8, LANES), F32),
            pltpu.VMEM((8, LANES), F32),
            pltpu.VMEM((HEAD_DIM + ONES_ROWS, tq), F32),
        ],
        compiler_params=_params(("arbitrary", "arbitrary")),
        name="fox_attention",
    )(qkv, qkv, qkv, ft, rest)


def _pool_branch(i, u, prev, gate, w_ref, scale):
    tp = u.shape[0]
    prev = jnp.where(i > 0, prev, 0.0)
    t1 = lax.broadcasted_iota(jnp.int32, (tp, POOL_GROUP), 0) + (i * tp + 1)
    outs = []
    for g, w in enumerate(POOL_WINDOWS):
        lo, hi = g * POOL_GROUP, (g + 1) * POOL_GROUP
        ug = u[:, lo:hi]
        ext = jnp.concatenate([prev[:, lo:hi], ug], axis=0)
        win = ext
        span = 1
        while span < w:
            win = win + pltpu.roll(win, span, axis=0)
            span *= 2
        win = win[POOL_HALO:]
        cnt = jnp.minimum(t1, w).astype(F32)
        pooled = win / cnt - ug
        mixed = jnp.dot(pooled.astype(BF16), w_ref[g], preferred_element_type=F32)
        outs.append(mixed)
    mixed = jnp.concatenate(outs, axis=1) * scale
    return (mixed * _silu(gate)).astype(BF16)


SSM_CHUNK = 8


def _discretise(lr, li, ldt):
    dt = jnp.exp(ldt)
    mag = jnp.exp(lr * dt)
    ab_re = mag * jnp.cos(li * dt)
    ab_im = mag * jnp.sin(li * dt)
    den = lr * lr + li * li
    nr = ab_re - 1.0
    ni = ab_im
    z_re = (nr * lr + ni * li) / den
    z_im = (ni * lr - nr * li) / den
    return ab_re, ab_im, z_re, z_im


def _powers(a_re, a_im, n):
    out = [(jnp.ones_like(a_re), jnp.zeros_like(a_im))]
    for _ in range(n):
        p_re, p_im = out[-1]
        out.append((p_re * a_re - p_im * a_im, p_re * a_im + p_im * a_re))
    return out


def _ssm_prep_kernel(lr_ref, li_ref, ldt_ref, lrc_ref, lic_ref, ldtc_ref,
                     br_ref, bi_ref, cr_ref, ci_ref, c_ref, wada_ref, bada_ref,
                     t_ref, p_ref, e_ref, al_ref, mod_ref, *, n_ada_tiles):
    @pl.when(pl.program_id(1) < n_ada_tiles)
    def _():
        _ada_kernel(c_ref, wada_ref, bada_ref, mod_ref)

    nl = SSM_CHUNK
    cw = br_ref.shape[2]
    a_re, a_im, z_re, z_im = _discretise(lr_ref[0, 0], li_ref[0, 0], ldt_ref[0, 0])
    pw = _powers(a_re, a_im, nl)
    al_ref[0, 0] = jnp.concatenate([pw[nl][0], pw[nl][1]], axis=0)
    br = br_ref[0, 0]
    bi = bi_ref[0, 0]
    bb_re = z_re * br - z_im * bi
    bb_im = z_re * bi + z_im * br
    bb = jnp.concatenate([bb_re, bb_im], axis=1)
    ac_re, ac_im, _, _ = _discretise(lrc_ref[0, 0], lic_ref[0, 0], ldtc_ref[0, 0])
    pwc = _powers(ac_re, ac_im, nl)
    cr = cr_ref[0, 0]
    ci = ci_ref[0, 0]
    ca = [jnp.concatenate([cr * q_re - ci * q_im, -(cr * q_im + ci * q_re)], axis=0)
          for q_re, q_im in pwc]
    kd_all = jnp.dot(bb, jnp.concatenate(ca[:nl], axis=1), preferred_element_type=F32,
                     precision=lax.Precision.HIGHEST).astype(BF16)
    kd = [kd_all[:, d * cw:(d + 1) * cw] for d in range(nl)]
    zero = jnp.zeros((cw, cw), BF16)
    for src in range(nl):
        for dst in range(nl):
            t_ref[0, 0, src * cw:(src + 1) * cw, dst * cw:(dst + 1) * cw] = (
                kd[dst - src] if dst >= src else zero)
        q_re, q_im = pw[nl - 1 - src]
        p_ref[0, 0, src * cw:(src + 1) * cw, :] = jnp.concatenate(
            [bb_re * q_re - bb_im * q_im, bb_re * q_im + bb_im * q_re], axis=1).astype(BF16)
        e_ref[0, 0, :, src * cw:(src + 1) * cw] = ca[src + 1].astype(BF16)


def _prep(lam_re, lam_im, log_dt, b_re, b_im, c_re, c_im, c, w_ada, b_ada):
    depth, ng, ns = lam_re.shape
    d, n_ada = w_ada.shape[1:]
    gc = b_re.shape[-1]
    n_slab = ng // SSM_SLAB_GROUPS
    eye = jnp.eye(SSM_SLAB_GROUPS, dtype=F32)
    sw = SSM_SLAB_GROUPS * ns

    def place_b(b):
        b = b.reshape(depth, n_slab, SSM_SLAB_GROUPS, ns, gc).transpose(0, 1, 2, 4, 3)
        return (b[:, :, :, :, None, :] * eye[None, None, :, None, :, None]).reshape(
            depth, n_slab, SSM_SLAB_GROUPS * gc, sw)

    def place_c(c):
        c = c.reshape(depth, n_slab, SSM_SLAB_GROUPS, gc, ns).transpose(0, 1, 2, 4, 3)
        return (c[:, :, :, :, None, :] * eye[None, None, :, None, :, None]).reshape(
            depth, n_slab, sw, SSM_SLAB_GROUPS * gc)

    cw = SSM_SLAB_GROUPS * gc
    nl = SSM_CHUNK
    ldt = jnp.broadcast_to(log_dt[:, :, None], (depth, ng, ns))
    row = lambda v: v.reshape(depth, n_slab, 1, sw)
    col = lambda v: jnp.broadcast_to(v.reshape(depth, n_slab, sw, 1), (depth, n_slab, sw, cw))
    blk = lambda *shape: pl.BlockSpec((1, 1) + shape, lambda l, k: (l, k, 0, 0))
    tn = _pick(n_ada, 2048)
    n_ada_tiles = n_ada // tn
    assert n_ada_tiles <= n_slab
    ada_tile = lambda l, k: (l, 0, jnp.minimum(k, n_ada_tiles - 1))
    return pl.pallas_call(
        functools.partial(_ssm_prep_kernel, n_ada_tiles=n_ada_tiles),
        grid=(depth, n_slab),
        in_specs=[blk(1, sw)] * 3 + [blk(sw, cw)] * 3 + [blk(cw, sw)] * 2 + [blk(sw, cw)] * 2 + [
            pl.BlockSpec((d, 1), lambda l, k: (0, 0)),
            pl.BlockSpec((1, d, tn), ada_tile),
            pl.BlockSpec((1, 1, tn), ada_tile),
        ],
        out_specs=[blk(nl * cw, nl * cw), blk(nl * cw, 2 * sw), blk(2 * sw, nl * cw),
                   blk(2, sw), pl.BlockSpec((1, 1, tn), ada_tile)],
        out_shape=[
            jax.ShapeDtypeStruct((depth, n_slab, nl * cw, nl * cw), BF16),
            jax.ShapeDtypeStruct((depth, n_slab, nl * cw, 2 * sw), BF16),
            jax.ShapeDtypeStruct((depth, n_slab, 2 * sw, nl * cw), BF16),
            jax.ShapeDtypeStruct((depth, n_slab, 2, sw), F32),
            jax.ShapeDtypeStruct((depth, 1, n_ada), F32),
        ],
        compiler_params=_params(("arbitrary", "arbitrary")),
        name="prep",
    )(row(lam_re), row(lam_im), row(ldt), col(lam_re), col(lam_im), col(ldt),
      place_b(b_re), place_b(b_im), place_c(c_re), place_c(c_im),
      c.reshape(d, 1), w_ada, b_ada.reshape(depth, 1, n_ada))


def _gelu_tanh(y):
    c = math.sqrt(2.0 / math.pi)
    return 0.5 * y * (1.0 + jnp.tanh(c * (y + 0.044715 * (y * y * y))))


SSM_SLABS_PER_STEP = 2


def _ssm_kernel(*refs):
    per = SSM_SLABS_PER_STEP
    u_refs = refs[:per]
    t_ref, p_ref, e_ref, al_ref, d_ref, y_ref, uc_ref, carry_ref = refs[per:]
    i = pl.program_id(1)
    nl = SSM_CHUNK
    ts, cw = u_refs[0].shape
    rows = ts // nl
    sw = al_ref.shape[3]

    @pl.when(i == 0)
    def _():
        carry_ref[...] = jnp.zeros_like(carry_ref)

    row = lax.broadcasted_iota(jnp.int32, (rows, sw), 0)
    first = row == 0
    vs, yts = [], []
    for k in range(per):
        for tau in range(nl):
            uc_ref[k, :, tau * cw:(tau + 1) * cw] = (
                u_refs[k][pl.ds(tau, rows, stride=nl), :].astype(BF16))
        vs.append(jnp.dot(uc_ref[k], p_ref[0, k], preferred_element_type=F32))
    for k in range(per):
        yts.append(jnp.dot(uc_ref[k], t_ref[0, k], preferred_element_type=F32))
    for k in range(per):
        u_ref = u_refs[k]
        xr = vs[k][:, :sw]
        xi = vs[k][:, sw:]
        a_re = al_ref[0, k, 0:1, :]
        a_im = al_ref[0, k, 1:2, :]
        c_re = carry_ref[k, 0:1, :]
        c_im = carry_ref[k, 1:2, :]
        xr = xr + jnp.where(first, a_re * c_re - a_im * c_im, 0.0)
        xi = xi + jnp.where(first, a_re * c_im + a_im * c_re, 0.0)
        q_re, q_im = a_re, a_im
        d = 1
        while d < rows:
            if d % 8:
                keep = row >= d
                sr = jnp.where(keep, pltpu.roll(xr, d, axis=0), 0.0)
                si = jnp.where(keep, pltpu.roll(xi, d, axis=0), 0.0)
                xr, xi = xr + (q_re * sr - q_im * si), xi + (q_re * si + q_im * sr)
            else:
                sr, si = xr[:rows - d], xi[:rows - d]
                xr, xi = (
                    jnp.concatenate([xr[:d], xr[d:] + (q_re * sr - q_im * si)], axis=0),
                    jnp.concatenate([xi[:d], xi[d:] + (q_re * si + q_im * sr)], axis=0))
            q_re, q_im = q_re * q_re - q_im * q_im, 2.0 * (q_re * q_im)
            d *= 2
        pr = jnp.where(first, c_re, pltpu.roll(xr, 1, axis=0))
        pi = jnp.where(first, c_im, pltpu.roll(xi, 1, axis=0))
        carry_ref[k, 0:1, :] = xr[rows - 1:rows, :]
        carry_ref[k, 1:2, :] = xi[rows - 1:rows, :]
        xp = jnp.concatenate([pr, pi], axis=1).astype(BF16)
        yc = yts[k] + jnp.dot(xp, e_ref[0, k], preferred_element_type=F32)
        for tau in range(nl):
            y_ref[k, pl.ds(tau, rows, stride=nl), :] = (
                yc[:, tau * cw:(tau + 1) * cw]
                + d_ref[:, k * cw:(k + 1) * cw] * u_ref[pl.ds(tau, rows, stride=nl), :])


def _ssm(rest, t_mat, p_mat, e_mat, a_l, d_row, *, layer, col_u):
    s = rest.shape[0]
    n_slab, cw = t_mat.shape[1], d_row.shape[1] // t_mat.shape[1]
    per = SSM_SLABS_PER_STEP
    ts = _pick(s, 4096)
    cu = col_u // cw
    sw = a_l.shape[3]
    mat = lambda m: pl.BlockSpec((1, per) + m.shape[2:], lambda k, i: (layer, k, 0, 0))
    u_spec = lambda n: pl.BlockSpec((ts, cw), lambda k, i: (i, cu + per * k + n))
    return pl.pallas_call(
        _ssm_kernel,
        grid=(n_slab // per, s // ts),
        in_specs=[u_spec(n) for n in range(per)] + [
            mat(t_mat), mat(p_mat), mat(e_mat), mat(a_l),
            pl.BlockSpec((1, per * cw), lambda k, i: (0, k)),
        ],
        out_specs=pl.BlockSpec((per, ts, cw), lambda k, i: (k, i, 0)),
        out_shape=jax.ShapeDtypeStruct((n_slab, s, cw), F32),
        scratch_shapes=[pltpu.VMEM((per, ts // SSM_CHUNK, SSM_CHUNK * cw), BF16),
                        pltpu.VMEM((per, 2, sw), F32)],
        compiler_params=_params(("arbitrary", "arbitrary")),
        name="ssm",
    )(*([rest] * per), t_mat, p_mat, e_mat, a_l, d_row)


def _glu_branch(y_raw, gate, wg_ref, bg):
    y = _gelu_tanh(y_raw)
    z = jnp.dot(y.astype(BF16), wg_ref[...], preferred_element_type=F32) + bg
    return (y * _sigmoid(z) * _silu(gate)).astype(BF16)


def _outproj_kernel(ya_ref, up_ref, prev_ref, gp_ref, ysr_ref, gs_ref, wp_ref, sc_ref,
                    wg_ref, bg_ref, w_ref, x_ref, mod_ref, fg_ref, o_ref, *, final):
    i = pl.program_id(0)
    da = ya_ref.shape[1]
    dp = up_ref.shape[1]
    d = x_ref.shape[1]
    tn = PROJ_TN
    ya = ya_ref[...]
    parts = [jnp.dot(ya, w_ref[0, 0:da, c0:c0 + tn], preferred_element_type=F32)
             for c0 in range(0, d, tn)]
    yp = _pool_branch(i, up_ref[...], prev_ref[...], gp_ref[...], wp_ref, sc_ref[...])
    y_raw = jnp.concatenate([ysr_ref[k] for k in range(ysr_ref.shape[0])], axis=1)
    ys = _glu_branch(y_raw, gs_ref[...], wg_ref, bg_ref[...])
    for n, c0 in enumerate(range(0, d, tn)):
        out = (parts[n]
               + jnp.dot(yp, w_ref[0, da:da + dp, c0:c0 + tn], preferred_element_type=F32)
               + jnp.dot(ys, w_ref[0, da + dp:, c0:c0 + tn], preferred_element_type=F32))
        o_ref[:, c0:c0 + tn] = x_ref[:, c0:c0 + tn] + mod_ref[:, c0:c0 + tn] * out
    if final:
        xn = o_ref[...]
        ms = jnp.mean(xn * xn, axis=-1, keepdims=True)
        o_ref[...] = xn * lax.rsqrt(ms + NORM_EPS) * fg_ref[...]


def _outproj(ya, rest, y_ssm, w_pool, pool_scale, w_glu, b_glu, w_out, x, gate, final_g,
             *, layer, final, col_up, col_gp, col_gs):
    s, d = x.shape
    dm = w_out.shape[1]
    d_pool = pool_scale.shape[1]
    d_ssm = b_glu.shape[1]
    tm = _pick(s, 512)
    halo_blocks = tm // POOL_HALO
    cu, cg, cs = col_up // d_pool, col_gp // d_pool, col_gs // d_ssm
    const = lambda a: pl.BlockSpec(a.shape, lambda i: (0,) * a.ndim)
    return pl.pallas_call(
        functools.partial(_outproj_kernel, final=final),
        grid=(s // tm,),
        in_specs=[
            pl.BlockSpec((tm, ya.shape[1]), lambda i: (i, 0)),
            pl.BlockSpec((tm, d_pool), lambda i: (i, cu)),
            pl.BlockSpec((POOL_HALO, d_pool),
                         lambda i: (jnp.maximum(i * halo_blocks - 1, 0), cu)),
            pl.BlockSpec((tm, d_pool), lambda i: (i, cg)),
            pl.BlockSpec((y_ssm.shape[0], tm, y_ssm.shape[2]), lambda i: (0, i, 0)),
            pl.BlockSpec((tm, d_ssm), lambda i: (i, cs)),
            const(w_pool), const(pool_scale), const(w_glu), const(b_glu),
            pl.BlockSpec((1, dm, d), lambda i: (layer, 0, 0), pipeline_mode=pl.Buffered(1)),
            pl.BlockSpec((tm, d), lambda i: (i, 0)),
            pl.BlockSpec((1, d), lambda i: (0, 0)),
            pl.BlockSpec((1, d), lambda i: (0, 0)),
        ],
        out_specs=pl.BlockSpec((tm, d), lambda i: (i, 0)),
        out_shape=jax.ShapeDtypeStruct((s, d), F32),
        compiler_params=_params(("arbitrary",)),
        name="outproj",
    )(ya, rest, rest, rest, y_ssm, rest, w_pool, pool_scale, w_glu, b_glu, w_out, x, gate,
      final_g)


def kernel(x, c, norm_g, w_ada, b_ada, w_in, b_f, w_pool, pool_scale, lam_re, lam_im,
           ssm_b_re, ssm_b_im, ssm_c_re, ssm_c_im, ssm_d, log_dt, w_glu, b_glu, w_out,
           final_g):
    b, s, d = x.shape
    assert b == 1
    depth = w_in.shape[0]
    d_pool = pool_scale.shape[1]
    d_ssm = b_glu.shape[1]
    d_attn = N_HEADS * HEAD_DIM
    n_f = b_f.shape[1]
    assert n_f == N_HEADS and w_in.shape[2] == 4 * d_attn + n_f + 2 * d_pool + 2 * d_ssm

    w_all = w_in.astype(BF16)
    w_tail = w_all[:, :, 4 * d_attn + n_f:]
    w_f = jnp.pad(w_in[:, :, 4 * d_attn:4 * d_attn + n_f],
                  ((0, 0), (0, 0), (0, LANES - n_f))).astype(BF16)
    b_f_row = jnp.pad(b_f, ((0, 0), (0, LANES - n_f))).reshape(depth, 1, LANES)
    col_up = d_attn
    col_gp = col_up + d_pool
    col_us = col_gp + d_pool
    col_gs = col_us + d_ssm

    t_all, p_all, e_all, al_all, mod = _prep(lam_re, lam_im, log_dt, ssm_b_re, ssm_b_im,
                                             ssm_c_re, ssm_c_im, c, w_ada, b_ada)
    mod = mod.reshape(depth, 3, d)
    w_pool_b = w_pool.astype(BF16)
    w_glu_b = w_glu.astype(BF16)
    w_out_b = w_out.astype(BF16)

    xs = x.reshape(s, d)
    for l in range(depth):
        qkv, rest, ft = _inproj(xs, norm_g[l].reshape(1, d), mod[l], w_all, w_tail, w_f[l],
                                b_f_row[l], layer=l, d_attn=d_attn)
        ya = _attention(qkv, ft, rest, d_attn=d_attn)
        y_ssm = _ssm(rest, t_all, p_all, e_all, al_all, ssm_d[l].reshape(1, d_ssm),
                     layer=l, col_u=col_us)
        xs = _outproj(ya, rest, y_ssm, w_pool_b[l], pool_scale[l].reshape(1, d_pool),
                      w_glu_b[l], b_glu[l].reshape(1, d_ssm), w_out_b, xs, mod[l, 2:3, :],
                      final_g.reshape(1, d), layer=l, final=(l == depth - 1),
                      col_up=col_up, col_gp=col_gp, col_gs=col_gs)
    return xs.reshape(b, s, d).astype(x.dtype)
```

```python
import functools
import math

import jax
import jax.numpy as jnp
from jax import lax
from jax.experimental import pallas as pl
from jax.experimental.pallas import tpu as pltpu

F32 = jnp.float32
BF16 = jnp.bfloat16

N_HEADS = 8
HEAD_DIM = 128
POOL_WINDOWS = (2, 4, 8, 16)
POOL_GROUP = 128
POOL_HALO = 16
SSM_GROUP = 16
SSM_STATE = 64
SSM_SLAB_GROUPS = 8
NORM_EPS = 1e-6
LANES = 128
VMEM_LIMIT = 56 * 1024 * 1024
PROJ_TN = 512


def _params(sem, vmem=VMEM_LIMIT):
    return pltpu.CompilerParams(dimension_semantics=sem, vmem_limit_bytes=vmem)


def _sigmoid(x):
    return 1.0 / (1.0 + jnp.exp(-x))


def _silu(x):
    return x * _sigmoid(x)


def _pick(n, pref):
    t = min(n, pref)
    while n % t:
        t //= 2
    return t


def _ada_kernel(c_ref, w_ref, b_ref, o_ref):
    ca = _silu(c_ref[...])
    o_ref[0] = jnp.sum(w_ref[0] * ca, axis=0, keepdims=True) + b_ref[0]


def _ada_mod(c, w_ada, b_ada):
    depth, d, n = w_ada.shape
    tn = _pick(n, 2048)
    return pl.pallas_call(
        _ada_kernel,
        grid=(depth, n // tn),
        in_specs=[
            pl.BlockSpec((d, 1), lambda l, j: (0, 0)),
            pl.BlockSpec((1, d, tn), lambda l, j: (l, 0, j)),
            pl.BlockSpec((1, 1, tn), lambda l, j: (l, 0, j)),
        ],
        out_specs=pl.BlockSpec((1, 1, tn), lambda l, j: (l, 0, j)),
        out_shape=jax.ShapeDtypeStruct((depth, 1, n), F32),
        compiler_params=_params(("arbitrary", "arbitrary")),
        name="ada_mod",
    )(c.reshape(d, 1), w_ada, b_ada.reshape(depth, 1, n))


def _inproj_kernel(x_ref, g_ref, mod_ref, wa_ref, wt_ref, wf_ref, bf_ref,
                   qkv_ref, rest_ref, ft_ref, h_ref, carry_ref, *, d_attn, q_scale):
    i = pl.program_id(0)
    tm = x_ref.shape[0]
    n_qkv = qkv_ref.shape[0] * HEAD_DIM
    n_a = wa_ref.shape[2]
    n = n_a + wt_ref.shape[2]
    tn = PROJ_TN

    def put_heads(c0, val):
        for hh in range(tn // HEAD_DIM):
            qkv_ref[c0 // HEAD_DIM + hh] = val[:, hh * HEAD_DIM:(hh + 1) * HEAD_DIM]

    x = x_ref[...]
    ms = jnp.mean(x * x, axis=-1, keepdims=True)
    shift = mod_ref[0:1, :]
    scale = mod_ref[1:2, :]
    h = (x * lax.rsqrt(ms + NORM_EPS) * g_ref[...]) * (1.0 + scale) + shift
    h_ref[...] = h.astype(BF16)

    for c0 in range(0, n, tn):
        w = wa_ref[0, :, c0:c0 + tn] if c0 < n_a else wt_ref[0, :, c0 - n_a:c0 - n_a + tn]
        proj = jnp.dot(h_ref[...], w, preferred_element_type=F32)
        if c0 < d_attn:
            put_heads(c0, (proj * q_scale).astype(BF16))
        elif c0 < n_qkv:
            put_heads(c0, proj.astype(BF16))
        else:
            rest_ref[:, c0 - n_qkv:c0 - n_qkv + tn] = proj

    f = jnp.dot(h_ref[...], wf_ref[...], preferred_element_type=F32) + bf_ref[...]
    logf = -(jnp.maximum(-f, 0.0) + jnp.log1p(jnp.exp(-jnp.abs(f))))
    row = lax.broadcasted_iota(jnp.int32, logf.shape, 0)
    cum = logf
    d = 1
    while d < tm:
        cum = cum + jnp.where(row >= d, pltpu.roll(cum, d, axis=0), 0.0)
        d *= 2

    @pl.when(i == 0)
    def _():
        carry_ref[...] = jnp.zeros_like(carry_ref)

    cum = cum + carry_ref[0:1, :]
    carry_ref[...] = jnp.broadcast_to(cum[tm - 1:tm, :], carry_ref.shape)
    cum_t = cum.T
    for hh in range(N_HEADS):
        ft_ref[hh] = cum_t[hh:hh + 1, :]


def _inproj(x, g, mod, w_all, w_tail, w_f, b_f_row, *, layer, d_attn):
    s, d = x.shape
    n_a = 4 * d_attn
    n = n_a + w_tail.shape[2]
    n_qkv = 3 * d_attn
    tm = _pick(s, 256)
    kern = functools.partial(_inproj_kernel, d_attn=d_attn,
                             q_scale=HEAD_DIM ** -0.5 * math.log2(math.e))
    once = pl.Buffered(1)
    return pl.pallas_call(
        kern,
        grid=(s // tm,),
        in_specs=[
            pl.BlockSpec((tm, d), lambda i: (i, 0)),
            pl.BlockSpec((1, d), lambda i: (0, 0)),
            pl.BlockSpec((3, d), lambda i: (0, 0)),
            pl.BlockSpec((1, d, n_a), lambda i: (layer, 0, 0), pipeline_mode=once),
            pl.BlockSpec((1, d, n - n_a), lambda i: (layer, 0, 0), pipeline_mode=once),
            pl.BlockSpec((d, LANES), lambda i: (0, 0), pipeline_mode=once),
            pl.BlockSpec((1, LANES), lambda i: (0, 0)),
        ],
        out_specs=[
            pl.BlockSpec((n_qkv // HEAD_DIM, tm, HEAD_DIM), lambda i: (0, i, 0)),
            pl.BlockSpec((tm, n - n_qkv), lambda i: (i, 0)),
            pl.BlockSpec((N_HEADS, 1, tm), lambda i: (0, 0, i)),
        ],
        out_shape=[
            jax.ShapeDtypeStruct((n_qkv // HEAD_DIM, s, HEAD_DIM), BF16),
            jax.ShapeDtypeStruct((s, n - n_qkv), F32),
            jax.ShapeDtypeStruct((N_HEADS, 1, s), F32),
        ],
        scratch_shapes=[pltpu.VMEM((tm, d), BF16), pltpu.VMEM((8, LANES), F32)],
        compiler_params=_params(("arbitrary",)),
        name="inproj",
    )(x, g, mod, w_all, w_tail, w_f, b_f_row)


AUG_TERMS = 3
ONES_ROWS = 16
Q_STRIP = 1024
FAST_UNROLL = 4
LOG2E = math.log2(math.e)


SAFE_EXP = 60.0
ZERO_EXP = -136.0
NORM_SLACK = 1.02


def _tile_lanes(row, n):
    return jnp.concatenate([row] * (n // LANES), axis=1)


def _attn_kernel(q_ref, k_ref, v_ref, ft_ref, g_ref, o_ref,
                 kaug_ref, vt_ref, base_ref, bend_ref, kall_ref, acc_ref, *, tq, tk):
    i = pl.program_id(1)
    seq = k_ref.shape[1]
    dh = HEAD_DIM
    sub = lax.broadcasted_iota(jnp.int32, (LANES, LANES), 0)
    lane_row = lax.broadcasted_iota(jnp.int32, (1, LANES), 1)
    ones_sq = jnp.ones((LANES, LANES), BF16)

    @pl.when(i == 0)
    def _():
        vt_ref[dh:, :] = jnp.ones((ONES_ROWS, seq), BF16)
        kall_ref[...] = jnp.zeros_like(kall_ref)
        bend_ref[...] = jnp.zeros_like(bend_ref)

        def fill(c, carry):
            c_off = pl.multiple_of(c * tk, tk)
            b_row = (-LOG2E) * ft_ref[0, :, pl.ds(c_off, tk)]
            base = jnp.broadcast_to(b_row[:, 0:1], (1, LANES))
            base_ref[pl.ds(c, 1), :] = base
            bend_ref[0:1, :] = jnp.where(
                lane_row == c, jnp.broadcast_to(b_row[:, tk - 1:tk], (1, LANES)),
                bend_ref[0:1, :])
            rel = b_row - _tile_lanes(base, tk)
            hi = rel.astype(BF16).astype(F32)
            mid = (rel - hi).astype(BF16).astype(F32)
            lo = (rel - hi - mid).astype(BF16).astype(F32)
            for bb in range(tk // LANES):
                off = pl.multiple_of(c_off + bb * LANES, LANES)
                cs = slice(bb * LANES, (bb + 1) * LANES)
                terms = jnp.where(sub == 0, hi[:, cs], jnp.where(
                    sub == 1, mid[:, cs], jnp.where(sub == 2, lo[:, cs], 0.0)))
                kb = k_ref[0, pl.ds(off, LANES), :]
                kaug_ref[pl.ds(off, LANES), 0:dh] = kb
                kaug_ref[pl.ds(off, LANES), dh:] = terms.T.astype(BF16)
                vt_ref[0:dh, pl.ds(off, LANES)] = (
                    v_ref[0, pl.ds(off, LANES), :].astype(F32).T.astype(BF16))
                kf = kb.astype(F32)
                n2 = jnp.dot((kf * kf).astype(BF16), ones_sq, preferred_element_type=F32)
                kall_ref[0:1, :] = jnp.maximum(kall_ref[0:1, :],
                                               jnp.max(n2, axis=0, keepdims=True))
            return carry
        lax.fori_loop(0, seq // tk, fill, 0)

    lane_q = lax.broadcasted_iota(jnp.int32, (tq, LANES), 1)
    q = q_ref[0]
    q_aug = jnp.concatenate(
        [q, jnp.where(lane_q < AUG_TERMS, 1.0, 0.0).astype(BF16)], axis=1)
    base_q = base_ref[pl.ds(i, 1), :]
    n_strip = tq // Q_STRIP

    def chunk(j):
        k_off = pl.multiple_of(j * tk, tk)
        kc = kaug_ref[pl.ds(k_off, tk), :]
        vc = vt_ref[:, pl.ds(k_off, tk)]
        delta = base_ref[pl.ds(j, 1), :] - base_q
        return kc, vc, delta

    def scores(kc, lo, hi):
        return lax.dot_general(kc, q_aug[lo:hi, :], (((1,), (1,)), ((), ())),
                               preferred_element_type=F32)

    def online_step(j, m, masked):
        kc, vc, delta = chunk(j)
        delta = _tile_lanes(delta, Q_STRIP)
        m_out = []
        for st in range(n_strip):
            lo, hi = st * Q_STRIP, (st + 1) * Q_STRIP
            s = scores(kc, lo, hi)
            if masked:
                key = lax.broadcasted_iota(jnp.int32, (tk, Q_STRIP), 0)
                qry = lax.broadcasted_iota(jnp.int32, (tk, Q_STRIP), 1) + lo
                s = jnp.where(key <= qry, s, -jnp.inf)
            m_old = m[st] - delta
            m_new = jnp.maximum(m_old, jnp.max(s, axis=0, keepdims=True))
            p = jnp.exp2(s - m_new).astype(BF16)
            corr = jnp.exp2(m_old - m_new)
            pv = jnp.dot(vc, p, preferred_element_type=F32)
            acc_ref[:, lo:hi] = acc_ref[:, lo:hi] * corr + pv
            m_out.append(m_new + delta)
        return tuple(m_out)

    q_off = pl.multiple_of(i * tq, tq)
    ones_row = jnp.ones((8, dh), BF16)
    row_sum = lambda a: lax.dot_general(ones_row, a.astype(BF16), (((1,), (1,)), ((), ())),
                                        preferred_element_type=F32)[0:1, :]
    qf = q.astype(F32)
    qk_self = row_sum(qf * kaug_ref[pl.ds(q_off, tq), 0:dh].astype(F32))
    rel_q = (-LOG2E) * ft_ref[0, :, pl.ds(q_off, tq)] - _tile_lanes(base_q, tq)
    m_row = qk_self + rel_q
    qk_bound = jnp.sqrt(row_sum(qf * qf) * _tile_lanes(kall_ref[0:1, :], tq)) * NORM_SLACK + 1.0
    slack_diag = jnp.max(qk_bound - qk_self)
    slack = jnp.max(qk_bound - m_row)
    live = jnp.logical_and(slack + (bend_ref[0:1, :] - base_q) >= ZERO_EXP, lane_row < i)
    n_live = jnp.sum(live.astype(jnp.int32))

    def one_pass(j):
        kc, vc, delta = chunk(j)
        ref = m_row - _tile_lanes(delta, tq)
        for st in range(n_strip):
            lo, hi = st * Q_STRIP, (st + 1) * Q_STRIP
            x = scores(kc, lo, hi) - ref[:, lo:hi]
            acc_ref[:, lo:hi] += jnp.dot(vc, jnp.exp2(x).astype(BF16),
                                         preferred_element_type=F32)

    def diag_pass():
        kc, vc, _ = chunk(i)
        half = tk // 2
        tri = (lax.broadcasted_iota(jnp.int32, (half, half), 0)
               <= lax.broadcasted_iota(jnp.int32, (half, half), 1))
        x = lax.dot_general(kc[0:half, :], q_aug, (((1,), (1,)), ((), ())),
                            preferred_element_type=F32) - m_row
        x = jnp.concatenate([jnp.where(tri, x[:, 0:half], -jnp.inf), x[:, half:]], axis=1)
        acc_ref[...] = jnp.dot(vc[:, 0:half], jnp.exp2(x).astype(BF16),
                               preferred_element_type=F32)
        x = lax.dot_general(kc[half:, :], q_aug[half:, :], (((1,), (1,)), ((), ())),
                            preferred_element_type=F32) - m_row[:, half:]
        acc_ref[:, half:] += jnp.dot(vc[:, half:], jnp.exp2(jnp.where(tri, x, -jnp.inf)).astype(BF16),
                                     preferred_element_type=F32)

    diag_pass()

    @pl.when(slack_diag <= SAFE_EXP)
    def _():
        def fast_group(jj, carry):
            for u in range(FAST_UNROLL):
                one_pass(i - 1 - u - FAST_UNROLL * jj)
            return carry
        n_group = n_live // FAST_UNROLL
        lax.fori_loop(0, n_group, fast_group, 0)

        def fast_single(jj, carry):
            one_pass(i - 1 - FAST_UNROLL * n_group - jj)
            return carry
        lax.fori_loop(0, n_live - FAST_UNROLL * n_group, fast_single, 0)

    @pl.when(slack_diag > SAFE_EXP)
    def _():
        acc_ref[...] = jnp.zeros_like(acc_ref)
        m0 = tuple(jnp.full((1, Q_STRIP), -jnp.inf, F32) for _ in range(n_strip))
        m_d = online_step(i, m0, True)
        lax.fori_loop(0, i, lambda jj, m: online_step(i - 1 - jj, m, False), m_d)

    y = (acc_ref[0:dh, :] / acc_ref[dh:dh + 1, :]).T
    o_ref[...] = (y * _silu(g_ref[...])).astype(o_ref.dtype)


def _attention(qkv, ft, rest, *, d_attn):
    s = qkv.shape[1]
    tq = _pick(s, 1024)
    nh = d_attn // HEAD_DIM
    kern = functools.partial(_attn_kernel, tq=tq, tk=tq)
    return pl.pallas_call(
        kern,
        grid=(nh, s // tq),
        in_specs=[
            pl.BlockSpec((1, tq, HEAD_DIM), lambda h, i: (h, i, 0)),
            pl.BlockSpec((1, s, HEAD_DIM), lambda h, i: (nh + h, 0, 0)),
            pl.BlockSpec((1, s, HEAD_DIM), lambda h, i: (2 * nh + h, 0, 0)),
            pl.BlockSpec((1, 1, s), lambda h, i: (h, 0, 0)),
            pl.BlockSpec((tq, HEAD_DIM), lambda h, i: (i, h)),
        ],
        out_specs=pl.BlockSpec((tq, HEAD_DIM), lambda h, i: (i, h)),
        out_shape=jax.ShapeDtypeStruct((s, d_attn), BF16),
        scratch_shapes=[
            pltpu.VMEM((s, 2 * HEAD_DIM), BF16),
            pltpu.VMEM((HEAD_DIM + ONES_ROWS, s), BF16),
            pltpu.VMEM((max(s // tq, 8), LANES), F32),
            pltpu.VMEM((8, LANES), F32),
            pltpu.VMEM((8, LANES), F32),
            pltpu.VMEM((HEAD_DIM + ONES_ROWS, tq), F32),
        ],
        compiler_params=_params(("arbitrary", "arbitrary")),
        name="fox_attention",
    )(qkv, qkv, qkv, ft, rest)


def _pool_branch(i, u, prev, gate, w_ref, scale):
    tp = u.shape[0]
    prev = jnp.where(i > 0, prev, 0.0)
    t1 = lax.broadcasted_iota(jnp.int32, (tp, POOL_GROUP), 0) + (i * tp + 1)
    outs = []
    for g, w in enumerate(POOL_WINDOWS):
        lo, hi = g * POOL_GROUP, (g + 1) * POOL_GROUP
        ug = u[:, lo:hi]
        ext = jnp.concatenate([prev[:, lo:hi], ug], axis=0)
        win = ext
        span = 1
        while span < w:
            win = win + pltpu.roll(win, span, axis=0)
            span *= 2
        win = win[POOL_HALO:]
        cnt = jnp.minimum(t1, w).astype(F32)
        pooled = win / cnt - ug
        mixed = jnp.dot(pooled.astype(BF16), w_ref[g], preferred_element_type=F32)
        outs.append(mixed)
    mixed = jnp.concatenate(outs, axis=1) * scale
    return (mixed * _silu(gate)).astype(BF16)


SSM_CHUNK = 8


def _discretise(lr, li, ldt):
    dt = jnp.exp(ldt)
    mag = jnp.exp(lr * dt)
    ab_re = mag * jnp.cos(li * dt)
    ab_im = mag * jnp.sin(li * dt)
    den = lr * lr + li * li
    nr = ab_re - 1.0
    ni = ab_im
    z_re = (nr * lr + ni * li) / den
    z_im = (ni * lr - nr * li) / den
    return ab_re, ab_im, z_re, z_im


def _powers(a_re, a_im, n):
    out = [(jnp.ones_like(a_re), jnp.zeros_like(a_im))]
    for _ in range(n):
        p_re, p_im = out[-1]
        out.append((p_re * a_re - p_im * a_im, p_re * a_im + p_im * a_re))
    return out


def _ssm_prep_kernel(lr_ref, li_ref, ldt_ref, lrc_ref, lic_ref, ldtc_ref,
                     br_ref, bi_ref, cr_ref, ci_ref, t_ref, p_ref, e_ref, al_ref):
    nl = SSM_CHUNK
    cw = br_ref.shape[2]
    a_re, a_im, z_re, z_im = _discretise(lr_ref[0, 0], li_ref[0, 0], ldt_ref[0, 0])
    pw = _powers(a_re, a_im, nl)
    al_ref[0, 0] = jnp.concatenate([pw[nl][0], pw[nl][1]], axis=0)
    br = br_ref[0, 0]
    bi = bi_ref[0, 0]
    bb_re = z_re * br - z_im * bi
    bb_im = z_re * bi + z_im * br
    bb = jnp.concatenate([bb_re, bb_im], axis=1)
    ac_re, ac_im, _, _ = _discretise(lrc_ref[0, 0], lic_ref[0, 0], ldtc_ref[0, 0])
    pwc = _powers(ac_re, ac_im, nl)
    cr = cr_ref[0, 0]
    ci = ci_ref[0, 0]
    ca = [jnp.concatenate([cr * q_re - ci * q_im, -(cr * q_im + ci * q_re)], axis=0)
          for q_re, q_im in pwc]
    kd_all = jnp.dot(bb, jnp.concatenate(ca[:nl], axis=1), preferred_element_type=F32,
                     precision=lax.Precision.HIGHEST).astype(BF16)
    kd = [kd_all[:, d * cw:(d + 1) * cw] for d in range(nl)]
    zero = jnp.zeros((cw, cw), BF16)
    for src in range(nl):
        for dst in range(nl):
            t_ref[0, 0, src * cw:(src + 1) * cw, dst * cw:(dst + 1) * cw] = (
                kd[dst - src] if dst >= src else zero)
        q_re, q_im = pw[nl - 1 - src]
        p_ref[0, 0, src * cw:(src + 1) * cw, :] = jnp.concatenate(
            [bb_re * q_re - bb_im * q_im, bb_re * q_im + bb_im * q_re], axis=1).astype(BF16)
        e_ref[0, 0, :, src * cw:(src + 1) * cw] = ca[src + 1].astype(BF16)


def _ssm_prep(lam_re, lam_im, log_dt, b_re, b_im, c_re, c_im):
    depth, ng, ns = lam_re.shape
    gc = b_re.shape[-1]
    n_slab = ng // SSM_SLAB_GROUPS
    eye = jnp.eye(SSM_SLAB_GROUPS, dtype=F32)
    sw = SSM_SLAB_GROUPS * ns

    def place_b(b):
        b = b.reshape(depth, n_slab, SSM_SLAB_GROUPS, ns, gc).transpose(0, 1, 2, 4, 3)
        return (b[:, :, :, :, None, :] * eye[None, None, :, None, :, None]).reshape(
            depth, n_slab, SSM_SLAB_GROUPS * gc, sw)

    def place_c(c):
        c = c.reshape(depth, n_slab, SSM_SLAB_GROUPS, gc, ns).transpose(0, 1, 2, 4, 3)
        return (c[:, :, :, :, None, :] * eye[None, None, :, None, :, None]).reshape(
            depth, n_slab, sw, SSM_SLAB_GROUPS * gc)

    cw = SSM_SLAB_GROUPS * gc
    nl = SSM_CHUNK
    ldt = jnp.broadcast_to(log_dt[:, :, None], (depth, ng, ns))
    row = lambda v: v.reshape(depth, n_slab, 1, sw)
    col = lambda v: jnp.broadcast_to(v.reshape(depth, n_slab, sw, 1), (depth, n_slab, sw, cw))
    blk = lambda *shape: pl.BlockSpec((1, 1) + shape, lambda l, k: (l, k, 0, 0))
    return pl.pallas_call(
        _ssm_prep_kernel,
        grid=(depth, n_slab),
        in_specs=[blk(1, sw)] * 3 + [blk(sw, cw)] * 3 + [blk(cw, sw)] * 2 + [blk(sw, cw)] * 2,
        out_specs=[blk(nl * cw, nl * cw), blk(nl * cw, 2 * sw), blk(2 * sw, nl * cw),
                   blk(2, sw)],
        out_shape=[
            jax.ShapeDtypeStruct((depth, n_slab, nl * cw, nl * cw), BF16),
            jax.ShapeDtypeStruct((depth, n_slab, nl * cw, 2 * sw), BF16),
            jax.ShapeDtypeStruct((depth, n_slab, 2 * sw, nl * cw), BF16),
            jax.ShapeDtypeStruct((depth, n_slab, 2, sw), F32),
        ],
        compiler_params=_params(("arbitrary", "arbitrary")),
        name="ssm_prep",
    )(row(lam_re), row(lam_im), row(ldt), col(lam_re), col(lam_im), col(ldt),
      place_b(b_re), place_b(b_im), place_c(c_re), place_c(c_im))


def _gelu_tanh(y):
    c = math.sqrt(2.0 / math.pi)
    return 0.5 * y * (1.0 + jnp.tanh(c * (y + 0.044715 * (y * y * y))))


SSM_SLABS_PER_STEP = 2


def _ssm_kernel(*refs):
    per = SSM_SLABS_PER_STEP
    u_refs = refs[:per]
    t_ref, p_ref, e_ref, al_ref, d_ref, y_ref, uc_ref, carry_ref = refs[per:]
    i = pl.program_id(1)
    nl = SSM_CHUNK
    ts, cw = u_refs[0].shape
    rows = ts // nl
    sw = al_ref.shape[3]

    @pl.when(i == 0)
    def _():
        carry_ref[...] = jnp.zeros_like(carry_ref)

    row = lax.broadcasted_iota(jnp.int32, (rows, sw), 0)
    first = row == 0
    vs, yts = [], []
    for k in range(per):
        for tau in range(nl):
            uc_ref[k, :, tau * cw:(tau + 1) * cw] = (
                u_refs[k][pl.ds(tau, rows, stride=nl), :].astype(BF16))
        vs.append(jnp.dot(uc_ref[k], p_ref[0, k], preferred_element_type=F32))
    for k in range(per):
        yts.append(jnp.dot(uc_ref[k], t_ref[0, k], preferred_element_type=F32))
    for k in range(per):
        u_ref = u_refs[k]
        xr = vs[k][:, :sw]
        xi = vs[k][:, sw:]
        a_re = al_ref[0, k, 0:1, :]
        a_im = al_ref[0, k, 1:2, :]
        c_re = carry_ref[k, 0:1, :]
        c_im = carry_ref[k, 1:2, :]
        xr = xr + jnp.where(first, a_re * c_re - a_im * c_im, 0.0)
        xi = xi + jnp.where(first, a_re * c_im + a_im * c_re, 0.0)
        q_re, q_im = a_re, a_im
        d = 1
        while d < rows:
            if d % 8:
                keep = row >= d
                sr = jnp.where(keep, pltpu.roll(xr, d, axis=0), 0.0)
                si = jnp.where(keep, pltpu.roll(xi, d, axis=0), 0.0)
                xr, xi = xr + (q_re * sr - q_im * si), xi + (q_re * si + q_im * sr)
            else:
                sr, si = xr[:rows - d], xi[:rows - d]
                xr, xi = (
                    jnp.concatenate([xr[:d], xr[d:] + (q_re * sr - q_im * si)], axis=0),
                    jnp.concatenate([xi[:d], xi[d:] + (q_re * si + q_im * sr)], axis=0))
            q_re, q_im = q_re * q_re - q_im * q_im, 2.0 * (q_re * q_im)
            d *= 2
        pr = jnp.where(first, c_re, pltpu.roll(xr, 1, axis=0))
        pi = jnp.where(first, c_im, pltpu.roll(xi, 1, axis=0))
        carry_ref[k, 0:1, :] = xr[rows - 1:rows, :]
        carry_ref[k, 1:2, :] = xi[rows - 1:rows, :]
        xp = jnp.concatenate([pr, pi], axis=1).astype(BF16)
        yc = yts[k] + jnp.dot(xp, e_ref[0, k], preferred_element_type=F32)
        for tau in range(nl):
            y_ref[k, pl.ds(tau, rows, stride=nl), :] = (
                yc[:, tau * cw:(tau + 1) * cw]
                + d_ref[:, k * cw:(k + 1) * cw] * u_ref[pl.ds(tau, rows, stride=nl), :])


def _ssm(rest, t_mat, p_mat, e_mat, a_l, d_row, *, layer, col_u):
    s = rest.shape[0]
    n_slab, cw = t_mat.shape[1], d_row.shape[1] // t_mat.shape[1]
    per = SSM_SLABS_PER_STEP
    ts = _pick(s, 4096)
    cu = col_u // cw
    sw = a_l.shape[3]
    mat = lambda m: pl.BlockSpec((1, per) + m.shape[2:], lambda k, i: (layer, k, 0, 0))
    u_spec = lambda n: pl.BlockSpec((ts, cw), lambda k, i: (i, cu + per * k + n))
    return pl.pallas_call(
        _ssm_kernel,
        grid=(n_slab // per, s // ts),
        in_specs=[u_spec(n) for n in range(per)] + [
            mat(t_mat), mat(p_mat), mat(e_mat), mat(a_l),
            pl.BlockSpec((1, per * cw), lambda k, i: (0, k)),
        ],
        out_specs=pl.BlockSpec((per, ts, cw), lambda k, i: (k, i, 0)),
        out_shape=jax.ShapeDtypeStruct((n_slab, s, cw), F32),
        scratch_shapes=[pltpu.VMEM((per, ts // SSM_CHUNK, SSM_CHUNK * cw), BF16),
                        pltpu.VMEM((per, 2, sw), F32)],
        compiler_params=_params(("arbitrary", "arbitrary")),
        name="ssm",
    )(*([rest] * per), t_mat, p_mat, e_mat, a_l, d_row)


def _glu_branch(y_raw, gate, wg_ref, bg):
    y = _gelu_tanh(y_raw)
    z = jnp.dot(y.astype(BF16), wg_ref[...], preferred_element_type=F32) + bg
    return (y * _sigmoid(z) * _silu(gate)).astype(BF16)


def _outproj_kernel(ya_ref, up_ref, prev_ref, gp_ref, ysr_ref, gs_ref, wp_ref, sc_ref,
                    wg_ref, bg_ref, w_ref, x_ref, mod_ref, fg_ref, o_ref, *, final):
    i = pl.program_id(0)
    da = ya_ref.shape[1]
    dp = up_ref.shape[1]
    d = x_ref.shape[1]
    tn = PROJ_TN
    ya = ya_ref[...]
    parts = [jnp.dot(ya, w_ref[0, 0:da, c0:c0 + tn], preferred_element_type=F32)
             for c0 in range(0, d, tn)]
    yp = _pool_branch(i, up_ref[...], prev_ref[...], gp_ref[...], wp_ref, sc_ref[...])
    y_raw = jnp.concatenate([ysr_ref[k] for k in range(ysr_ref.shape[0])], axis=1)
    ys = _glu_branch(y_raw, gs_ref[...], wg_ref, bg_ref[...])
    for n, c0 in enumerate(range(0, d, tn)):
        out = (parts[n]
               + jnp.dot(yp, w_ref[0, da:da + dp, c0:c0 + tn], preferred_element_type=F32)
               + jnp.dot(ys, w_ref[0, da + dp:, c0:c0 + tn], preferred_element_type=F32))
        o_ref[:, c0:c0 + tn] = x_ref[:, c0:c0 + tn] + mod_ref[:, c0:c0 + tn] * out
    if final:
        xn = o_ref[...]
        ms = jnp.mean(xn * xn, axis=-1, keepdims=True)
        o_ref[...] = xn * lax.rsqrt(ms + NORM_EPS) * fg_ref[...]


def _outproj(ya, rest, y_ssm, w_pool, pool_scale, w_glu, b_glu, w_out, x, gate, final_g,
             *, layer, final, col_up, col_gp, col_gs):
    s, d = x.shape
    dm = w_out.shape[1]
    d_pool = pool_scale.shape[1]
    d_ssm = b_glu.shape[1]
    tm = _pick(s, 512)
    halo_blocks = tm // POOL_HALO
    cu, cg, cs = col_up // d_pool, col_gp // d_pool, col_gs // d_ssm
    const = lambda a: pl.BlockSpec(a.shape, lambda i: (0,) * a.ndim)
    return pl.pallas_call(
        functools.partial(_outproj_kernel, final=final),
        grid=(s // tm,),
        in_specs=[
            pl.BlockSpec((tm, ya.shape[1]), lambda i: (i, 0)),
            pl.BlockSpec((tm, d_pool), lambda i: (i, cu)),
            pl.BlockSpec((POOL_HALO, d_pool),
                         lambda i: (jnp.maximum(i * halo_blocks - 1, 0), cu)),
            pl.BlockSpec((tm, d_pool), lambda i: (i, cg)),
            pl.BlockSpec((y_ssm.shape[0], tm, y_ssm.shape[2]), lambda i: (0, i, 0)),
            pl.BlockSpec((tm, d_ssm), lambda i: (i, cs)),
            const(w_pool), const(pool_scale), const(w_glu), const(b_glu),
            pl.BlockSpec((1, dm, d), lambda i: (layer, 0, 0), pipeline_mode=pl.Buffered(1)),
            pl.BlockSpec((tm, d), lambda i: (i, 0)),
            pl.BlockSpec((1, d), lambda i: (0, 0)),
            pl.BlockSpec((1, d), lambda i: (0, 0)),
        ],
        out_specs=pl.BlockSpec((tm, d), lambda i: (i, 0)),
        out_shape=jax.ShapeDtypeStruct((s, d), F32),
        compiler_params=_params(("arbitrary",)),
        name="outproj",
    )(ya, rest, rest, rest, y_ssm, rest, w_pool, pool_scale, w_glu, b_glu, w_out, x, gate,
      final_g)


def kernel(x, c, norm_g, w_ada, b_ada, w_in, b_f, w_pool, pool_scale, lam_re, lam_im,
           ssm_b_re, ssm_b_im, ssm_c_re, ssm_c_im, ssm_d, log_dt, w_glu, b_glu, w_out,
           final_g):
    b, s, d = x.shape
    assert b == 1
    depth = w_in.shape[0]
    d_pool = pool_scale.shape[1]
    d_ssm = b_glu.shape[1]
    d_attn = N_HEADS * HEAD_DIM
    n_f = b_f.shape[1]
    assert n_f == N_HEADS and w_in.shape[2] == 4 * d_attn + n_f + 2 * d_pool + 2 * d_ssm

    w_all = w_in.astype(BF16)
    w_tail = w_all[:, :, 4 * d_attn + n_f:]
    w_f = jnp.pad(w_in[:, :, 4 * d_attn:4 * d_attn + n_f],
                  ((0, 0), (0, 0), (0, LANES - n_f))).astype(BF16)
    b_f_row = jnp.pad(b_f, ((0, 0), (0, LANES - n_f))).reshape(depth, 1, LANES)
    col_up = d_attn
    col_gp = col_up + d_pool
    col_us = col_gp + d_pool
    col_gs = col_us + d_ssm

    mod = _ada_mod(c, w_ada, b_ada).reshape(depth, 3, d)
    t_all, p_all, e_all, al_all = _ssm_prep(lam_re, lam_im, log_dt, ssm_b_re, ssm_b_im,
                                            ssm_c_re, ssm_c_im)
    w_pool_b = w_pool.astype(BF16)
    w_glu_b = w_glu.astype(BF16)
    w_out_b = w_out.astype(BF16)

    xs = x.reshape(s, d)
    for l in range(depth):
        qkv, rest, ft = _inproj(xs, norm_g[l].reshape(1, d), mod[l], w_all, w_tail, w_f[l],
                                b_f_row[l], layer=l, d_attn=d_attn)
        ya = _attention(qkv, ft, rest, d_attn=d_attn)
        y_ssm = _ssm(rest, t_all, p_all, e_all, al_all, ssm_d[l].reshape(1, d_ssm),
                     layer=l, col_u=col_us)
        xs = _outproj(ya, rest, y_ssm, w_pool_b[l], pool_scale[l].reshape(1, d_pool),
                      w_glu_b[l], b_glu[l].reshape(1, d_ssm), w_out_b, xs, mod[l, 2:3, :],
                      final_g.reshape(1, d), layer=l, final=(l == depth - 1),
                      col_up=col_up, col_gp=col_gp, col_gs=col_gs)
    return xs.reshape(b, s, d).astype(x.dtype)
```
